```python
import math
import jax, jax.numpy as jnp
from jax import lax
import numpy as np

D_MODEL = 1024
BATCH = 4
SEQ = 8192
DEPTH = 1

GLA_HEADS = 4
GLA_DK = 64
GLA_DV = 128
GLA_RANK = 16
GLA_GATE_TEMP = 16.0
GLA_CHUNK = 64
GLA_WIDTH = GLA_HEADS * GLA_DV
DIFF_HEADS = 4
DIFF_DH = 64
DIFF_DV = 2 * DIFF_DH
DIFF_WIDTH = DIFF_HEADS * DIFF_DV
Q_BLOCK = 128
MIX_WIDTH = GLA_WIDTH + DIFF_WIDTH
REL_BUCKETS = 32
REL_MAX_DIST = 128
D_FF = 2816
CONV_WIDTH = 3
N_MOD = 6
EPS = 1e-6

IN_SIZES = (
    GLA_HEADS * GLA_DK,
    GLA_HEADS * GLA_DK,
    GLA_WIDTH,
    GLA_WIDTH,
    GLA_RANK,
    GLA_RANK,
    DIFF_HEADS * 2 * DIFF_DH,
    DIFF_HEADS * 2 * DIFF_DH,
    DIFF_WIDTH,
)
IN_TOTAL = sum(IN_SIZES)

kernel_name = "hymba_gla_diffattn_convffn_encoder"


def rms_norm(x, w):
    xf = x.astype(jnp.float32)
    y = xf * lax.rsqrt(jnp.mean(xf * xf, axis=-1, keepdims=True) + EPS)
    return (y * w.astype(jnp.float32)).astype(x.dtype)


def modulate(h, shift, scale):
    return h * (1 + scale[:, None, :]) + shift[:, None, :]


def split_columns(t, sizes):
    outs, start = [], 0
    for s in sizes:
        outs.append(t[..., start:start + s])
        start += s
    return outs


def t5_bucket(rel):
    half = REL_BUCKETS // 2
    max_exact = half // 2
    ret = jnp.where(rel > 0, half, 0)
    n = jnp.abs(rel)
    nf = jnp.maximum(n, 1).astype(jnp.float32)
    large = max_exact + (jnp.log(nf / max_exact) / math.log(REL_MAX_DIST / max_exact)
                         * (half - max_exact)).astype(jnp.int32)
    large = jnp.minimum(large, half - 1)
    return ret + jnp.where(n < max_exact, n, large)


def gla_chunked(q, k, v, log_a, strict):
    B, H, L, DK = q.shape
    DV = v.shape[-1]
    N, C = L // GLA_CHUNK, GLA_CHUNK
    q = q.reshape(B, H, N, C, DK)
    k = k.reshape(B, H, N, C, DK)
    v = v.reshape(B, H, N, C, DV).astype(jnp.float32)
    b = jnp.cumsum(log_a.reshape(B, H, N, C, DK), axis=3)
    b_last = b[..., -1:, :]
    q_in = q * jnp.exp(b)
    k_in = k * jnp.exp(-b)
    k_st = k * jnp.exp(b_last - b)
    mask = jnp.tril(jnp.ones((C, C), jnp.float32), k=-1 if strict else 0)
    att = jnp.einsum('bhncd,bhnsd->bhncs', q_in, k_in) * mask
    o_intra = jnp.einsum('bhncs,bhnsv->bhncv', att, v)
    chunk_update = jnp.einsum('bhnsd,bhnsv->nbhdv', k_st, v)
    chunk_decay = jnp.exp(b_last[..., 0, :]).transpose(2, 0, 1, 3)

    def step(S, inp):
        upd, dec = inp
        return dec[..., None] * S + upd, S

    S0 = jnp.zeros((B, H, DK, DV), jnp.float32)
    _, S_prev = lax.scan(step, S0, (chunk_update, chunk_decay))
    o_inter = jnp.einsum('bhncd,nbhdv->bhncv', q_in, S_prev)
    return (o_intra + o_inter).reshape(B, H, L, DV)


def diff_attention(q, k, v, positions, rel_table, lam):
    B, H, _, L, DH = q.shape
    nb = L // Q_BLOCK
    scale = DH ** -0.5
    qb = q.reshape(B, H, 2, nb, Q_BLOCK, DH).transpose(3, 0, 1, 2, 4, 5)
    pb = positions.reshape(B, nb, Q_BLOCK).transpose(1, 0, 2)

    def block(args):
        q_blk, p_blk = args
        s = jnp.einsum('bhiqd,bhikd->bhiqk', q_blk, k).astype(jnp.float32) * scale
        rel = positions[:, None, :] - p_blk[:, :, None]
        bias = rel_table.astype(jnp.float32)[t5_bucket(rel)]
        s = s + bias.transpose(0, 3, 1, 2)[:, :, None]
        p = jax.nn.softmax(s, axis=-1)
        a = p[:, :, 0] - lam * p[:, :, 1]
        return jnp.einsum('bhqk,bhkv->bhqv', a, v.astype(jnp.float32))

    o = lax.map(block, (qb, pb))
    return o.transpose(1, 2, 0, 3, 4).reshape(B, H, L, -1)


def depthwise_conv3(u, w, bias):
    up = jnp.pad(u, ((0, 0), (1, 1), (0, 0)))
    return w[0] * up[:, :-2] + w[1] * up[:, 1:-1] + w[2] * up[:, 2:] + bias


def setup_inputs(seed: int = 0) -> dict:
    key = jax.random.key(seed)
    ks = jax.random.split(key, 26)
    nrm = lambda k, shape, s: jax.random.normal(k, shape, jnp.float32) * s
    D = D_MODEL
    return {
        "x": nrm(ks[0], (BATCH, SEQ, D), 1.0),
        "c": nrm(ks[1], (BATCH, D), 1.0),
        "positions": (jnp.arange(SEQ, dtype=jnp.int32)[None, :]
                      + jax.random.randint(ks[2], (BATCH, 1), 0, 1024, jnp.int32)),
        "w_ada": nrm(ks[3], (DEPTH, D, N_MOD * D), 0.5 * D ** -0.5),
        "b_ada": nrm(ks[4], (DEPTH, N_MOD * D), 0.02),
        "attn_norm_w": 1.0 + nrm(ks[5], (DEPTH, D), 0.02),
        "w_in": nrm(ks[6], (DEPTH, D, IN_TOTAL), D ** -0.5),
        "gla_dec_w_fwd": nrm(ks[7], (DEPTH, GLA_RANK, GLA_HEADS * GLA_DK), GLA_RANK ** -0.5),
        "gla_dec_b_fwd": nrm(ks[8], (DEPTH, GLA_HEADS * GLA_DK), 0.1),
        "gla_dec_w_bwd": nrm(ks[9], (DEPTH, GLA_RANK, GLA_HEADS * GLA_DK), GLA_RANK ** -0.5),
        "gla_dec_b_bwd": nrm(ks[10], (DEPTH, GLA_HEADS * GLA_DK), 0.1),
        "gla_norm_w": 1.0 + nrm(ks[11], (DEPTH, GLA_WIDTH), 0.02),
        "diff_lambda_q1": nrm(ks[12], (DEPTH, DIFF_DH), 0.1),
        "diff_lambda_k1": nrm(ks[13], (DEPTH, DIFF_DH), 0.1),
        "diff_lambda_q2": nrm(ks[14], (DEPTH, DIFF_DH), 0.1),
        "diff_lambda_k2": nrm(ks[15], (DEPTH, DIFF_DH), 0.1),
        "diff_norm_w": 1.0 + nrm(ks[16], (DEPTH, DIFF_DV), 0.02),
        "rel_bias_table": nrm(ks[17], (REL_BUCKETS, DIFF_HEADS), 0.5),
        "w_out": nrm(ks[18], (DEPTH, MIX_WIDTH, D), MIX_WIDTH ** -0.5),
        "ffn_norm_w": 1.0 + nrm(ks[19], (DEPTH, D), 0.02),
        "w_up": nrm(ks[20], (DEPTH, D, 2 * D_FF), D ** -0.5),
        "conv_w": nrm(ks[21], (DEPTH, CONV_WIDTH, 2 * D_FF), CONV_WIDTH ** -0.5),
        "conv_b": nrm(ks[22], (DEPTH, 2 * D_FF), 0.02),
        "w_down": nrm(ks[23], (DEPTH, D_FF, D), D_FF ** -0.5),
        "final_norm_w": 1.0 + nrm(ks[24], (D,), 0.02),
    }


def reference(x, c, positions, w_ada, b_ada, attn_norm_w, w_in,
              gla_dec_w_fwd, gla_dec_b_fwd, gla_dec_w_bwd, gla_dec_b_bwd, gla_norm_w,
              diff_lambda_q1, diff_lambda_k1, diff_lambda_q2, diff_lambda_k2, diff_norm_w,
              rel_bias_table, w_out, ffn_norm_w, w_up, conv_w, conv_b, w_down, final_norm_w):
    B, L, _ = x.shape
    f32 = jnp.float32
    mod_in = jax.nn.silu(c)

    def heads(t, h):
        return t.reshape(B, L, h, -1).transpose(0, 2, 1, 3)

    for i in range(DEPTH):
        lambda_init = 0.8 - 0.6 * math.exp(-0.3 * i)
        mod = mod_in @ w_ada[i] + b_ada[i]
        sh_a, sc_a, g_a, sh_f, sc_f, g_f = jnp.split(mod, N_MOD, axis=-1)

        h = modulate(rms_norm(x, attn_norm_w[i]), sh_a, sc_a)
        proj = h @ w_in[i]
        q_g, k_g, v_g, r_g, lr_f, lr_b, q_d, k_d, v_d = split_columns(proj, IN_SIZES)

        la_f = jax.nn.log_sigmoid((lr_f @ gla_dec_w_fwd[i] + gla_dec_b_fwd[i]).astype(f32)) / GLA_GATE_TEMP
        la_b = jax.nn.log_sigmoid((lr_b @ gla_dec_w_bwd[i] + gla_dec_b_bwd[i]).astype(f32)) / GLA_GATE_TEMP
        qa = heads(q_g, GLA_HEADS).astype(f32) * GLA_DK ** -0.5
        ka = heads(k_g, GLA_HEADS).astype(f32)
        va = heads(v_g, GLA_HEADS).astype(f32)
        la_f = heads(la_f, GLA_HEADS)
        la_b = heads(la_b, GLA_HEADS)
        o_fwd = gla_chunked(qa, ka, va, la_f, strict=False)
        flip = lambda t: jnp.flip(t, axis=2)
        o_bwd = flip(gla_chunked(flip(qa), flip(ka), flip(va), flip(la_b), strict=True))
        o_a = (o_fwd + o_bwd).transpose(0, 2, 1, 3)
        o_a = rms_norm(o_a, gla_norm_w[i].reshape(GLA_HEADS, GLA_DV)).reshape(B, L, GLA_WIDTH)
        o_a = o_a.astype(x.dtype) * jax.nn.silu(r_g)

        lam = (jnp.exp(jnp.sum(diff_lambda_q1[i] * diff_lambda_k1[i]).astype(f32))
               - jnp.exp(jnp.sum(diff_lambda_q2[i] * diff_lambda_k2[i]).astype(f32))
               + lambda_init)
        qd = q_d.reshape(B, L, DIFF_HEADS, 2, DIFF_DH).transpose(0, 2, 3, 1, 4)
        kd = k_d.reshape(B, L, DIFF_HEADS, 2, DIFF_DH).transpose(0, 2, 3, 1, 4)
        vd = heads(v_d, DIFF_HEADS)
        o_b = diff_attention(qd, kd, vd, positions, rel_bias_table, lam)
        o_b = rms_norm(o_b.transpose(0, 2, 1, 3), diff_norm_w[i]) * (1 - lambda_init)
        o_b = o_b.reshape(B, L, DIFF_WIDTH).astype(x.dtype)

        mixed = jnp.concatenate([o_a, o_b], axis=-1) @ w_out[i]
        x = x + g_a[:, None, :] * mixed

        h2 = modulate(rms_norm(x, ffn_norm_w[i]), sh_f, sc_f)
        u = depthwise_conv3(h2 @ w_up[i], conv_w[i], conv_b[i])
        gate, val = jnp.split(u, 2, axis=-1)
        x = x + g_f[:, None, :] * ((jax.nn.silu(gate) * val) @ w_down[i])

    return rms_norm(x, final_norm_w)
```

```python
import functools
import math

import jax
import jax.numpy as jnp
from jax import lax
from jax.experimental import pallas as pl
from jax.experimental.pallas import tpu as pltpu

F32 = jnp.float32
BF16 = jnp.bfloat16

D_MODEL = 1024
GLA_HEADS = 4
GLA_DK = 64
GLA_DV = 128
GLA_RANK = 16
GLA_GATE_TEMP = 16.0
GLA_CHUNK = 64
GLA_QK = GLA_HEADS * GLA_DK
GLA_WIDTH = GLA_HEADS * GLA_DV
DIFF_HEADS = 4
DIFF_DH = 64
DIFF_DV = 2 * DIFF_DH
DIFF_WIDTH = DIFF_HEADS * DIFF_DV
REL_BUCKETS = 32
D_FF = 2816
N_MOD = 6
EPS = 1e-6
LAMBDA_INIT = 0.8 - 0.6 * math.exp(-0.3 * 0)

BUCKET_STEPS = (1, 2, 3, 4, 5, 6, 7, 8, 12, 16, 23, 32, 46, 64, 91)
FAR_DIST = BUCKET_STEPS[-1]

VMEM_LIMIT = 56 * 1024 * 1024


def _split_bf16(a):
    hi = a.astype(BF16)
    lo = (a - hi.astype(F32)).astype(BF16)
    return hi, lo


def _dot(a, b):
    return jnp.dot(a, b, preferred_element_type=F32)


def _sigmoid(v):
    return 1.0 / (1.0 + jnp.exp(-v))


def _params(*sem):
    return pltpu.CompilerParams(dimension_semantics=sem, vmem_limit_bytes=VMEM_LIMIT)


def _ada_kernel(c_ref, w_ref, b_ref, o_ref):
    c = c_ref[...]
    s_hi, s_lo = _split_bf16(c * _sigmoid(c))
    w_hi, w_lo = _split_bf16(w_ref[...])
    o_ref[...] = (_dot(s_hi, w_hi) + _dot(s_lo, w_hi) + _dot(s_hi, w_lo)) + b_ref[...]


def _ada(c, w, b):
    B, D = c.shape
    N = w.shape[1]
    tn = 1024
    return pl.pallas_call(
        _ada_kernel,
        grid=(N // tn,),
        in_specs=[pl.BlockSpec((B, D), lambda j: (0, 0)),
                  pl.BlockSpec((D, tn), lambda j: (0, j)),
                  pl.BlockSpec((1, tn), lambda j: (0, j))],
        out_specs=pl.BlockSpec((B, tn), lambda j: (0, j)),
        out_shape=jax.ShapeDtypeStruct((B, N), F32),
        compiler_params=_params("arbitrary"),
        name="ada",
    )(c, w, b.reshape(1, N))


def _rms_mod(x, nw, sh, sc):
    ms = jnp.mean(x * x, axis=-1, keepdims=True)
    return (x * lax.rsqrt(ms + EPS) * nw) * (1.0 + sc) + sh


def _inproj_kernel(x_ref, sh_ref, sc_ref, nw_ref, wg_ref, wlr_ref, wd_ref, decw_ref, decb_ref,
                   qg_ref, kg_ref, vg_ref, rg_ref, laf_ref, lab_ref, qd_ref, kd_ref, vd_ref):
    h = _rms_mod(x_ref[...], nw_ref[...], sh_ref[0], sc_ref[0]).astype(BF16)
    g = _dot(h, wg_ref[...])
    qg_ref[...] = g[:, :GLA_QK] * (GLA_DK ** -0.5)
    kg_ref[...] = g[:, GLA_QK:2 * GLA_QK]
    vg_ref[...] = g[:, 2 * GLA_QK:2 * GLA_QK + GLA_WIDTH]
    rg_ref[...] = g[:, 2 * GLA_QK + GLA_WIDTH:]
    lr = _dot(h, wlr_ref[...])
    z = _dot(lr.astype(BF16), decw_ref[...]) + decb_ref[...]
    la = (jnp.minimum(z, 0.0) - jnp.log(1.0 + jnp.exp(-jnp.abs(z)))) * (1.0 / GLA_GATE_TEMP)
    laf_ref[...] = la[:, :GLA_QK]
    lab_ref[...] = la[:, GLA_QK:]
    d = _dot(h, wd_ref[...])
    qd_ref[...] = (d[:, :DIFF_WIDTH] * (DIFF_DH ** -0.5)).astype(BF16)
    kd_ref[...] = d[:, DIFF_WIDTH:2 * DIFF_WIDTH].astype(BF16)
    vd_ref[...] = d[:, 2 * DIFF_WIDTH:].astype(BF16)


def _inproj(x2d, sh, sc, nw, wg, wlr, wd, decw, decb, L, tm):
    T, D = x2d.shape
    per_b = L // tm
    row = lambda i: (i, 0)
    full = lambda i: (0, 0)
    bvec = lambda i: (i // per_b, 0, 0)
    outs = [(GLA_QK, F32), (GLA_QK, F32), (GLA_WIDTH, F32), (GLA_WIDTH, F32), (GLA_QK, F32), (GLA_QK, F32),
            (DIFF_WIDTH, BF16), (DIFF_WIDTH, BF16), (DIFF_WIDTH, BF16)]
    return pl.pallas_call(
        _inproj_kernel,
        grid=(T // tm,),
        in_specs=[pl.BlockSpec((tm, D), row),
                  pl.BlockSpec((1, 1, D), bvec), pl.BlockSpec((1, 1, D), bvec),
                  pl.BlockSpec((1, D), full),
                  pl.BlockSpec(wg.shape, full), pl.BlockSpec(wlr.shape, full), pl.BlockSpec(wd.shape, full),
                  pl.BlockSpec(decw.shape, full), pl.BlockSpec(decb.shape, full)],
        out_specs=[pl.BlockSpec((tm, n), row) for n, _ in outs],
        out_shape=[jax.ShapeDtypeStruct((T, n), dt) for n, dt in outs],
        compiler_params=_params("arbitrary"),
        name="inproj",
    )(x2d, sh, sc, nw, wg, wlr, wd, decw, decb)


def _gla_chunk(q, k, v, la, s_ref, tri, att_mask, ones_col, head_masks, bd_mask, last_row):
    la_hi, la_lo = _split_bf16(la)
    b = _dot(tri, la_hi) + _dot(tri, la_lo)
    tot = b[last_row:last_row + 1, :]
    q_in = (q * jnp.exp(b)).astype(BF16)
    k_in = (k * jnp.exp(-b)).astype(BF16)
    k_st = (k * jnp.exp(tot - b)).astype(BF16)
    v16 = v.astype(BF16)
    s_prev = s_ref[...]
    o = _dot(q_in, s_prev.astype(BF16))
    outs = []
    for hd in range(GLA_HEADS):
        qh = jnp.where(head_masks[hd], q_in, jnp.zeros_like(q_in))
        att = lax.dot_general(qh, k_in, (((1,), (1,)), ((), ())), preferred_element_type=F32)
        att = jnp.where(att_mask, att, 0.0).astype(BF16)
        outs.append(_dot(att, v16[:, hd * GLA_DV:(hd + 1) * GLA_DV]))
    o = o + jnp.concatenate(outs, axis=1)
    upd = lax.dot_general(k_st, v16, (((0,), (0,)), ((), ())), preferred_element_type=F32)
    dn = (((0,), (0,)), ((), ()))
    tot_col = (lax.dot_general(la_hi, ones_col, dn, preferred_element_type=F32)
               + lax.dot_general(la_lo, ones_col, dn, preferred_element_type=F32))
    dec = jnp.exp(jnp.concatenate([tot_col] * GLA_HEADS, axis=1))
    s_ref[...] = dec * s_prev + jnp.where(bd_mask, upd, 0.0)
    return o


def _gla_kernel(qf_ref, kf_ref, vf_ref, laf_ref, qb_ref, kb_ref, vb_ref, lab_ref,
                of_ref, ob_ref, sf_ref, sb_ref, *, n_chunks):
    C = GLA_CHUNK

    @pl.when(pl.program_id(1) == 0)
    def _():
        sf_ref[...] = jnp.zeros_like(sf_ref)
        sb_ref[...] = jnp.zeros_like(sb_ref)

    r = lax.broadcasted_iota(jnp.int32, (C, C), 0)
    s = lax.broadcasted_iota(jnp.int32, (C, C), 1)
    tril = (s <= r)
    triu = (s >= r)
    tri_f = jnp.where(tril, 1.0, 0.0).astype(BF16)
    tri_b = jnp.where(triu, 1.0, 0.0).astype(BF16)
    mask_f = tril
    mask_b = (s > r)
    ones_col = jnp.ones((C, 128), BF16)
    lane = lax.broadcasted_iota(jnp.int32, (C, GLA_QK), 1)
    head_masks = [(lane >= hd * GLA_DK) & (lane < (hd + 1) * GLA_DK) for hd in range(GLA_HEADS)]
    rr = lax.broadcasted_iota(jnp.int32, (GLA_QK, GLA_WIDTH), 0)
    cc = lax.broadcasted_iota(jnp.int32, (GLA_QK, GLA_WIDTH), 1)
    bd_mask = (rr // GLA_DK) == (cc // GLA_DV)

    def body(i, carry):
        f0 = pl.multiple_of(i * C, C)
        of_ref[pl.ds(f0, C), :] = _gla_chunk(
            qf_ref[pl.ds(f0, C), :], kf_ref[pl.ds(f0, C), :], vf_ref[pl.ds(f0, C), :], laf_ref[pl.ds(f0, C), :],
            sf_ref, tri_f, mask_f, ones_col, head_masks, bd_mask, C - 1)
        b0 = pl.multiple_of((n_chunks - 1 - i) * C, C)
        ob_ref[pl.ds(b0, C), :] = _gla_chunk(
            qb_ref[pl.ds(b0, C), :], kb_ref[pl.ds(b0, C), :], vb_ref[pl.ds(b0, C), :], lab_ref[pl.ds(b0, C), :],
            sb_ref, tri_b, mask_b, ones_col, head_masks, bd_mask, 0)
        return carry

    lax.fori_loop(0, n_chunks, body, 0)


def _gla(qg, kg, vg, laf, lab, B, L, tb):
    T = B * L
    J = L // tb
    fwd = lambda b, j: (b * J + j, 0)
    bwd = lambda b, j: (b * J + (J - 1 - j), 0)
    qk = lambda im: pl.BlockSpec((tb, GLA_QK), im)
    vv = lambda im: pl.BlockSpec((tb, GLA_WIDTH), im)
    return pl.pallas_call(
        functools.partial(_gla_kernel, n_chunks=tb // GLA_CHUNK),
        grid=(B, J),
        in_specs=[qk(fwd), qk(fwd), vv(fwd), qk(fwd), qk(bwd), qk(bwd), vv(bwd), qk(bwd)],
        out_specs=[vv(fwd), vv(bwd)],
        out_shape=[jax.ShapeDtypeStruct((T, GLA_WIDTH), F32)] * 2,
        scratch_shapes=[pltpu.VMEM((GLA_QK, GLA_WIDTH), F32)] * 2,
        compiler_params=_params("arbitrary", "arbitrary"),
        name="gla",
    )(qg, kg, vg, laf, qg, kg, vg, lab)


def _bias_from_rel(rel, tab_ref, hd):
    n = jnp.abs(rel)
    half = REL_BUCKETS // 2
    neg = jnp.full(rel.shape, tab_ref[0, hd], F32)
    pos = jnp.full(rel.shape, tab_ref[half, hd], F32)
    for j, t in enumerate(BUCKET_STEPS):
        ge = n >= t
        neg = jnp.where(ge, tab_ref[j + 1, hd], neg)
        pos = jnp.where(ge, tab_ref[half + j + 1, hd], pos)
    return jnp.where(rel > 0, pos, neg)


def _diff_kernel(qlo_ref, qhi_ref, klo_ref, khi_ref, tab_ref,
                 qT_ref, k_ref, vT_ref, posq_ref, posk_ref, lq1_ref, lk1_ref, lq2_ref, lk2_ref, nw_ref,
                 o_ref, m_ref, l_ref, acc_ref, *, tq, tk, n_kt):
    b = pl.program_id(0)
    hd = pl.program_id(1)
    qi = pl.program_id(2)
    half = REL_BUCKETS // 2

    qT = qT_ref[0]
    row = lax.broadcasted_iota(jnp.int32, qT.shape, 0)
    zero = jnp.zeros_like(qT)
    qmaps = (jnp.where(row < DIFF_DH, qT, zero), jnp.where(row >= DIFF_DH, qT, zero))
    q_lo = qlo_ref[b, qi]
    q_hi = qhi_ref[b, qi]
    posq = posq_ref[0]

    m_ref[...] = jnp.full(m_ref.shape, -1e30, F32)
    l_ref[...] = jnp.zeros(l_ref.shape, F32)
    acc_ref[...] = jnp.zeros(acc_ref.shape, F32)

    def step(kt, bias):
        k0 = pl.multiple_of(kt * tk, tk)
        kblk = k_ref[0, pl.ds(k0, tk), :]
        vblk = vT_ref[0, :, pl.ds(k0, tk)]
        for mp in range(2):
            s = _dot(kblk, qmaps[mp]) + bias
            m_old = m_ref[mp]
            m_new = jnp.maximum(m_old, jnp.max(s, axis=0, keepdims=True))
            alpha = jnp.exp(m_old - m_new)
            p = jnp.exp(s - m_new)
            l_ref[mp] = alpha * l_ref[mp] + jnp.sum(p, axis=0, keepdims=True)
            acc_ref[mp] = alpha * acc_ref[mp] + _dot(vblk, p.astype(BF16))
            m_ref[mp] = m_new

    def body(kt, carry):
        k_lo = klo_ref[b, kt]
        k_hi = khi_ref[b, kt]
        right = (k_lo - q_hi) >= FAR_DIST
        left = (q_lo - k_hi) >= FAR_DIST

        @pl.when(right)
        def _():
            step(kt, tab_ref[REL_BUCKETS - 1, hd])

        @pl.when(left)
        def _():
            step(kt, tab_ref[half - 1, hd])

        @pl.when(jnp.logical_not(jnp.logical_or(right, left)))
        def _():
            k0 = pl.multiple_of(kt * tk, tk)
            pk = posk_ref[0, pl.ds(k0, tk), :]
            pk = jnp.concatenate([pk] * (tq // 128), axis=1)
            step(kt, _bias_from_rel(pk - posq, tab_ref, hd))

        return carry

    lax.fori_loop(0, n_kt, body, 0)

    lam = (jnp.exp(jnp.sum(lq1_ref[...] * lk1_ref[...], axis=1, keepdims=True))
           - jnp.exp(jnp.sum(lq2_ref[...] * lk2_ref[...], axis=1, keepdims=True)) + LAMBDA_INIT)
    o = acc_ref[0] / l_ref[0] - lam * (acc_ref[1] / l_ref[1])
    ms = jnp.mean(o * o, axis=0, keepdims=True)
    nw = jnp.concatenate([nw_ref[...]] * (tq // 128), axis=1)
    o = o * lax.rsqrt(ms + EPS) * nw * (1.0 - LAMBDA_INIT)
    o_ref[0] = o.T.astype(o_ref.dtype)


def _diffattn(qT, kd, vT, positions, table, lq1, lk1, lq2, lk2, nw, tq, tk):
    B, _, L = qT.shape
    nq, nk = L // tq, L // tk
    pq = positions.reshape(B, nq, tq)
    pk = positions.reshape(B, nk, tk)
    qlo, qhi = pq.min(axis=2), pq.max(axis=2)
    klo, khi = pk.min(axis=2), pk.max(axis=2)
    posq = positions.reshape(B, 1, L)
    posk = jnp.broadcast_to(positions[:, :, None], (B, L, 128))
    nw_b = jnp.broadcast_to(nw.reshape(DIFF_DV, 1), (DIFF_DV, 128))
    vec = lambda a: a.reshape(1, DIFF_DH)
    full2 = lambda b, h, i, *_: (0, 0)
    grid_spec = pltpu.PrefetchScalarGridSpec(
        num_scalar_prefetch=5,
        grid=(B, DIFF_HEADS, nq),
        in_specs=[pl.BlockSpec((1, DIFF_DV, tq), lambda b, h, i, *_: (b, h, i)),
                  pl.BlockSpec((1, L, DIFF_DV), lambda b, h, i, *_: (b, 0, h)),
                  pl.BlockSpec((1, DIFF_DV, L), lambda b, h, i, *_: (b, h, 0)),
                  pl.BlockSpec((1, 1, tq), lambda b, h, i, *_: (b, 0, i)),
                  pl.BlockSpec((1, L, 128), lambda b, h, i, *_: (b, 0, 0)),
                  pl.BlockSpec((1, DIFF_DH), full2), pl.BlockSpec((1, DIFF_DH), full2),
                  pl.BlockSpec((1, DIFF_DH), full2), pl.BlockSpec((1, DIFF_DH), full2),
                  pl.BlockSpec((DIFF_DV, 128), full2)],
        out_specs=pl.BlockSpec((1, tq, DIFF_DV), lambda b, h, i, *_: (b, i, h)),
        scratch_shapes=[pltpu.VMEM((2, 1, tq), F32), pltpu.VMEM((2, 1, tq), F32),
                        pltpu.VMEM((2, DIFF_DV, tq), F32)],
    )
    return pl.pallas_call(
        functools.partial(_diff_kernel, tq=tq, tk=tk, n_kt=nk),
        grid_spec=grid_spec,
        out_shape=jax.ShapeDtypeStruct((B, L, DIFF_WIDTH), BF16),
        compiler_params=_params("arbitrary", "arbitrary", "arbitrary"),
        name="diffattn",
    )(qlo, qhi, klo, khi, table, qT, kd, vT, posq, posk, vec(lq1), vec(lk1), vec(lq2), vec(lk2), nw_b)


def _outproj_kernel(of_ref, ob_ref, rg_ref, gnw_ref, od_ref, wa_ref, wb_ref, ga_ref, x_ref, o_ref):
    o = of_ref[...] + ob_ref[...]
    parts = []
    for hd in range(GLA_HEADS):
        oh = o[:, hd * GLA_DV:(hd + 1) * GLA_DV]
        ms = jnp.mean(oh * oh, axis=-1, keepdims=True)
        parts.append(oh * lax.rsqrt(ms + EPS))
    r = rg_ref[...]
    oa = jnp.concatenate(parts, axis=1) * gnw_ref[...] * (r * _sigmoid(r))
    mixed = _dot(oa.astype(BF16), wa_ref[...]) + _dot(od_ref[...], wb_ref[...])
    o_ref[...] = x_ref[...] + ga_ref[0] * mixed


def _outproj(of, ob, rg, gnw, od, wa, wb, ga, x2d, L, tm):
    T, D = x2d.shape
    per_b = L // tm
    row = lambda i: (i, 0)
    full = lambda i: (0, 0)
    bvec = lambda i: (i // per_b, 0, 0)
    return pl.pallas_call(
        _outproj_kernel,
        grid=(T // tm,),
        in_specs=[pl.BlockSpec((tm, GLA_WIDTH), row), pl.BlockSpec((tm, GLA_WIDTH), row),
                  pl.BlockSpec((tm, GLA_WIDTH), row), pl.BlockSpec((1, GLA_WIDTH), full),
                  pl.BlockSpec((tm, DIFF_WIDTH), row),
                  pl.BlockSpec(wa.shape, full), pl.BlockSpec(wb.shape, full),
                  pl.BlockSpec((1, 1, D), bvec), pl.BlockSpec((tm, D), row)],
        out_specs=pl.BlockSpec((tm, D), row),
        out_shape=jax.ShapeDtypeStruct((T, D), F32),
        compiler_params=_params("arbitrary"),
        name="outproj",
    )(of, ob, rg, gnw, od, wa, wb, ga, x2d)


HALO = 8
FF_CHUNK = 256


def _ffn_kernel(x_ref, xp_ref, xn_ref, sh_ref, sc_ref, gf_ref, nw_ref, wup_ref, cw_ref, cb_ref, wdn_ref, fw_ref,
                o_ref, acc_ref, *, tm, per_b):
    i = pl.program_id(0)
    first = (i % per_b) == 0
    last = (i % per_b) == per_b - 1
    nw, sh, sc = nw_ref[...], sh_ref[0], sc_ref[0]
    x = x_ref[...]
    h_mid = _rms_mod(x, nw, sh, sc)
    h_prev = jnp.where(first, 0.0, _rms_mod(xp_ref[...], nw, sh, sc))
    h_next = jnp.where(last, 0.0, _rms_mod(xn_ref[...], nw, sh, sc))
    h = jnp.concatenate([h_prev, h_mid, h_next], axis=0).astype(BF16)
    rows = tm + 2 * HALO

    def conv(u, col):
        w = cw_ref[:, col:col + FF_CHUNK]
        up = pltpu.roll(u, 1, 0)[HALO:HALO + tm]
        un = pltpu.roll(u, rows - 1, 0)[HALO:HALO + tm]
        return (w[0:1] * up + w[1:2] * u[HALO:HALO + tm] + w[2:3] * un) + cb_ref[:, col:col + FF_CHUNK]

    for n in range(D_FF // FF_CHUNK):
        cg, cv = n * FF_CHUNK, D_FF + n * FF_CHUNK
        gate = conv(_dot(h, wup_ref[:, cg:cg + FF_CHUNK]), cg)
        val = conv(_dot(h, wup_ref[:, cv:cv + FF_CHUNK]), cv)
        a = (gate * _sigmoid(gate) * val).astype(BF16)
        part = _dot(a, wdn_ref[cg:cg + FF_CHUNK, :])
        if n == 0:
            acc_ref[...] = part
        else:
            acc_ref[...] += part
    y = x + gf_ref[0] * acc_ref[...]
    ms = jnp.mean(y * y, axis=-1, keepdims=True)
    o_ref[...] = y * lax.rsqrt(ms + EPS) * fw_ref[...]


def _ffn(x1, sh, sc, gf, nw, wup, cw, cb, wdn, fw, L, tm):
    T, D = x1.shape
    per_b = L // tm
    hb = tm // HALO
    n_halo = T // HALO
    row = lambda i: (i, 0)
    full = lambda i: (0, 0)
    bvec = lambda i: (i // per_b, 0, 0)
    return pl.pallas_call(
        functools.partial(_ffn_kernel, tm=tm, per_b=per_b),
        grid=(T // tm,),
        in_specs=[pl.BlockSpec((tm, D), row),
                  pl.BlockSpec((HALO, D), lambda i: (jnp.maximum(i * hb - 1, 0), 0)),
                  pl.BlockSpec((HALO, D), lambda i: (jnp.minimum((i + 1) * hb, n_halo - 1), 0)),
                  pl.BlockSpec((1, 1, D), bvec), pl.BlockSpec((1, 1, D), bvec), pl.BlockSpec((1, 1, D), bvec),
                  pl.BlockSpec((1, D), full),
                  pl.BlockSpec(wup.shape, full), pl.BlockSpec(cw.shape, full), pl.BlockSpec(cb.shape, full),
                  pl.BlockSpec(wdn.shape, full), pl.BlockSpec((1, D), full)],
        out_specs=pl.BlockSpec((tm, D), row),
        out_shape=jax.ShapeDtypeStruct((T, D), F32),
        scratch_shapes=[pltpu.VMEM((tm, D), F32)],
        compiler_params=_params("arbitrary"),
        name="ffn",
    )(x1, x1, x1, sh, sc, gf, nw, wup, cw, cb, wdn, fw)


def kernel(x, c, positions, w_ada, b_ada, attn_norm_w, w_in, gla_dec_w_fwd, gla_dec_b_fwd, gla_dec_w_bwd,
           gla_dec_b_bwd, gla_norm_w, diff_lambda_q1, diff_lambda_k1, diff_lambda_q2, diff_lambda_k2,
           diff_norm_w, rel_bias_table, w_out, ffn_norm_w, w_up, conv_w, conv_b, w_down, final_norm_w):
    B, L, D = x.shape
    T = B * L
    tm = min(512, L)
    x2d = x.reshape(T, D)

    mod = _ada(c, w_ada[0], b_ada[0])
    sh_a, sc_a, g_a, sh_f, sc_f, g_f = [m.reshape(B, 1, D) for m in jnp.split(mod, N_MOD, axis=-1)]

    o_lr = 2 * GLA_QK + 2 * GLA_WIDTH
    o_d = o_lr + 2 * GLA_RANK
    w = w_in[0]
    wg = w[:, :o_lr].astype(BF16)
    wlr = jnp.pad(w[:, o_lr:o_d], ((0, 0), (0, 128 - 2 * GLA_RANK))).astype(BF16)
    wd = w[:, o_d:].astype(BF16)
    decw = jnp.zeros((128, 2 * GLA_QK), F32)
    decw = decw.at[:GLA_RANK, :GLA_QK].set(gla_dec_w_fwd[0]).at[GLA_RANK:2 * GLA_RANK, GLA_QK:].set(gla_dec_w_bwd[0])
    decb = jnp.concatenate([gla_dec_b_fwd[0], gla_dec_b_bwd[0]]).reshape(1, 2 * GLA_QK)

    qg, kg, vg, rg, laf, lab, qd, kd, vd = _inproj(
        x2d, sh_a, sc_a, attn_norm_w[0].reshape(1, D), wg, wlr, wd, decw.astype(BF16), decb, L, tm)

    o_f, o_b = _gla(qg, kg, vg, laf, lab, B, L, min(512, L))

    qT = jnp.swapaxes(qd.reshape(B, L, DIFF_WIDTH), 1, 2)
    vT = jnp.swapaxes(vd.reshape(B, L, DIFF_WIDTH), 1, 2)
    o_d = _diffattn(qT, kd.reshape(B, L, DIFF_WIDTH), vT, positions, rel_bias_table,
                    diff_lambda_q1[0], diff_lambda_k1[0], diff_lambda_q2[0], diff_lambda_k2[0],
                    diff_norm_w[0], min(512, L), min(256, L))

    wo = w_out[0].astype(BF16)
    x1 = _outproj(o_f, o_b, rg, gla_norm_w[0].reshape(1, GLA_WIDTH), o_d.reshape(T, DIFF_WIDTH),
                  wo[:GLA_WIDTH], wo[GLA_WIDTH:], g_a, x2d, L, tm)

    out = _ffn(x1, sh_f, sc_f, g_f, ffn_norm_w[0].reshape(1, D), w_up[0].astype(BF16), conv_w[0],
               conv_b[0].reshape(1, 2 * D_FF), w_down[0].astype(BF16), final_norm_w.reshape(1, D), L, tm)
    return out.reshape(B, L, D)
```

```python
import functools
import math

import jax
import jax.numpy as jnp
from jax import lax
from jax.experimental import pallas as pl
from jax.experimental.pallas import tpu as pltpu

F32 = jnp.float32
BF16 = jnp.bfloat16

D_MODEL = 1024
GLA_HEADS = 4
GLA_DK = 64
GLA_DV = 128
GLA_RANK = 16
GLA_GATE_TEMP = 16.0
GLA_CHUNK = 64
GLA_QK = GLA_HEADS * GLA_DK
GLA_WIDTH = GLA_HEADS * GLA_DV
DIFF_HEADS = 4
DIFF_DH = 64
DIFF_DV = 2 * DIFF_DH
DIFF_WIDTH = DIFF_HEADS * DIFF_DV
REL_BUCKETS = 32
D_FF = 2816
N_MOD = 6
EPS = 1e-6
LAMBDA_INIT = 0.8 - 0.6 * math.exp(-0.3 * 0)
LOG2E = 1.4426950408889634

BUCKET_STEPS = (1, 2, 3, 4, 5, 6, 7, 8, 12, 16, 23, 32, 46, 64, 91)
FAR_DIST = BUCKET_STEPS[-1]

VMEM_LIMIT = 56 * 1024 * 1024


def _split_bf16(a):
    hi = a.astype(BF16)
    lo = (a - hi.astype(F32)).astype(BF16)
    return hi, lo


def _dot(a, b):
    return jnp.dot(a, b, preferred_element_type=F32)


def _sigmoid(v):
    return 1.0 / (1.0 + jnp.exp(-v))


def _params(*sem):
    return pltpu.CompilerParams(dimension_semantics=sem, vmem_limit_bytes=VMEM_LIMIT)


def _ada_kernel(c_ref, w_ref, b_ref, o_ref):
    c = c_ref[...]
    s_hi, s_lo = _split_bf16(c * _sigmoid(c))
    w_hi, w_lo = _split_bf16(w_ref[...])
    o_ref[...] = (_dot(s_hi, w_hi) + _dot(s_lo, w_hi) + _dot(s_hi, w_lo)) + b_ref[...]


def _ada(c, w, b):
    B, D = c.shape
    N = w.shape[1]
    tn = 1024
    return pl.pallas_call(
        _ada_kernel,
        grid=(N // tn,),
        in_specs=[pl.BlockSpec((B, D), lambda j: (0, 0)),
                  pl.BlockSpec((D, tn), lambda j: (0, j)),
                  pl.BlockSpec((1, tn), lambda j: (0, j))],
        out_specs=pl.BlockSpec((B, tn), lambda j: (0, j)),
        out_shape=jax.ShapeDtypeStruct((B, N), F32),
        compiler_params=_params("arbitrary"),
        name="ada",
    )(c, w, b.reshape(1, N))


def _rms_mod(x, nw, sh, sc):
    ms = jnp.mean(x * x, axis=-1, keepdims=True)
    return (x * lax.rsqrt(ms + EPS) * nw) * (1.0 + sc) + sh


def _inproj_kernel(x_ref, sh_ref, sc_ref, nw_ref, wg_ref, wlr_ref, wd_ref, decw_ref, decb_ref,
                   qg_ref, kg_ref, vg_ref, rg_ref, laf_ref, lab_ref, qd_ref, kd_ref, vd_ref):
    h = _rms_mod(x_ref[...], nw_ref[...], sh_ref[0], sc_ref[0]).astype(BF16)
    g = _dot(h, wg_ref[...])
    qg_ref[...] = g[:, :GLA_QK] * (GLA_DK ** -0.5)
    kg_ref[...] = g[:, GLA_QK:2 * GLA_QK]
    vg_ref[...] = g[:, 2 * GLA_QK:2 * GLA_QK + GLA_WIDTH]
    rg_ref[...] = g[:, 2 * GLA_QK + GLA_WIDTH:]
    lr = _dot(h, wlr_ref[...])
    z = _dot(lr.astype(BF16), decw_ref[...]) + decb_ref[...]
    la = (jnp.minimum(z, 0.0) - jnp.log(1.0 + jnp.exp(-jnp.abs(z)))) * (1.0 / GLA_GATE_TEMP)
    laf_ref[...] = la[:, :GLA_QK]
    lab_ref[...] = la[:, GLA_QK:]
    d = _dot(h, wd_ref[...])
    qd_ref[...] = (d[:, :DIFF_WIDTH] * (DIFF_DH ** -0.5 * LOG2E)).astype(BF16)
    kd_ref[...] = d[:, DIFF_WIDTH:2 * DIFF_WIDTH].astype(BF16)
    vd_ref[...] = d[:, 2 * DIFF_WIDTH:].astype(BF16)


def _inproj(x2d, sh, sc, nw, wg, wlr, wd, decw, decb, L, tm):
    T, D = x2d.shape
    per_b = L // tm
    row = lambda i: (i, 0)
    full = lambda i: (0, 0)
    bvec = lambda i: (i // per_b, 0, 0)
    outs = [(GLA_QK, F32), (GLA_QK, F32), (GLA_WIDTH, F32), (GLA_WIDTH, F32), (GLA_QK, F32), (GLA_QK, F32),
            (DIFF_WIDTH, BF16), (DIFF_WIDTH, BF16), (DIFF_WIDTH, BF16)]
    return pl.pallas_call(
        _inproj_kernel,
        grid=(T // tm,),
        in_specs=[pl.BlockSpec((tm, D), row),
                  pl.BlockSpec((1, 1, D), bvec), pl.BlockSpec((1, 1, D), bvec),
                  pl.BlockSpec((1, D), full),
                  pl.BlockSpec(wg.shape, full), pl.BlockSpec(wlr.shape, full), pl.BlockSpec(wd.shape, full),
                  pl.BlockSpec(decw.shape, full), pl.BlockSpec(decb.shape, full)],
        out_specs=[pl.BlockSpec((tm, n), row) for n, _ in outs],
        out_shape=[jax.ShapeDtypeStruct((T, n), dt) for n, dt in outs],
        compiler_params=_params("arbitrary"),
        name="inproj",
    )(x2d, sh, sc, nw, wg, wlr, wd, decw, decb)


def _gla_chunk(q, k, v, la, s_ref, tri, att_mask, ones_col, head_masks, bd_mask, last_row):
    la_hi, la_lo = _split_bf16(la)
    b = _dot(tri, la_hi) + _dot(tri, la_lo)
    tot = b[last_row:last_row + 1, :]
    q_in = (q * jnp.exp(b)).astype(BF16)
    k_in = (k * jnp.exp(-b)).astype(BF16)
    k_st = (k * jnp.exp(tot - b)).astype(BF16)
    v16 = v.astype(BF16)
    s_prev = s_ref[...]
    o = _dot(q_in, s_prev.astype(BF16))
    outs = []
    for hd in range(GLA_HEADS):
        qh = jnp.where(head_masks[hd], q_in, jnp.zeros_like(q_in))
        att = lax.dot_general(qh, k_in, (((1,), (1,)), ((), ())), preferred_element_type=F32)
        att = jnp.where(att_mask, att, 0.0).astype(BF16)
        outs.append(_dot(att, v16[:, hd * GLA_DV:(hd + 1) * GLA_DV]))
    o = o + jnp.concatenate(outs, axis=1)
    upd = lax.dot_general(k_st, v16, (((0,), (0,)), ((), ())), preferred_element_type=F32)
    dn = (((0,), (0,)), ((), ()))
    tot_col = (lax.dot_general(la_hi, ones_col, dn, preferred_element_type=F32)
               + lax.dot_general(la_lo, ones_col, dn, preferred_element_type=F32))
    dec = jnp.exp(jnp.concatenate([tot_col] * GLA_HEADS, axis=1))
    s_ref[...] = dec * s_prev + jnp.where(bd_mask, upd, 0.0)
    return o


def _gla_kernel(qf_ref, kf_ref, vf_ref, laf_ref, qb_ref, kb_ref, vb_ref, lab_ref,
                of_ref, ob_ref, sf_ref, sb_ref, *, n_chunks):
    C = GLA_CHUNK

    @pl.when(pl.program_id(1) == 0)
    def _():
        sf_ref[...] = jnp.zeros_like(sf_ref)
        sb_ref[...] = jnp.zeros_like(sb_ref)

    r = lax.broadcasted_iota(jnp.int32, (C, C), 0)
    s = lax.broadcasted_iota(jnp.int32, (C, C), 1)
    tril = (s <= r)
    triu = (s >= r)
    tri_f = jnp.where(tril, 1.0, 0.0).astype(BF16)
    tri_b = jnp.where(triu, 1.0, 0.0).astype(BF16)
    mask_f = tril
    mask_b = (s > r)
    ones_col = jnp.ones((C, 128), BF16)
    lane = lax.broadcasted_iota(jnp.int32, (C, GLA_QK), 1)
    head_masks = [(lane >= hd * GLA_DK) & (lane < (hd + 1) * GLA_DK) for hd in range(GLA_HEADS)]
    rr = lax.broadcasted_iota(jnp.int32, (GLA_QK, GLA_WIDTH), 0)
    cc = lax.broadcasted_iota(jnp.int32, (GLA_QK, GLA_WIDTH), 1)
    bd_mask = (rr // GLA_DK) == (cc // GLA_DV)

    def body(i, carry):
        f0 = pl.multiple_of(i * C, C)
        of_ref[pl.ds(f0, C), :] = _gla_chunk(
            qf_ref[pl.ds(f0, C), :], kf_ref[pl.ds(f0, C), :], vf_ref[pl.ds(f0, C), :], laf_ref[pl.ds(f0, C), :],
            sf_ref, tri_f, mask_f, ones_col, head_masks, bd_mask, C - 1)
        b0 = pl.multiple_of((n_chunks - 1 - i) * C, C)
        ob_ref[pl.ds(b0, C), :] = _gla_chunk(
            qb_ref[pl.ds(b0, C), :], kb_ref[pl.ds(b0, C), :], vb_ref[pl.ds(b0, C), :], lab_ref[pl.ds(b0, C), :],
            sb_ref, tri_b, mask_b, ones_col, head_masks, bd_mask, 0)
        return carry

    lax.fori_loop(0, n_chunks, body, 0)


def _gla(qg, kg, vg, laf, lab, B, L, tb):
    T = B * L
    J = L // tb
    fwd = lambda b, j: (b * J + j, 0)
    bwd = lambda b, j: (b * J + (J - 1 - j), 0)
    qk = lambda im: pl.BlockSpec((tb, GLA_QK), im)
    vv = lambda im: pl.BlockSpec((tb, GLA_WIDTH), im)
    return pl.pallas_call(
        functools.partial(_gla_kernel, n_chunks=tb // GLA_CHUNK),
        grid=(B, J),
        in_specs=[qk(fwd), qk(fwd), vv(fwd), qk(fwd), qk(bwd), qk(bwd), vv(bwd), qk(bwd)],
        out_specs=[vv(fwd), vv(bwd)],
        out_shape=[jax.ShapeDtypeStruct((T, GLA_WIDTH), F32)] * 2,
        scratch_shapes=[pltpu.VMEM((GLA_QK, GLA_WIDTH), F32)] * 2,
        compiler_params=_params("arbitrary", "arbitrary"),
        name="gla",
    )(qg, kg, vg, laf, qg, kg, vg, lab)


def _bias_from_rel(rel, tab_ref, hd):
    n = jnp.abs(rel)
    half = REL_BUCKETS // 2
    neg = jnp.full(rel.shape, tab_ref[0, hd], F32)
    pos = jnp.full(rel.shape, tab_ref[half, hd], F32)
    for j, t in enumerate(BUCKET_STEPS):
        ge = n >= t
        neg = jnp.where(ge, tab_ref[j + 1, hd], neg)
        pos = jnp.where(ge, tab_ref[half + j + 1, hd], pos)
    return jnp.where(rel > 0, pos, neg)


KEY_CHUNK = 128
COL_TILE = 256
CLS_LEFT, CLS_RIGHT, CLS_STRIP0, CLS_GENERIC = 0, 1, 2, 99
STEP_LEFT, STEP_RIGHT, STEP_MIXED = 0, 1, 2


def _strips_kernel(tab_ref, o_ref, *, tq, n_strips):
    hd = pl.program_id(0)
    kk = lax.broadcasted_iota(jnp.int32, (KEY_CHUNK, tq), 0)
    qq = lax.broadcasted_iota(jnp.int32, (KEY_CHUNK, tq), 1)
    for j in range(n_strips):
        o_ref[0, j] = _bias_from_rel(KEY_CHUNK * (j - 1) + kk - qq, tab_ref, hd)


def _strips(table, tq, n_strips):
    H = table.shape[1]
    return pl.pallas_call(
        functools.partial(_strips_kernel, tq=tq, n_strips=n_strips),
        grid_spec=pltpu.PrefetchScalarGridSpec(
            num_scalar_prefetch=1, grid=(H,), in_specs=[],
            out_specs=pl.BlockSpec((1, n_strips, KEY_CHUNK, tq), lambda h, *_: (h, 0, 0, 0))),
        out_shape=jax.ShapeDtypeStruct((H, n_strips, KEY_CHUNK, tq), F32),
        compiler_params=_params("arbitrary"),
        name="bias_strips",
    )(table)


def _diff_kernel(stepcls_ref, chunkcls_ref, tab_ref,
                 qT_ref, k_ref, vT_ref, strip_ref, posq_ref, posk_ref, lq1_ref, lk1_ref, lq2_ref, lk2_ref, nw_ref,
                 o_ref, m_ref, l_ref, acc_ref, sa_ref, sb_ref, bias_ref, *, tq, tk, n_kt, n_strips):
    hd = pl.program_id(1)
    r = pl.program_id(0) * pl.num_programs(2) + pl.program_id(2)
    half = REL_BUCKETS // 2
    n_ct = 2 * tq // COL_TILE
    n_chunk = tk // KEY_CHUNK

    qT = qT_ref[0]
    row = lax.broadcasted_iota(jnp.int32, qT.shape, 0)
    zero = jnp.zeros_like(qT)
    qstack = jnp.concatenate([jnp.where(row < DIFF_DH, qT, zero), jnp.where(row >= DIFF_DH, qT, zero)], axis=1)

    m_ref[...] = jnp.full(m_ref.shape, -1e30, F32)
    l_ref[...] = jnp.zeros(l_ref.shape, F32)
    acc_ref[...] = jnp.zeros(acc_ref.shape, F32)

    def scores_tile(kt, s_ref, j):
        kblk = k_ref[0, pl.ds(pl.multiple_of(kt * tk, tk), tk), :]
        cols = slice(j * COL_TILE, (j + 1) * COL_TILE)
        s_ref[:, cols] = _dot(kblk, qstack[:, cols])

    def scores(kt, s_ref):
        for j in range(n_ct):
            scores_tile(kt, s_ref, j)

    def build_bias(kt):
        posq = posq_ref[0]
        for c in range(n_chunk):
            cc = chunkcls_ref[r, kt * n_chunk + c]
            rows = slice(c * KEY_CHUNK, (c + 1) * KEY_CHUNK)

            @pl.when(cc == CLS_LEFT)
            def _():
                bias_ref[rows, :] = jnp.full((KEY_CHUNK, tq), tab_ref[half - 1, hd], F32)

            @pl.when(cc == CLS_RIGHT)
            def _():
                bias_ref[rows, :] = jnp.full((KEY_CHUNK, tq), tab_ref[REL_BUCKETS - 1, hd], F32)

            @pl.when(jnp.logical_and(cc >= CLS_STRIP0, cc < CLS_STRIP0 + n_strips))
            def _():
                bias_ref[rows, :] = strip_ref[0, cc - CLS_STRIP0]

            @pl.when(cc == CLS_GENERIC)
            def _():
                k0 = pl.multiple_of(kt * tk + c * KEY_CHUNK, KEY_CHUNK)
                pk = posk_ref[0, pl.ds(k0, KEY_CHUNK), :]
                pk = jnp.concatenate([pk] * (tq // 128), axis=1)
                bias_ref[rows, :] = _bias_from_rel(pk - posq, tab_ref, hd)

    def softmax_pv(kt, s_ref, const, s_next):
        vblk = vT_ref[0, :, pl.ds(pl.multiple_of(kt * tk, tk), tk)]
        for j in range(n_ct):
            if s_next is not None:
                scores_tile(kt + 1, s_next, j)
            cols = slice(j * COL_TILE, (j + 1) * COL_TILE)
            s = s_ref[:, cols]
            if const is None:
                q0 = (j * COL_TILE) % tq
                s = s + bias_ref[:, q0:q0 + COL_TILE]
                shift = 0.0
            else:
                shift = const
            m_old = m_ref[:, cols]
            m_new = jnp.maximum(m_old, jnp.max(s, axis=0, keepdims=True) + shift)
            alpha = jnp.exp2(m_old - m_new)
            p = jnp.exp2(s - (m_new - shift))
            l_ref[:, cols] = alpha * l_ref[:, cols] + jnp.sum(p, axis=0, keepdims=True)
            acc_ref[:, cols] = alpha * acc_ref[:, cols] + _dot(vblk, p.astype(BF16))
            m_ref[:, cols] = m_new

    def step(kt, s_cur, s_next):
        cls = stepcls_ref[r, kt]

        @pl.when(cls == STEP_MIXED)
        def _():
            build_bias(kt)
            softmax_pv(kt, s_cur, None, s_next)

        @pl.when(cls != STEP_MIXED)
        def _():
            const = jnp.where(cls == STEP_LEFT, tab_ref[half - 1, hd], tab_ref[REL_BUCKETS - 1, hd])
            softmax_pv(kt, s_cur, const, s_next)

    scores(0, sa_ref)

    def pair(kk, carry):
        step(2 * kk, sa_ref, sb_ref)
        step(2 * kk + 1, sb_ref, sa_ref)
        return carry

    lax.fori_loop(0, n_kt // 2 - 1, pair, 0)
    step(n_kt - 2, sa_ref, sb_ref)
    step(n_kt - 1, sb_ref, None)

    lam = (jnp.exp(jnp.sum(lq1_ref[...] * lk1_ref[...], axis=1, keepdims=True))
           - jnp.exp(jnp.sum(lq2_ref[...] * lk2_ref[...], axis=1, keepdims=True)) + LAMBDA_INIT)
    acc = acc_ref[...]
    l = l_ref[...]
    o = acc[:, :tq] / l[:, :tq] - lam * (acc[:, tq:] / l[:, tq:])
    ms = jnp.mean(o * o, axis=0, keepdims=True)
    nw = jnp.concatenate([nw_ref[...]] * (tq // 128), axis=1)
    o = o * lax.rsqrt(ms + EPS) * nw * (1.0 - LAMBDA_INIT)
    o_ref[0] = o.T.astype(o_ref.dtype)


def _classify(positions, tq, tk, n_strips):
    B, L = positions.shape
    nq, nc = L // tq, L // KEY_CHUNK
    pq = positions.reshape(B, nq, tq)
    pk = positions.reshape(B, nc, KEY_CHUNK)
    qlo, qhi, q0 = pq.min(axis=2), pq.max(axis=2), pq[:, :, 0]
    klo, khi, k0 = pk.min(axis=2), pk.max(axis=2), pk[:, :, 0]
    q_run = jnp.all(pq == q0[:, :, None] + jnp.arange(tq, dtype=positions.dtype), axis=2)
    k_run = jnp.all(pk == k0[:, :, None] + jnp.arange(KEY_CHUNK, dtype=positions.dtype), axis=2)
    left = (qlo[:, :, None] - khi[:, None, :]) >= FAR_DIST
    right = (klo[:, None, :] - qhi[:, :, None]) >= FAR_DIST
    d = k0[:, None, :] - q0[:, :, None]
    j = d // KEY_CHUNK + 1
    strip_ok = (q_run[:, :, None] & k_run[:, None, :] & (d % KEY_CHUNK == 0) & (j >= 0) & (j < n_strips))
    cls = jnp.where(left, CLS_LEFT, jnp.where(right, CLS_RIGHT, jnp.where(strip_ok, CLS_STRIP0 + j, CLS_GENERIC)))
    cls = cls.astype(jnp.int32)
    per_step = cls.reshape(B, nq, L // tk, tk // KEY_CHUNK)
    step = jnp.where(jnp.all(per_step == CLS_LEFT, axis=3), STEP_LEFT,
                     jnp.where(jnp.all(per_step == CLS_RIGHT, axis=3), STEP_RIGHT, STEP_MIXED)).astype(jnp.int32)
    return step.reshape(B * nq, L // tk), cls.reshape(B * nq, nc)


def _diffattn(qT, kd, vT, positions, table, lq1, lk1, lq2, lk2, nw, tq, tk):
    B, _, L = qT.shape
    nq, nk = L // tq, L // tk
    n_strips = tq // KEY_CHUNK + 2
    table2 = table * LOG2E
    strips = _strips(table2, tq, n_strips)
    stepcls, chunkcls = _classify(positions, tq, tk, n_strips)
    posq = positions.reshape(B, 1, L)
    posk = jnp.broadcast_to(positions[:, :, None], (B, L, 128))
    nw_b = jnp.broadcast_to(nw.reshape(DIFF_DV, 1), (DIFF_DV, 128))
    vec = lambda a: a.reshape(1, DIFF_DH)
    full2 = lambda b, h, i, *_: (0, 0)
    grid_spec = pltpu.PrefetchScalarGridSpec(
        num_scalar_prefetch=3,
        grid=(B, DIFF_HEADS, nq),
        in_specs=[pl.BlockSpec((1, DIFF_DV, tq), lambda b, h, i, *_: (b, h, i)),
                  pl.BlockSpec((1, L, DIFF_DV), lambda b, h, i, *_: (b, 0, h)),
                  pl.BlockSpec((1, DIFF_DV, L), lambda b, h, i, *_: (b, h, 0)),
                  pl.BlockSpec((1, n_strips, KEY_CHUNK, tq), lambda b, h, i, *_: (h, 0, 0, 0)),
                  pl.BlockSpec((1, 1, tq), lambda b, h, i, *_: (b, 0, i)),
                  pl.BlockSpec((1, L, 128), lambda b, h, i, *_: (b, 0, 0)),
                  pl.BlockSpec((1, DIFF_DH), full2), pl.BlockSpec((1, DIFF_DH), full2),
                  pl.BlockSpec((1, DIFF_DH), full2), pl.BlockSpec((1, DIFF_DH), full2),
                  pl.BlockSpec((DIFF_DV, 128), full2)],
        out_specs=pl.BlockSpec((1, tq, DIFF_DV), lambda b, h, i, *_: (b, i, h)),
        scratch_shapes=[pltpu.VMEM((1, 2 * tq), F32), pltpu.VMEM((1, 2 * tq), F32),
                        pltpu.VMEM((DIFF_DV, 2 * tq), F32),
                        pltpu.VMEM((tk, 2 * tq), F32), pltpu.VMEM((tk, 2 * tq), F32),
                        pltpu.VMEM((tk, tq), F32)],
    )
    return pl.pallas_call(
        functools.partial(_diff_kernel, tq=tq, tk=tk, n_kt=nk, n_strips=n_strips),
        grid_spec=grid_spec,
        out_shape=jax.ShapeDtypeStruct((B, L, DIFF_WIDTH), BF16),
        compiler_params=_params("arbitrary", "arbitrary", "arbitrary"),
        name="diffattn",
    )(stepcls, chunkcls, table2, qT, kd, vT, strips, posq, posk, vec(lq1), vec(lk1), vec(lq2), vec(lk2), nw_b)


def _outproj_kernel(of_ref, ob_ref, rg_ref, gnw_ref, od_ref, wa_ref, wb_ref, ga_ref, x_ref, o_ref):
    o = of_ref[...] + ob_ref[...]
    parts = []
    for hd in range(GLA_HEADS):
        oh = o[:, hd * GLA_DV:(hd + 1) * GLA_DV]
        ms = jnp.mean(oh * oh, axis=-1, keepdims=True)
        parts.append(oh * lax.rsqrt(ms + EPS))
    r = rg_ref[...]
    oa = jnp.concatenate(parts, axis=1) * gnw_ref[...] * (r * _sigmoid(r))
    mixed = _dot(oa.astype(BF16), wa_ref[...]) + _dot(od_ref[...], wb_ref[...])
    o_ref[...] = x_ref[...] + ga_ref[0] * mixed


def _outproj(of, ob, rg, gnw, od, wa, wb, ga, x2d, L, tm):
    T, D = x2d.shape
    per_b = L // tm
    row = lambda i: (i, 0)
    full = lambda i: (0, 0)
    bvec = lambda i: (i // per_b, 0, 0)
    return pl.pallas_call(
        _outproj_kernel,
        grid=(T // tm,),
        in_specs=[pl.BlockSpec((tm, GLA_WIDTH), row), pl.BlockSpec((tm, GLA_WIDTH), row),
                  pl.BlockSpec((tm, GLA_WIDTH), row), pl.BlockSpec((1, GLA_WIDTH), full),
                  pl.BlockSpec((tm, DIFF_WIDTH), row),
                  pl.BlockSpec(wa.shape, full), pl.BlockSpec(wb.shape, full),
                  pl.BlockSpec((1, 1, D), bvec), pl.BlockSpec((tm, D), row)],
        out_specs=pl.BlockSpec((tm, D), row),
        out_shape=jax.ShapeDtypeStruct((T, D), F32),
        compiler_params=_params("arbitrary"),
        name="outproj",
    )(of, ob, rg, gnw, od, wa, wb, ga, x2d)


HALO = 8
FF_CHUNK = 256


def _ffn_kernel(x_ref, xp_ref, xn_ref, sh_ref, sc_ref, gf_ref, nw_ref, wup_ref, cw_ref, cb_ref, wdn_ref, fw_ref,
                o_ref, acc_ref, *, tm, per_b):
    i = pl.program_id(0)
    first = (i % per_b) == 0
    last = (i % per_b) == per_b - 1
    nw, sh, sc = nw_ref[...], sh_ref[0], sc_ref[0]
    x = x_ref[...]
    h_mid = _rms_mod(x, nw, sh, sc)
    h_prev = jnp.where(first, 0.0, _rms_mod(xp_ref[...], nw, sh, sc))
    h_next = jnp.where(last, 0.0, _rms_mod(xn_ref[...], nw, sh, sc))
    h = jnp.concatenate([h_prev, h_mid, h_next], axis=0).astype(BF16)
    rows = tm + 2 * HALO

    def conv(u, col):
        w = cw_ref[:, col:col + FF_CHUNK]
        up = pltpu.roll(u, 1, 0)[HALO:HALO + tm]
        un = pltpu.roll(u, rows - 1, 0)[HALO:HALO + tm]
        return (w[0:1] * up + w[1:2] * u[HALO:HALO + tm] + w[2:3] * un) + cb_ref[:, col:col + FF_CHUNK]

    for n in range(D_FF // FF_CHUNK):
        cg, cv = n * FF_CHUNK, D_FF + n * FF_CHUNK
        gate = conv(_dot(h, wup_ref[:, cg:cg + FF_CHUNK]), cg)
        val = conv(_dot(h, wup_ref[:, cv:cv + FF_CHUNK]), cv)
        a = (gate * _sigmoid(gate) * val).astype(BF16)
        part = _dot(a, wdn_ref[cg:cg + FF_CHUNK, :])
        if n == 0:
            acc_ref[...] = part
        else:
            acc_ref[...] += part
    y = x + gf_ref[0] * acc_ref[...]
    ms = jnp.mean(y * y, axis=-1, keepdims=True)
    o_ref[...] = y * lax.rsqrt(ms + EPS) * fw_ref[...]


def _ffn(x1, sh, sc, gf, nw, wup, cw, cb, wdn, fw, L, tm):
    T, D = x1.shape
    per_b = L // tm
    hb = tm // HALO
    n_halo = T // HALO
    row = lambda i: (i, 0)
    full = lambda i: (0, 0)
    bvec = lambda i: (i // per_b, 0, 0)
    return pl.pallas_call(
        functools.partial(_ffn_kernel, tm=tm, per_b=per_b),
        grid=(T // tm,),
        in_specs=[pl.BlockSpec((tm, D), row),
                  pl.BlockSpec((HALO, D), lambda i: (jnp.maximum(i * hb - 1, 0), 0)),
                  pl.BlockSpec((HALO, D), lambda i: (jnp.minimum((i + 1) * hb, n_halo - 1), 0)),
                  pl.BlockSpec((1, 1, D), bvec), pl.BlockSpec((1, 1, D), bvec), pl.BlockSpec((1, 1, D), bvec),
                  pl.BlockSpec((1, D), full),
                  pl.BlockSpec(wup.shape, full), pl.BlockSpec(cw.shape, full), pl.BlockSpec(cb.shape, full),
                  pl.BlockSpec(wdn.shape, full), pl.BlockSpec((1, D), full)],
        out_specs=pl.BlockSpec((tm, D), row),
        out_shape=jax.ShapeDtypeStruct((T, D), F32),
        scratch_shapes=[pltpu.VMEM((tm, D), F32)],
        compiler_params=_params("arbitrary"),
        name="ffn",
    )(x1, x1, x1, sh, sc, gf, nw, wup, cw, cb, wdn, fw)


def kernel(x, c, positions, w_ada, b_ada, attn_norm_w, w_in, gla_dec_w_fwd, gla_dec_b_fwd, gla_dec_w_bwd,
           gla_dec_b_bwd, gla_norm_w, diff_lambda_q1, diff_lambda_k1, diff_lambda_q2, diff_lambda_k2,
           diff_norm_w, rel_bias_table, w_out, ffn_norm_w, w_up, conv_w, conv_b, w_down, final_norm_w):
    B, L, D = x.shape
    T = B * L
    tm = min(512, L)
    x2d = x.reshape(T, D)

    mod = _ada(c, w_ada[0], b_ada[0])
    sh_a, sc_a, g_a, sh_f, sc_f, g_f = [m.reshape(B, 1, D) for m in jnp.split(mod, N_MOD, axis=-1)]

    o_lr = 2 * GLA_QK + 2 * GLA_WIDTH
    o_d = o_lr + 2 * GLA_RANK
    w = w_in[0]
    wg = w[:, :o_lr].astype(BF16)
    wlr = jnp.pad(w[:, o_lr:o_d], ((0, 0), (0, 128 - 2 * GLA_RANK))).astype(BF16)
    wd = w[:, o_d:].astype(BF16)
    decw = jnp.zeros((128, 2 * GLA_QK), F32)
    decw = decw.at[:GLA_RANK, :GLA_QK].set(gla_dec_w_fwd[0]).at[GLA_RANK:2 * GLA_RANK, GLA_QK:].set(gla_dec_w_bwd[0])
    decb = jnp.concatenate([gla_dec_b_fwd[0], gla_dec_b_bwd[0]]).reshape(1, 2 * GLA_QK)

    qg, kg, vg, rg, laf, lab, qd, kd, vd = _inproj(
        x2d, sh_a, sc_a, attn_norm_w[0].reshape(1, D), wg, wlr, wd, decw.astype(BF16), decb, L, tm)

    o_f, o_b = _gla(qg, kg, vg, laf, lab, B, L, min(512, L))

    qT = jnp.swapaxes(qd.reshape(B, L, DIFF_WIDTH), 1, 2)
    vT = jnp.swapaxes(vd.reshape(B, L, DIFF_WIDTH), 1, 2)
    o_d = _diffattn(qT, kd.reshape(B, L, DIFF_WIDTH), vT, positions, rel_bias_table,
                    diff_lambda_q1[0], diff_lambda_k1[0], diff_lambda_q2[0], diff_lambda_k2[0],
                    diff_norm_w[0], min(512, L), min(512, L // 2))

    wo = w_out[0].astype(BF16)
    x1 = _outproj(o_f, o_b, rg, gla_norm_w[0].reshape(1, GLA_WIDTH), o_d.reshape(T, DIFF_WIDTH),
                  wo[:GLA_WIDTH], wo[GLA_WIDTH:], g_a, x2d, L, tm)

    out = _ffn(x1, sh_f, sc_f, g_f, ffn_norm_w[0].reshape(1, D), w_up[0].astype(BF16), conv_w[0],
               conv_b[0].reshape(1, 2 * D_FF), w_down[0].astype(BF16), final_norm_w.reshape(1, D), L, tm)
    return out.reshape(B, L, D)
```

```python
import functools
import math

import jax
import jax.numpy as jnp
from jax import lax
from jax.experimental import pallas as pl
from jax.experimental.pallas import tpu as pltpu

F32 = jnp.float32
BF16 = jnp.bfloat16

D_MODEL = 1024
GLA_HEADS = 4
GLA_DK = 64
GLA_DV = 128
GLA_RANK = 16
GLA_GATE_TEMP = 16.0
GLA_CHUNK = 64
GLA_QK = GLA_HEADS * GLA_DK
GLA_WIDTH = GLA_HEADS * GLA_DV
DIFF_HEADS = 4
DIFF_DH = 64
DIFF_DV = 2 * DIFF_DH
DIFF_WIDTH = DIFF_HEADS * DIFF_DV
REL_BUCKETS = 32
D_FF = 2816
N_MOD = 6
EPS = 1e-6
LAMBDA_INIT = 0.8 - 0.6 * math.exp(-0.3 * 0)
LOG2E = 1.4426950408889634

BUCKET_STEPS = (1, 2, 3, 4, 5, 6, 7, 8, 12, 16, 23, 32, 46, 64, 91)
FAR_DIST = BUCKET_STEPS[-1]

VMEM_LIMIT = 56 * 1024 * 1024


def _split_bf16(a):
    hi = a.astype(BF16)
    lo = (a - hi.astype(F32)).astype(BF16)
    return hi, lo


def _dot(a, b):
    return jnp.dot(a, b, preferred_element_type=F32)


def _sigmoid(v):
    return 1.0 / (1.0 + jnp.exp(-v))


def _params(*sem):
    return pltpu.CompilerParams(dimension_semantics=sem, vmem_limit_bytes=VMEM_LIMIT)


def _ada_kernel(c_ref, w_ref, b_ref, o_ref):
    c = c_ref[...]
    s_hi, s_lo = _split_bf16(c * _sigmoid(c))
    w_hi, w_lo = _split_bf16(w_ref[...])
    o_ref[...] = (_dot(s_hi, w_hi) + _dot(s_lo, w_hi) + _dot(s_hi, w_lo)) + b_ref[...]


def _ada(c, w, b):
    B, D = c.shape
    N = w.shape[1]
    tn = 1024
    return pl.pallas_call(
        _ada_kernel,
        grid=(N // tn,),
        in_specs=[pl.BlockSpec((B, D), lambda j: (0, 0)),
                  pl.BlockSpec((D, tn), lambda j: (0, j)),
                  pl.BlockSpec((1, tn), lambda j: (0, j))],
        out_specs=pl.BlockSpec((B, tn), lambda j: (0, j)),
        out_shape=jax.ShapeDtypeStruct((B, N), F32),
        compiler_params=_params("arbitrary"),
        name="ada",
    )(c, w, b.reshape(1, N))


def _rms_mod(x, nw, sh, sc):
    ms = jnp.mean(x * x, axis=-1, keepdims=True)
    return (x * lax.rsqrt(ms + EPS) * nw) * (1.0 + sc) + sh


def _inproj_kernel(x_ref, sh_ref, sc_ref, nw_ref, wg_ref, wlr_ref, wd_ref, decw_ref, decb_ref,
                   qg_ref, kg_ref, vg_ref, rg_ref, laf_ref, lab_ref, qd_ref, kd_ref, vd_ref):
    h = _rms_mod(x_ref[...], nw_ref[...], sh_ref[0], sc_ref[0]).astype(BF16)
    g = _dot(h, wg_ref[...])
    qg_ref[...] = g[:, :GLA_QK] * (GLA_DK ** -0.5)
    kg_ref[...] = g[:, GLA_QK:2 * GLA_QK]
    vg_ref[...] = g[:, 2 * GLA_QK:2 * GLA_QK + GLA_WIDTH]
    rg_ref[...] = g[:, 2 * GLA_QK + GLA_WIDTH:]
    lr = _dot(h, wlr_ref[...])
    z = _dot(lr.astype(BF16), decw_ref[...]) + decb_ref[...]
    la = (jnp.minimum(z, 0.0) - jnp.log(1.0 + jnp.exp(-jnp.abs(z)))) * (1.0 / GLA_GATE_TEMP)
    laf_ref[...] = la[:, :GLA_QK]
    lab_ref[...] = la[:, GLA_QK:]
    d = _dot(h, wd_ref[...])
    qd_ref[...] = (d[:, :DIFF_WIDTH] * (DIFF_DH ** -0.5 * LOG2E)).astype(BF16)
    kd_ref[...] = d[:, DIFF_WIDTH:2 * DIFF_WIDTH].astype(BF16)
    vd_ref[...] = d[:, 2 * DIFF_WIDTH:].astype(BF16)


def _inproj(x2d, sh, sc, nw, wg, wlr, wd, decw, decb, L, tm):
    T, D = x2d.shape
    per_b = L // tm
    row = lambda i: (i, 0)
    full = lambda i: (0, 0)
    bvec = lambda i: (i // per_b, 0, 0)
    outs = [(GLA_QK, F32), (GLA_QK, F32), (GLA_WIDTH, F32), (GLA_WIDTH, F32), (GLA_QK, F32), (GLA_QK, F32),
            (DIFF_WIDTH, BF16), (DIFF_WIDTH, BF16), (DIFF_WIDTH, BF16)]
    return pl.pallas_call(
        _inproj_kernel,
        grid=(T // tm,),
        in_specs=[pl.BlockSpec((tm, D), row),
                  pl.BlockSpec((1, 1, D), bvec), pl.BlockSpec((1, 1, D), bvec),
                  pl.BlockSpec((1, D), full),
                  pl.BlockSpec(wg.shape, full), pl.BlockSpec(wlr.shape, full), pl.BlockSpec(wd.shape, full),
                  pl.BlockSpec(decw.shape, full), pl.BlockSpec(decb.shape, full)],
        out_specs=[pl.BlockSpec((tm, n), row) for n, _ in outs],
        out_shape=[jax.ShapeDtypeStruct((T, n), dt) for n, dt in outs],
        compiler_params=_params("arbitrary"),
        name="inproj",
    )(x2d, sh, sc, nw, wg, wlr, wd, decw, decb)


def _gla_chunk(q, k, v, la, s_ref, tri, att_mask, ones_col, head_masks, bd_mask, last_row):
    la_hi, la_lo = _split_bf16(la)
    b = _dot(tri, la_hi) + _dot(tri, la_lo)
    tot = b[last_row:last_row + 1, :]
    q_in = (q * jnp.exp(b)).astype(BF16)
    k_in = (k * jnp.exp(-b)).astype(BF16)
    k_st = (k * jnp.exp(tot - b)).astype(BF16)
    v16 = v.astype(BF16)
    s_prev = s_ref[...]
    o = _dot(q_in, s_prev.astype(BF16))
    outs = []
    for hd in range(GLA_HEADS):
        qh = jnp.where(head_masks[hd], q_in, jnp.zeros_like(q_in))
        att = lax.dot_general(qh, k_in, (((1,), (1,)), ((), ())), preferred_element_type=F32)
        att = jnp.where(att_mask, att, 0.0).astype(BF16)
        outs.append(_dot(att, v16[:, hd * GLA_DV:(hd + 1) * GLA_DV]))
    o = o + jnp.concatenate(outs, axis=1)
    upd = lax.dot_general(k_st, v16, (((0,), (0,)), ((), ())), preferred_element_type=F32)
    dn = (((0,), (0,)), ((), ()))
    tot_col = (lax.dot_general(la_hi, ones_col, dn, preferred_element_type=F32)
               + lax.dot_general(la_lo, ones_col, dn, preferred_element_type=F32))
    dec = jnp.exp(jnp.concatenate([tot_col] * GLA_HEADS, axis=1))
    s_ref[...] = dec * s_prev + jnp.where(bd_mask, upd, 0.0)
    return o


def _gla_kernel(qf_ref, kf_ref, vf_ref, laf_ref, qb_ref, kb_ref, vb_ref, lab_ref,
                of_ref, ob_ref, sf_ref, sb_ref, *, n_chunks):
    C = GLA_CHUNK

    @pl.when(pl.program_id(1) == 0)
    def _():
        sf_ref[...] = jnp.zeros_like(sf_ref)
        sb_ref[...] = jnp.zeros_like(sb_ref)

    r = lax.broadcasted_iota(jnp.int32, (C, C), 0)
    s = lax.broadcasted_iota(jnp.int32, (C, C), 1)
    tril = (s <= r)
    triu = (s >= r)
    tri_f = jnp.where(tril, 1.0, 0.0).astype(BF16)
    tri_b = jnp.where(triu, 1.0, 0.0).astype(BF16)
    mask_f = tril
    mask_b = (s > r)
    ones_col = jnp.ones((C, 128), BF16)
    lane = lax.broadcasted_iota(jnp.int32, (C, GLA_QK), 1)
    head_masks = [(lane >= hd * GLA_DK) & (lane < (hd + 1) * GLA_DK) for hd in range(GLA_HEADS)]
    rr = lax.broadcasted_iota(jnp.int32, (GLA_QK, GLA_WIDTH), 0)
    cc = lax.broadcasted_iota(jnp.int32, (GLA_QK, GLA_WIDTH), 1)
    bd_mask = (rr // GLA_DK) == (cc // GLA_DV)

    def body(i, carry):
        f0 = pl.multiple_of(i * C, C)
        of_ref[pl.ds(f0, C), :] = _gla_chunk(
            qf_ref[pl.ds(f0, C), :], kf_ref[pl.ds(f0, C), :], vf_ref[pl.ds(f0, C), :], laf_ref[pl.ds(f0, C), :],
            sf_ref, tri_f, mask_f, ones_col, head_masks, bd_mask, C - 1)
        b0 = pl.multiple_of((n_chunks - 1 - i) * C, C)
        ob_ref[pl.ds(b0, C), :] = _gla_chunk(
            qb_ref[pl.ds(b0, C), :], kb_ref[pl.ds(b0, C), :], vb_ref[pl.ds(b0, C), :], lab_ref[pl.ds(b0, C), :],
            sb_ref, tri_b, mask_b, ones_col, head_masks, bd_mask, 0)
        return carry

    lax.fori_loop(0, n_chunks, body, 0)


def _gla(qg, kg, vg, laf, lab, B, L, tb):
    T = B * L
    J = L // tb
    fwd = lambda b, j: (b * J + j, 0)
    bwd = lambda b, j: (b * J + (J - 1 - j), 0)
    qk = lambda im: pl.BlockSpec((tb, GLA_QK), im)
    vv = lambda im: pl.BlockSpec((tb, GLA_WIDTH), im)
    return pl.pallas_call(
        functools.partial(_gla_kernel, n_chunks=tb // GLA_CHUNK),
        grid=(B, J),
        in_specs=[qk(fwd), qk(fwd), vv(fwd), qk(fwd), qk(bwd), qk(bwd), vv(bwd), qk(bwd)],
        out_specs=[vv(fwd), vv(bwd)],
        out_shape=[jax.ShapeDtypeStruct((T, GLA_WIDTH), F32)] * 2,
        scratch_shapes=[pltpu.VMEM((GLA_QK, GLA_WIDTH), F32)] * 2,
        compiler_params=_params("arbitrary", "arbitrary"),
        name="gla",
    )(qg, kg, vg, laf, qg, kg, vg, lab)


def _bias_from_rel(rel, tab_ref, hd):
    n = jnp.abs(rel)
    half = REL_BUCKETS // 2
    neg = jnp.full(rel.shape, tab_ref[0, hd], F32)
    pos = jnp.full(rel.shape, tab_ref[half, hd], F32)
    for j, t in enumerate(BUCKET_STEPS):
        ge = n >= t
        neg = jnp.where(ge, tab_ref[j + 1, hd], neg)
        pos = jnp.where(ge, tab_ref[half + j + 1, hd], pos)
    return jnp.where(rel > 0, pos, neg)


KEY_CHUNK = 128
COL_TILE = 256
CLS_LEFT, CLS_RIGHT, CLS_STRIP0, CLS_GENERIC = 0, 1, 2, 99
STEP_LEFT, STEP_RIGHT, STEP_MIXED = 0, 1, 2


def _strips_kernel(tab_ref, o_ref, *, tq, n_strips):
    hd = pl.program_id(0)
    kk = lax.broadcasted_iota(jnp.int32, (KEY_CHUNK, tq), 0)
    qq = lax.broadcasted_iota(jnp.int32, (KEY_CHUNK, tq), 1)
    for j in range(n_strips):
        o_ref[0, j] = _bias_from_rel(KEY_CHUNK * (j - 1) + kk - qq, tab_ref, hd)


def _strips(table, tq, n_strips):
    H = table.shape[1]
    return pl.pallas_call(
        functools.partial(_strips_kernel, tq=tq, n_strips=n_strips),
        grid_spec=pltpu.PrefetchScalarGridSpec(
            num_scalar_prefetch=1, grid=(H,), in_specs=[],
            out_specs=pl.BlockSpec((1, n_strips, KEY_CHUNK, tq), lambda h, *_: (h, 0, 0, 0))),
        out_shape=jax.ShapeDtypeStruct((H, n_strips, KEY_CHUNK, tq), F32),
        compiler_params=_params("arbitrary"),
        name="bias_strips",
    )(table)


def _diff_kernel(stepcls_ref, chunkcls_ref, tab_ref,
                 qT_ref, k_ref, vT_ref, strip_ref, posq_ref, posk_ref, lq1_ref, lk1_ref, lq2_ref, lk2_ref, nw_ref,
                 o_ref, m_ref, l_ref, acc_ref, sa_ref, sb_ref, smaxa_ref, smaxb_ref, bias_ref,
                 *, tq, tk, n_kt, n_strips):
    hd = pl.program_id(1)
    r = pl.program_id(0) * pl.num_programs(2) + pl.program_id(2)
    half = REL_BUCKETS // 2
    n_ct = 2 * tq // COL_TILE
    n_chunk = tk // KEY_CHUNK

    qT = qT_ref[0]
    row = lax.broadcasted_iota(jnp.int32, qT.shape, 0)
    zero = jnp.zeros_like(qT)
    qstack = jnp.concatenate([jnp.where(row < DIFF_DH, qT, zero), jnp.where(row >= DIFF_DH, qT, zero)], axis=1)

    m_ref[...] = jnp.full(m_ref.shape, -1e30, F32)
    l_ref[...] = jnp.zeros(l_ref.shape, F32)
    acc_ref[...] = jnp.zeros(acc_ref.shape, F32)

    bias_max = functools.reduce(jnp.maximum, [tab_ref[i, hd] for i in range(REL_BUCKETS)])

    def scores_tile(kt, buf, j):
        s_ref, smax_ref = buf
        kblk = k_ref[0, pl.ds(pl.multiple_of(kt * tk, tk), tk), :]
        cols = slice(j * COL_TILE, (j + 1) * COL_TILE)
        s = _dot(kblk, qstack[:, cols])
        s_ref[:, cols] = s
        smax_ref[:, cols] = jnp.max(s, axis=0, keepdims=True)

    def scores(kt, buf):
        for j in range(n_ct):
            scores_tile(kt, buf, j)

    def build_bias(kt):
        posq = posq_ref[0]
        for c in range(n_chunk):
            cc = chunkcls_ref[r, kt * n_chunk + c]
            rows = slice(c * KEY_CHUNK, (c + 1) * KEY_CHUNK)

            @pl.when(cc == CLS_LEFT)
            def _():
                bias_ref[rows, :] = jnp.full((KEY_CHUNK, tq), tab_ref[half - 1, hd], F32)

            @pl.when(cc == CLS_RIGHT)
            def _():
                bias_ref[rows, :] = jnp.full((KEY_CHUNK, tq), tab_ref[REL_BUCKETS - 1, hd], F32)

            @pl.when(jnp.logical_and(cc >= CLS_STRIP0, cc < CLS_STRIP0 + n_strips))
            def _():
                bias_ref[rows, :] = strip_ref[0, cc - CLS_STRIP0]

            @pl.when(cc == CLS_GENERIC)
            def _():
                k0 = pl.multiple_of(kt * tk + c * KEY_CHUNK, KEY_CHUNK)
                pk = posk_ref[0, pl.ds(k0, KEY_CHUNK), :]
                pk = jnp.concatenate([pk] * (tq // 128), axis=1)
                bias_ref[rows, :] = _bias_from_rel(pk - posq, tab_ref, hd)

    def softmax_pv(kt, cur, const, nxt):
        s_ref, smax_ref = cur
        vblk = vT_ref[0, :, pl.ds(pl.multiple_of(kt * tk, tk), tk)]
        for j in range(n_ct):
            if nxt is not None:
                scores_tile(kt + 1, nxt, j)
            cols = slice(j * COL_TILE, (j + 1) * COL_TILE)
            shift = bias_max if const is None else const
            m_old = m_ref[:, cols]
            m_new = jnp.maximum(m_old, smax_ref[:, cols] + shift)
            alpha = jnp.exp2(m_old - m_new)
            if const is None:
                q0 = (j * COL_TILE) % tq
                p = jnp.exp2((s_ref[:, cols] + bias_ref[:, q0:q0 + COL_TILE]) - m_new)
            else:
                p = jnp.exp2(s_ref[:, cols] - (m_new - shift))
            l_ref[:, cols] = alpha * l_ref[:, cols] + jnp.sum(p, axis=0, keepdims=True)
            acc_ref[:, cols] = alpha * acc_ref[:, cols] + _dot(vblk, p.astype(BF16))
            m_ref[:, cols] = m_new

    def step(kt, cur, nxt):
        cls = stepcls_ref[r, kt]

        @pl.when(cls == STEP_MIXED)
        def _():
            build_bias(kt)
            softmax_pv(kt, cur, None, nxt)

        @pl.when(cls != STEP_MIXED)
        def _():
            const = jnp.where(cls == STEP_LEFT, tab_ref[half - 1, hd], tab_ref[REL_BUCKETS - 1, hd])
            softmax_pv(kt, cur, const, nxt)

    buf_a = (sa_ref, smaxa_ref)
    buf_b = (sb_ref, smaxb_ref)
    scores(0, buf_a)

    def pair(kk, carry):
        step(2 * kk, buf_a, buf_b)
        step(2 * kk + 1, buf_b, buf_a)
        return carry

    lax.fori_loop(0, n_kt // 2 - 1, pair, 0)
    step(n_kt - 2, buf_a, buf_b)
    step(n_kt - 1, buf_b, None)

    lam = (jnp.exp(jnp.sum(lq1_ref[...] * lk1_ref[...], axis=1, keepdims=True))
           - jnp.exp(jnp.sum(lq2_ref[...] * lk2_ref[...], axis=1, keepdims=True)) + LAMBDA_INIT)
    acc = acc_ref[...]
    l = l_ref[...]
    o = acc[:, :tq] / l[:, :tq] - lam * (acc[:, tq:] / l[:, tq:])
    ms = jnp.mean(o * o, axis=0, keepdims=True)
    nw = jnp.concatenate([nw_ref[...]] * (tq // 128), axis=1)
    o = o * lax.rsqrt(ms + EPS) * nw * (1.0 - LAMBDA_INIT)
    o_ref[0] = o.T.astype(o_ref.dtype)


def _classify(positions, tq, tk, n_strips):
    B, L = positions.shape
    nq, nc = L // tq, L // KEY_CHUNK
    pq = positions.reshape(B, nq, tq)
    pk = positions.reshape(B, nc, KEY_CHUNK)
    qlo, qhi, q0 = pq.min(axis=2), pq.max(axis=2), pq[:, :, 0]
    klo, khi, k0 = pk.min(axis=2), pk.max(axis=2), pk[:, :, 0]
    q_run = jnp.all(pq == q0[:, :, None] + jnp.arange(tq, dtype=positions.dtype), axis=2)
    k_run = jnp.all(pk == k0[:, :, None] + jnp.arange(KEY_CHUNK, dtype=positions.dtype), axis=2)
    left = (qlo[:, :, None] - khi[:, None, :]) >= FAR_DIST
    right = (klo[:, None, :] - qhi[:, :, None]) >= FAR_DIST
    d = k0[:, None, :] - q0[:, :, None]
    j = d // KEY_CHUNK + 1
    strip_ok = (q_run[:, :, None] & k_run[:, None, :] & (d % KEY_CHUNK == 0) & (j >= 0) & (j < n_strips))
    cls = jnp.where(left, CLS_LEFT, jnp.where(right, CLS_RIGHT, jnp.where(strip_ok, CLS_STRIP0 + j, CLS_GENERIC)))
    cls = cls.astype(jnp.int32)
    per_step = cls.reshape(B, nq, L // tk, tk // KEY_CHUNK)
    step = jnp.where(jnp.all(per_step == CLS_LEFT, axis=3), STEP_LEFT,
                     jnp.where(jnp.all(per_step == CLS_RIGHT, axis=3), STEP_RIGHT, STEP_MIXED)).astype(jnp.int32)
    return step.reshape(B * nq, L // tk), cls.reshape(B * nq, nc)


def _diffattn(qT, kd, vT, positions, table, lq1, lk1, lq2, lk2, nw, tq, tk):
    B, _, L = qT.shape
    nq, nk = L // tq, L // tk
    n_strips = tq // KEY_CHUNK + 2
    table2 = table * LOG2E
    strips = _strips(table2, tq, n_strips)
    stepcls, chunkcls = _classify(positions, tq, tk, n_strips)
    posq = positions.reshape(B, 1, L)
    posk = jnp.broadcast_to(positions[:, :, None], (B, L, 128))
    nw_b = jnp.broadcast_to(nw.reshape(DIFF_DV, 1), (DIFF_DV, 128))
    vec = lambda a: a.reshape(1, DIFF_DH)
    full2 = lambda b, h, i, *_: (0, 0)
    grid_spec = pltpu.PrefetchScalarGridSpec(
        num_scalar_prefetch=3,
        grid=(B, DIFF_HEADS, nq),
        in_specs=[pl.BlockSpec((1, DIFF_DV, tq), lambda b, h, i, *_: (b, h, i)),
                  pl.BlockSpec((1, L, DIFF_DV), lambda b, h, i, *_: (b, 0, h)),
                  pl.BlockSpec((1, DIFF_DV, L), lambda b, h, i, *_: (b, h, 0)),
                  pl.BlockSpec((1, n_strips, KEY_CHUNK, tq), lambda b, h, i, *_: (h, 0, 0, 0)),
                  pl.BlockSpec((1, 1, tq), lambda b, h, i, *_: (b, 0, i)),
                  pl.BlockSpec((1, L, 128), lambda b, h, i, *_: (b, 0, 0)),
                  pl.BlockSpec((1, DIFF_DH), full2), pl.BlockSpec((1, DIFF_DH), full2),
                  pl.BlockSpec((1, DIFF_DH), full2), pl.BlockSpec((1, DIFF_DH), full2),
                  pl.BlockSpec((DIFF_DV, 128), full2)],
        out_specs=pl.BlockSpec((1, tq, DIFF_DV), lambda b, h, i, *_: (b, i, h)),
        scratch_shapes=[pltpu.VMEM((1, 2 * tq), F32), pltpu.VMEM((1, 2 * tq), F32),
                        pltpu.VMEM((DIFF_DV, 2 * tq), F32),
                        pltpu.VMEM((tk, 2 * tq), F32), pltpu.VMEM((tk, 2 * tq), F32),
                        pltpu.VMEM((1, 2 * tq), F32), pltpu.VMEM((1, 2 * tq), F32),
                        pltpu.VMEM((tk, tq), F32)],
    )
    return pl.pallas_call(
        functools.partial(_diff_kernel, tq=tq, tk=tk, n_kt=nk, n_strips=n_strips),
        grid_spec=grid_spec,
        out_shape=jax.ShapeDtypeStruct((B, L, DIFF_WIDTH), BF16),
        compiler_params=_params("arbitrary", "arbitrary", "arbitrary"),
        name="diffattn",
    )(stepcls, chunkcls, table2, qT, kd, vT, strips, posq, posk, vec(lq1), vec(lk1), vec(lq2), vec(lk2), nw_b)


def _outproj_kernel(of_ref, ob_ref, rg_ref, gnw_ref, od_ref, wa_ref, wb_ref, ga_ref, x_ref, o_ref):
    o = of_ref[...] + ob_ref[...]
    parts = []
    for hd in range(GLA_HEADS):
        oh = o[:, hd * GLA_DV:(hd + 1) * GLA_DV]
        ms = jnp.mean(oh * oh, axis=-1, keepdims=True)
        parts.append(oh * lax.rsqrt(ms + EPS))
    r = rg_ref[...]
    oa = jnp.concatenate(parts, axis=1) * gnw_ref[...] * (r * _sigmoid(r))
    mixed = _dot(oa.astype(BF16), wa_ref[...]) + _dot(od_ref[...], wb_ref[...])
    o_ref[...] = x_ref[...] + ga_ref[0] * mixed


def _outproj(of, ob, rg, gnw, od, wa, wb, ga, x2d, L, tm):
    T, D = x2d.shape
    per_b = L // tm
    row = lambda i: (i, 0)
    full = lambda i: (0, 0)
    bvec = lambda i: (i // per_b, 0, 0)
    return pl.pallas_call(
        _outproj_kernel,
        grid=(T // tm,),
        in_specs=[pl.BlockSpec((tm, GLA_WIDTH), row), pl.BlockSpec((tm, GLA_WIDTH), row),
                  pl.BlockSpec((tm, GLA_WIDTH), row), pl.BlockSpec((1, GLA_WIDTH), full),
                  pl.BlockSpec((tm, DIFF_WIDTH), row),
                  pl.BlockSpec(wa.shape, full), pl.BlockSpec(wb.shape, full),
                  pl.BlockSpec((1, 1, D), bvec), pl.BlockSpec((tm, D), row)],
        out_specs=pl.BlockSpec((tm, D), row),
        out_shape=jax.ShapeDtypeStruct((T, D), F32),
        compiler_params=_params("arbitrary"),
        name="outproj",
    )(of, ob, rg, gnw, od, wa, wb, ga, x2d)


HALO = 8
FF_CHUNK = 256


def _ffn_kernel(x_ref, xp_ref, xn_ref, sh_ref, sc_ref, gf_ref, nw_ref, wup_ref, cw_ref, cb_ref, wdn_ref, fw_ref,
                o_ref, acc_ref, *, tm, per_b):
    i = pl.program_id(0)
    first = (i % per_b) == 0
    last = (i % per_b) == per_b - 1
    nw, sh, sc = nw_ref[...], sh_ref[0], sc_ref[0]
    x = x_ref[...]
    h_mid = _rms_mod(x, nw, sh, sc)
    h_prev = jnp.where(first, 0.0, _rms_mod(xp_ref[...], nw, sh, sc))
    h_next = jnp.where(last, 0.0, _rms_mod(xn_ref[...], nw, sh, sc))
    h = jnp.concatenate([h_prev, h_mid, h_next], axis=0).astype(BF16)
    rows = tm + 2 * HALO

    def conv(u, col):
        w = cw_ref[:, col:col + FF_CHUNK]
        up = pltpu.roll(u, 1, 0)[HALO:HALO + tm]
        un = pltpu.roll(u, rows - 1, 0)[HALO:HALO + tm]
        return (w[0:1] * up + w[1:2] * u[HALO:HALO + tm] + w[2:3] * un) + cb_ref[:, col:col + FF_CHUNK]

    for n in range(D_FF // FF_CHUNK):
        cg, cv = n * FF_CHUNK, D_FF + n * FF_CHUNK
        gate = conv(_dot(h, wup_ref[:, cg:cg + FF_CHUNK]), cg)
        val = conv(_dot(h, wup_ref[:, cv:cv + FF_CHUNK]), cv)
        a = (gate * _sigmoid(gate) * val).astype(BF16)
        part = _dot(a, wdn_ref[cg:cg + FF_CHUNK, :])
        if n == 0:
            acc_ref[...] = part
        else:
            acc_ref[...] += part
    y = x + gf_ref[0] * acc_ref[...]
    ms = jnp.mean(y * y, axis=-1, keepdims=True)
    o_ref[...] = y * lax.rsqrt(ms + EPS) * fw_ref[...]


def _ffn(x1, sh, sc, gf, nw, wup, cw, cb, wdn, fw, L, tm):
    T, D = x1.shape
    per_b = L // tm
    hb = tm // HALO
    n_halo = T // HALO
    row = lambda i: (i, 0)
    full = lambda i: (0, 0)
    bvec = lambda i: (i // per_b, 0, 0)
    return pl.pallas_call(
        functools.partial(_ffn_kernel, tm=tm, per_b=per_b),
        grid=(T // tm,),
        in_specs=[pl.BlockSpec((tm, D), row),
                  pl.BlockSpec((HALO, D), lambda i: (jnp.maximum(i * hb - 1, 0), 0)),
                  pl.BlockSpec((HALO, D), lambda i: (jnp.minimum((i + 1) * hb, n_halo - 1), 0)),
                  pl.BlockSpec((1, 1, D), bvec), pl.BlockSpec((1, 1, D), bvec), pl.BlockSpec((1, 1, D), bvec),
                  pl.BlockSpec((1, D), full),
                  pl.BlockSpec(wup.shape, full), pl.BlockSpec(cw.shape, full), pl.BlockSpec(cb.shape, full),
                  pl.BlockSpec(wdn.shape, full), pl.BlockSpec((1, D), full)],
        out_specs=pl.BlockSpec((tm, D), row),
        out_shape=jax.ShapeDtypeStruct((T, D), F32),
        scratch_shapes=[pltpu.VMEM((tm, D), F32)],
        compiler_params=_params("arbitrary"),
        name="ffn",
    )(x1, x1, x1, sh, sc, gf, nw, wup, cw, cb, wdn, fw)


def kernel(x, c, positions, w_ada, b_ada, attn_norm_w, w_in, gla_dec_w_fwd, gla_dec_b_fwd, gla_dec_w_bwd,
           gla_dec_b_bwd, gla_norm_w, diff_lambda_q1, diff_lambda_k1, diff_lambda_q2, diff_lambda_k2,
           diff_norm_w, rel_bias_table, w_out, ffn_norm_w, w_up, conv_w, conv_b, w_down, final_norm_w):
    B, L, D = x.shape
    T = B * L
    tm = min(512, L)
    x2d = x.reshape(T, D)

    mod = _ada(c, w_ada[0], b_ada[0])
    sh_a, sc_a, g_a, sh_f, sc_f, g_f = [m.reshape(B, 1, D) for m in jnp.split(mod, N_MOD, axis=-1)]

    o_lr = 2 * GLA_QK + 2 * GLA_WIDTH
    o_d = o_lr + 2 * GLA_RANK
    w = w_in[0]
    wg = w[:, :o_lr].astype(BF16)
    wlr = jnp.pad(w[:, o_lr:o_d], ((0, 0), (0, 128 - 2 * GLA_RANK))).astype(BF16)
    wd = w[:, o_d:].astype(BF16)
    decw = jnp.zeros((128, 2 * GLA_QK), F32)
    decw = decw.at[:GLA_RANK, :GLA_QK].set(gla_dec_w_fwd[0]).at[GLA_RANK:2 * GLA_RANK, GLA_QK:].set(gla_dec_w_bwd[0])
    decb = jnp.concatenate([gla_dec_b_fwd[0], gla_dec_b_bwd[0]]).reshape(1, 2 * GLA_QK)

    qg, kg, vg, rg, laf, lab, qd, kd, vd = _inproj(
        x2d, sh_a, sc_a, attn_norm_w[0].reshape(1, D), wg, wlr, wd, decw.astype(BF16), decb, L, tm)

    o_f, o_b = _gla(qg, kg, vg, laf, lab, B, L, min(512, L))

    qT = jnp.swapaxes(qd.reshape(B, L, DIFF_WIDTH), 1, 2)
    vT = jnp.swapaxes(vd.reshape(B, L, DIFF_WIDTH), 1, 2)
    o_d = _diffattn(qT, kd.reshape(B, L, DIFF_WIDTH), vT, positions, rel_bias_table,
                    diff_lambda_q1[0], diff_lambda_k1[0], diff_lambda_q2[0], diff_lambda_k2[0],
                    diff_norm_w[0], min(512, L), min(1024, L // 2))

    wo = w_out[0].astype(BF16)
    x1 = _outproj(o_f, o_b, rg, gla_norm_w[0].reshape(1, GLA_WIDTH), o_d.reshape(T, DIFF_WIDTH),
                  wo[:GLA_WIDTH], wo[GLA_WIDTH:], g_a, x2d, L, tm)

    out = _ffn(x1, sh_f, sc_f, g_f, ffn_norm_w[0].reshape(1, D), w_up[0].astype(BF16), conv_w[0],
               conv_b[0].reshape(1, 2 * D_FF), w_down[0].astype(BF16), final_norm_w.reshape(1, D), L, tm)
    return out.reshape(B, L, D)
```

```python
import functools
import math

import jax
import jax.numpy as jnp
from jax import lax
from jax.experimental import pallas as pl
from jax.experimental.pallas import tpu as pltpu

F32 = jnp.float32
BF16 = jnp.bfloat16

D_MODEL = 1024
GLA_HEADS = 4
GLA_DK = 64
GLA_DV = 128
GLA_RANK = 16
GLA_GATE_TEMP = 16.0
GLA_CHUNK = 64
GLA_QK = GLA_HEADS * GLA_DK
GLA_WIDTH = GLA_HEADS * GLA_DV
DIFF_HEADS = 4
DIFF_DH = 64
DIFF_DV = 2 * DIFF_DH
DIFF_WIDTH = DIFF_HEADS * DIFF_DV
REL_BUCKETS = 32
D_FF = 2816
N_MOD = 6
EPS = 1e-6
LAMBDA_INIT = 0.8 - 0.6 * math.exp(-0.3 * 0)
LOG2E = 1.4426950408889634

BUCKET_STEPS = (1, 2, 3, 4, 5, 6, 7, 8, 12, 16, 23, 32, 46, 64, 91)
FAR_DIST = BUCKET_STEPS[-1]

VMEM_LIMIT = 56 * 1024 * 1024


def _split_bf16(a):
    hi = a.astype(BF16)
    lo = (a - hi.astype(F32)).astype(BF16)
    return hi, lo


def _dot(a, b):
    return jnp.dot(a, b, preferred_element_type=F32)


def _sigmoid(v):
    return 1.0 / (1.0 + jnp.exp(-v))


def _params(*sem):
    return pltpu.CompilerParams(dimension_semantics=sem, vmem_limit_bytes=VMEM_LIMIT)


def _ada_kernel(c_ref, w_ref, b_ref, o_ref):
    c = c_ref[...]
    s_hi, s_lo = _split_bf16(c * _sigmoid(c))
    w_hi, w_lo = _split_bf16(w_ref[...])
    o_ref[...] = (_dot(s_hi, w_hi) + _dot(s_lo, w_hi) + _dot(s_hi, w_lo)) + b_ref[...]


def _ada(c, w, b):
    B, D = c.shape
    N = w.shape[1]
    tn = 1024
    return pl.pallas_call(
        _ada_kernel,
        grid=(N // tn,),
        in_specs=[pl.BlockSpec((B, D), lambda j: (0, 0)),
                  pl.BlockSpec((D, tn), lambda j: (0, j)),
                  pl.BlockSpec((1, tn), lambda j: (0, j))],
        out_specs=pl.BlockSpec((B, tn), lambda j: (0, j)),
        out_shape=jax.ShapeDtypeStruct((B, N), F32),
        compiler_params=_params("arbitrary"),
        name="ada",
    )(c, w, b.reshape(1, N))


def _rms_mod(x, nw, sh, sc):
    ms = jnp.mean(x * x, axis=-1, keepdims=True)
    return (x * lax.rsqrt(ms + EPS) * nw) * (1.0 + sc) + sh


def _inproj_kernel(x_ref, sh_ref, sc_ref, nw_ref, wg_ref, wlr_ref, wd_ref, decw_ref, decb_ref,
                   qg_ref, kg_ref, vg_ref, rg_ref, laf_ref, lab_ref, qd_ref, kd_ref, vd_ref):
    h = _rms_mod(x_ref[...], nw_ref[...], sh_ref[0], sc_ref[0]).astype(BF16)
    g = _dot(h, wg_ref[...])
    qg_ref[...] = g[:, :GLA_QK] * (GLA_DK ** -0.5)
    kg_ref[...] = g[:, GLA_QK:2 * GLA_QK]
    vg_ref[...] = g[:, 2 * GLA_QK:2 * GLA_QK + GLA_WIDTH]
    rg_ref[...] = g[:, 2 * GLA_QK + GLA_WIDTH:]
    lr = _dot(h, wlr_ref[...])
    z = _dot(lr.astype(BF16), decw_ref[...]) + decb_ref[...]
    la = (jnp.minimum(z, 0.0) - jnp.log(1.0 + jnp.exp(-jnp.abs(z)))) * (1.0 / GLA_GATE_TEMP)
    laf_ref[...] = la[:, :GLA_QK]
    lab_ref[...] = la[:, GLA_QK:]
    d = _dot(h, wd_ref[...])
    qd_ref[...] = (d[:, :DIFF_WIDTH] * (DIFF_DH ** -0.5 * LOG2E)).astype(BF16)
    kd_ref[...] = d[:, DIFF_WIDTH:2 * DIFF_WIDTH].astype(BF16)
    vd_ref[...] = d[:, 2 * DIFF_WIDTH:].astype(BF16)


def _inproj(x2d, sh, sc, nw, wg, wlr, wd, decw, decb, L, tm):
    T, D = x2d.shape
    per_b = L // tm
    row = lambda i: (i, 0)
    full = lambda i: (0, 0)
    bvec = lambda i: (i // per_b, 0, 0)
    outs = [(GLA_QK, F32), (GLA_QK, F32), (GLA_WIDTH, F32), (GLA_WIDTH, F32), (GLA_QK, F32), (GLA_QK, F32),
            (DIFF_WIDTH, BF16), (DIFF_WIDTH, BF16), (DIFF_WIDTH, BF16)]
    return pl.pallas_call(
        _inproj_kernel,
        grid=(T // tm,),
        in_specs=[pl.BlockSpec((tm, D), row),
                  pl.BlockSpec((1, 1, D), bvec), pl.BlockSpec((1, 1, D), bvec),
                  pl.BlockSpec((1, D), full),
                  pl.BlockSpec(wg.shape, full), pl.BlockSpec(wlr.shape, full), pl.BlockSpec(wd.shape, full),
                  pl.BlockSpec(decw.shape, full), pl.BlockSpec(decb.shape, full)],
        out_specs=[pl.BlockSpec((tm, n), row) for n, _ in outs],
        out_shape=[jax.ShapeDtypeStruct((T, n), dt) for n, dt in outs],
        compiler_params=_params("arbitrary"),
        name="inproj",
    )(x2d, sh, sc, nw, wg, wlr, wd, decw, decb)


def _gla_kernel(qf_ref, kf_ref, vf_ref, laf_ref, qb_ref, kb_ref, vb_ref, lab_ref,
                of_ref, ob_ref, sf_ref, sb_ref, *, n_chunks, group):
    C = GLA_CHUNK

    @pl.when(pl.program_id(1) == 0)
    def _():
        sf_ref[...] = jnp.zeros_like(sf_ref)
        sb_ref[...] = jnp.zeros_like(sb_ref)

    r = lax.broadcasted_iota(jnp.int32, (C, C), 0)
    s = lax.broadcasted_iota(jnp.int32, (C, C), 1)
    tri_f = jnp.where(s <= r, 1.0, 0.0).astype(BF16)
    tri_b = jnp.where(s >= r, 1.0, 0.0).astype(BF16)
    r4 = lax.broadcasted_iota(jnp.int32, (GLA_HEADS * C, C), 0) % C
    s4 = lax.broadcasted_iota(jnp.int32, (GLA_HEADS * C, C), 1)
    mask_f = s4 <= r4
    mask_b = s4 > r4
    lane = lax.broadcasted_iota(jnp.int32, (C, GLA_QK), 1)
    head_masks = [(lane >= hd * GLA_DK) & (lane < (hd + 1) * GLA_DK) for hd in range(GLA_HEADS)]
    nt = (((1,), (1,)), ((), ()))
    tn = (((0,), (0,)), ((), ()))

    def per_head(x):
        zero = jnp.zeros_like(x)
        return [jnp.where(mk, x, zero) for mk in head_masks]

    dirs = ((qf_ref, kf_ref, vf_ref, laf_ref, of_ref, sf_ref, tri_f, mask_f, C - 1, False),
            (qb_ref, kb_ref, vb_ref, lab_ref, ob_ref, sb_ref, tri_b, mask_b, 0, True))

    def run_group(g):
        work = []
        for (q_ref, k_ref, v_ref, la_ref, o_ref, s_ref, tri, att_mask, last_row, rev) in dirs:
            for i in range(group):
                c = g * group + i
                c = (n_chunks - 1 - c) if rev else c
                work.append(dict(rows=pl.ds(pl.multiple_of(c * C, C), C), q_ref=q_ref, k_ref=k_ref, v_ref=v_ref,
                                 la_ref=la_ref, o_ref=o_ref, s_ref=s_ref, tri=tri, att_mask=att_mask,
                                 last_row=last_row))
        for w in work:
            la_hi, la_lo = _split_bf16(w["la_ref"][w["rows"], :])
            w["b"] = _dot(w["tri"], la_hi) + _dot(w["tri"], la_lo)
        for w in work:
            b = w["b"]
            tot = b[w["last_row"]:w["last_row"] + 1, :]
            q = w["q_ref"][w["rows"], :]
            k = w["k_ref"][w["rows"], :]
            w["dec"] = jnp.exp(tot)
            w["qs"] = jnp.concatenate(per_head((q * jnp.exp(b)).astype(BF16)), axis=0)
            w["k_in"] = (k * jnp.exp(-b)).astype(BF16)
            w["k_st"] = per_head((k * jnp.exp(tot - b)).astype(BF16))
            w["v"] = w["v_ref"][w["rows"], :].astype(BF16)
        for w in work:
            att = lax.dot_general(w["qs"], w["k_in"], nt, preferred_element_type=F32)
            w["att"] = jnp.where(w["att_mask"], att, 0.0).astype(BF16)
        for w in work:
            w["o"] = jnp.concatenate(
                [_dot(w["att"][hd * C:(hd + 1) * C, :], w["v"][:, hd * GLA_DV:(hd + 1) * GLA_DV])
                 for hd in range(GLA_HEADS)], axis=1)
        for w in work:
            upd = None
            for hd in range(GLA_HEADS):
                t = lax.dot_general(w["v"][:, hd * GLA_DV:(hd + 1) * GLA_DV], w["k_st"][hd], tn,
                                    preferred_element_type=F32)
                upd = t if upd is None else upd + t
            w["upd"] = upd
        for d in range(2):
            s_ref = dirs[d][5]
            st = s_ref[...]
            for w in work[d * group:(d + 1) * group]:
                oi = lax.dot_general(w["qs"], st.astype(BF16), nt, preferred_element_type=F32)
                oi = jnp.concatenate([oi[hd * C:(hd + 1) * C, :] for hd in range(GLA_HEADS)], axis=1)
                w["o_ref"][w["rows"], :] = w["o"] + oi
                st = st * w["dec"] + w["upd"]
            s_ref[...] = st

    n_groups = n_chunks // group
    if n_groups == 1:
        run_group(0)
    else:
        def body(g, carry):
            run_group(g)
            return carry
        lax.fori_loop(0, n_groups, body, 0)


def _gla(qg, kg, vg, laf, lab, B, L, tb):
    T = B * L
    J = L // tb
    fwd = lambda b, j: (b * J + j, 0)
    bwd = lambda b, j: (b * J + (J - 1 - j), 0)
    qk = lambda im: pl.BlockSpec((tb, GLA_QK), im)
    vv = lambda im: pl.BlockSpec((tb, GLA_WIDTH), im)
    return pl.pallas_call(
        functools.partial(_gla_kernel, n_chunks=tb // GLA_CHUNK, group=min(4, tb // GLA_CHUNK)),
        grid=(B, J),
        in_specs=[qk(fwd), qk(fwd), vv(fwd), qk(fwd), qk(bwd), qk(bwd), vv(bwd), qk(bwd)],
        out_specs=[vv(fwd), vv(bwd)],
        out_shape=[jax.ShapeDtypeStruct((T, GLA_WIDTH), F32)] * 2,
        scratch_shapes=[pltpu.VMEM((GLA_DV, GLA_QK), F32)] * 2,
        compiler_params=_params("arbitrary", "arbitrary"),
        name="gla",
    )(qg, kg, vg, laf, qg, kg, vg, lab)


def _bias_from_rel(rel, tab_ref, hd):
    n = jnp.abs(rel)
    half = REL_BUCKETS // 2
    neg = jnp.full(rel.shape, tab_ref[0, hd], F32)
    pos = jnp.full(rel.shape, tab_ref[half, hd], F32)
    for j, t in enumerate(BUCKET_STEPS):
        ge = n >= t
        neg = jnp.where(ge, tab_ref[j + 1, hd], neg)
        pos = jnp.where(ge, tab_ref[half + j + 1, hd], pos)
    return jnp.where(rel > 0, pos, neg)


KEY_CHUNK = 128
COL_TILE = 256
CLS_LEFT, CLS_RIGHT, CLS_STRIP0, CLS_GENERIC = 0, 1, 2, 99
STEP_LEFT, STEP_RIGHT, STEP_MIXED = 0, 1, 2


def _strips_kernel(tab_ref, o_ref, *, tq, n_strips):
    hd = pl.program_id(0)
    kk = lax.broadcasted_iota(jnp.int32, (KEY_CHUNK, tq), 0)
    qq = lax.broadcasted_iota(jnp.int32, (KEY_CHUNK, tq), 1)
    for j in range(n_strips):
        o_ref[0, j] = _bias_from_rel(KEY_CHUNK * (j - 1) + kk - qq, tab_ref, hd)


def _strips(table, tq, n_strips):
    H = table.shape[1]
    return pl.pallas_call(
        functools.partial(_strips_kernel, tq=tq, n_strips=n_strips),
        grid_spec=pltpu.PrefetchScalarGridSpec(
            num_scalar_prefetch=1, grid=(H,), in_specs=[],
            out_specs=pl.BlockSpec((1, n_strips, KEY_CHUNK, tq), lambda h, *_: (h, 0, 0, 0))),
        out_shape=jax.ShapeDtypeStruct((H, n_strips, KEY_CHUNK, tq), F32),
        compiler_params=_params("arbitrary"),
        name="bias_strips",
    )(table)


def _diff_kernel(stepcls_ref, chunkcls_ref, tab_ref,
                 qT_ref, k_ref, vT_ref, strip_ref, posq_ref, posk_ref, lq1_ref, lk1_ref, lq2_ref, lk2_ref, nw_ref,
                 o_ref, m_ref, l_ref, acc_ref, sa_ref, sb_ref, smaxa_ref, smaxb_ref, bias_ref,
                 *, tq, tk, n_kt, n_strips):
    hd = pl.program_id(1)
    r = pl.program_id(0) * pl.num_programs(2) + pl.program_id(2)
    half = REL_BUCKETS // 2
    n_ct = 2 * tq // COL_TILE
    n_chunk = tk // KEY_CHUNK

    qT = qT_ref[0]
    row = lax.broadcasted_iota(jnp.int32, qT.shape, 0)
    zero = jnp.zeros_like(qT)
    qstack = jnp.concatenate([jnp.where(row < DIFF_DH, qT, zero), jnp.where(row >= DIFF_DH, qT, zero)], axis=1)

    m_ref[...] = jnp.full(m_ref.shape, -1e30, F32)
    l_ref[...] = jnp.zeros(l_ref.shape, F32)
    acc_ref[...] = jnp.zeros(acc_ref.shape, F32)

    bias_max = functools.reduce(jnp.maximum, [tab_ref[i, hd] for i in range(REL_BUCKETS)])

    def scores_tile(kt, buf, j):
        s_ref, smax_ref = buf
        kblk = k_ref[0, pl.ds(pl.multiple_of(kt * tk, tk), tk), :]
        cols = slice(j * COL_TILE, (j + 1) * COL_TILE)
        s = _dot(kblk, qstack[:, cols])
        s_ref[:, cols] = s
        smax_ref[:, cols] = jnp.max(s, axis=0, keepdims=True)

    def scores(kt, buf):
        for j in range(n_ct):
            scores_tile(kt, buf, j)

    def build_bias(kt):
        posq = posq_ref[0]
        for c in range(n_chunk):
            cc = chunkcls_ref[r, kt * n_chunk + c]
            rows = slice(c * KEY_CHUNK, (c + 1) * KEY_CHUNK)

            @pl.when(cc == CLS_LEFT)
            def _():
                bias_ref[rows, :] = jnp.full((KEY_CHUNK, tq), tab_ref[half - 1, hd], F32)

            @pl.when(cc == CLS_RIGHT)
            def _():
                bias_ref[rows, :] = jnp.full((KEY_CHUNK, tq), tab_ref[REL_BUCKETS - 1, hd], F32)

            @pl.when(jnp.logical_and(cc >= CLS_STRIP0, cc < CLS_STRIP0 + n_strips))
            def _():
                bias_ref[rows, :] = strip_ref[0, cc - CLS_STRIP0]

            @pl.when(cc == CLS_GENERIC)
            def _():
                k0 = pl.multiple_of(kt * tk + c * KEY_CHUNK, KEY_CHUNK)
                pk = posk_ref[0, pl.ds(k0, KEY_CHUNK), :]
                pk = jnp.concatenate([pk] * (tq // 128), axis=1)
                bias_ref[rows, :] = _bias_from_rel(pk - posq, tab_ref, hd)

    def softmax_pv(kt, cur, const, nxt):
        s_ref, smax_ref = cur
        vblk = vT_ref[0, :, pl.ds(pl.multiple_of(kt * tk, tk), tk)]
        for j in range(n_ct):
            if nxt is not None:
                scores_tile(kt + 1, nxt, j)
            cols = slice(j * COL_TILE, (j + 1) * COL_TILE)
            shift = bias_max if const is None else const
            m_old = m_ref[:, cols]
            m_new = jnp.maximum(m_old, smax_ref[:, cols] + shift)
            alpha = jnp.exp2(m_old - m_new)
            if const is None:
                q0 = (j * COL_TILE) % tq
                p = jnp.exp2((s_ref[:, cols] + bias_ref[:, q0:q0 + COL_TILE]) - m_new)
            else:
                p = jnp.exp2(s_ref[:, cols] - (m_new - shift))
            l_ref[:, cols] = alpha * l_ref[:, cols] + jnp.sum(p, axis=0, keepdims=True)
            acc_ref[:, cols] = alpha * acc_ref[:, cols] + _dot(vblk, p.astype(BF16))
            m_ref[:, cols] = m_new

    def step(kt, cur, nxt):
        cls = stepcls_ref[r, kt]

        @pl.when(cls == STEP_MIXED)
        def _():
            build_bias(kt)
            softmax_pv(kt, cur, None, nxt)

        @pl.when(cls != STEP_MIXED)
        def _():
            const = jnp.where(cls == STEP_LEFT, tab_ref[half - 1, hd], tab_ref[REL_BUCKETS - 1, hd])
            softmax_pv(kt, cur, const, nxt)

    buf_a = (sa_ref, smaxa_ref)
    buf_b = (sb_ref, smaxb_ref)
    scores(0, buf_a)

    def pair(kk, carry):
        step(2 * kk, buf_a, buf_b)
        step(2 * kk + 1, buf_b, buf_a)
        return carry

    lax.fori_loop(0, n_kt // 2 - 1, pair, 0)
    step(n_kt - 2, buf_a, buf_b)
    step(n_kt - 1, buf_b, None)

    lam = (jnp.exp(jnp.sum(lq1_ref[...] * lk1_ref[...], axis=1, keepdims=True))
           - jnp.exp(jnp.sum(lq2_ref[...] * lk2_ref[...], axis=1, keepdims=True)) + LAMBDA_INIT)
    acc = acc_ref[...]
    l = l_ref[...]
    o = acc[:, :tq] / l[:, :tq] - lam * (acc[:, tq:] / l[:, tq:])
    ms = jnp.mean(o * o, axis=0, keepdims=True)
    nw = jnp.concatenate([nw_ref[...]] * (tq // 128), axis=1)
    o = o * lax.rsqrt(ms + EPS) * nw * (1.0 - LAMBDA_INIT)
    o_ref[0] = o.T.astype(o_ref.dtype)


def _classify(positions, tq, tk, n_strips):
    B, L = positions.shape
    nq, nc = L // tq, L // KEY_CHUNK
    pq = positions.reshape(B, nq, tq)
    pk = positions.reshape(B, nc, KEY_CHUNK)
    qlo, qhi, q0 = pq.min(axis=2), pq.max(axis=2), pq[:, :, 0]
    klo, khi, k0 = pk.min(axis=2), pk.max(axis=2), pk[:, :, 0]
    q_run = jnp.all(pq == q0[:, :, None] + jnp.arange(tq, dtype=positions.dtype), axis=2)
    k_run = jnp.all(pk == k0[:, :, None] + jnp.arange(KEY_CHUNK, dtype=positions.dtype), axis=2)
    left = (qlo[:, :, None] - khi[:, None, :]) >= FAR_DIST
    right = (klo[:, None, :] - qhi[:, :, None]) >= FAR_DIST
    d = k0[:, None, :] - q0[:, :, None]
    j = d // KEY_CHUNK + 1
    strip_ok = (q_run[:, :, None] & k_run[:, None, :] & (d % KEY_CHUNK == 0) & (j >= 0) & (j < n_strips))
    cls = jnp.where(left, CLS_LEFT, jnp.where(right, CLS_RIGHT, jnp.where(strip_ok, CLS_STRIP0 + j, CLS_GENERIC)))
    cls = cls.astype(jnp.int32)
    per_step = cls.reshape(B, nq, L // tk, tk // KEY_CHUNK)
    step = jnp.where(jnp.all(per_step == CLS_LEFT, axis=3), STEP_LEFT,
                     jnp.where(jnp.all(per_step == CLS_RIGHT, axis=3), STEP_RIGHT, STEP_MIXED)).astype(jnp.int32)
    return step.reshape(B * nq, L // tk), cls.reshape(B * nq, nc)


def _diffattn(qT, kd, vT, positions, table, lq1, lk1, lq2, lk2, nw, tq, tk):
    B, _, L = qT.shape
    nq, nk = L // tq, L // tk
    n_strips = tq // KEY_CHUNK + 2
    table2 = table * LOG2E
    strips = _strips(table2, tq, n_strips)
    stepcls, chunkcls = _classify(positions, tq, tk, n_strips)
    posq = positions.reshape(B, 1, L)
    posk = jnp.broadcast_to(positions[:, :, None], (B, L, 128))
    nw_b = jnp.broadcast_to(nw.reshape(DIFF_DV, 1), (DIFF_DV, 128))
    vec = lambda a: a.reshape(1, DIFF_DH)
    full2 = lambda b, h, i, *_: (0, 0)
    grid_spec = pltpu.PrefetchScalarGridSpec(
        num_scalar_prefetch=3,
        grid=(B, DIFF_HEADS, nq),
        in_specs=[pl.BlockSpec((1, DIFF_DV, tq), lambda b, h, i, *_: (b, h, i)),
                  pl.BlockSpec((1, L, DIFF_DV), lambda b, h, i, *_: (b, 0, h)),
                  pl.BlockSpec((1, DIFF_DV, L), lambda b, h, i, *_: (b, h, 0)),
                  pl.BlockSpec((1, n_strips, KEY_CHUNK, tq), lambda b, h, i, *_: (h, 0, 0, 0)),
                  pl.BlockSpec((1, 1, tq), lambda b, h, i, *_: (b, 0, i)),
                  pl.BlockSpec((1, L, 128), lambda b, h, i, *_: (b, 0, 0)),
                  pl.BlockSpec((1, DIFF_DH), full2), pl.BlockSpec((1, DIFF_DH), full2),
                  pl.BlockSpec((1, DIFF_DH), full2), pl.BlockSpec((1, DIFF_DH), full2),
                  pl.BlockSpec((DIFF_DV, 128), full2)],
        out_specs=pl.BlockSpec((1, tq, DIFF_DV), lambda b, h, i, *_: (b, i, h)),
        scratch_shapes=[pltpu.VMEM((1, 2 * tq), F32), pltpu.VMEM((1, 2 * tq), F32),
                        pltpu.VMEM((DIFF_DV, 2 * tq), F32),
                        pltpu.VMEM((tk, 2 * tq), F32), pltpu.VMEM((tk, 2 * tq), F32),
                        pltpu.VMEM((1, 2 * tq), F32), pltpu.VMEM((1, 2 * tq), F32),
                        pltpu.VMEM((tk, tq), F32)],
    )
    return pl.pallas_call(
        functools.partial(_diff_kernel, tq=tq, tk=tk, n_kt=nk, n_strips=n_strips),
        grid_spec=grid_spec,
        out_shape=jax.ShapeDtypeStruct((B, L, DIFF_WIDTH), BF16),
        compiler_params=_params("arbitrary", "arbitrary", "arbitrary"),
        name="diffattn",
    )(stepcls, chunkcls, table2, qT, kd, vT, strips, posq, posk, vec(lq1), vec(lk1), vec(lq2), vec(lk2), nw_b)


def _outproj_kernel(of_ref, ob_ref, rg_ref, gnw_ref, od_ref, wa_ref, wb_ref, ga_ref, x_ref, o_ref):
    o = of_ref[...] + ob_ref[...]
    parts = []
    for hd in range(GLA_HEADS):
        oh = o[:, hd * GLA_DV:(hd + 1) * GLA_DV]
        ms = jnp.mean(oh * oh, axis=-1, keepdims=True)
        parts.append(oh * lax.rsqrt(ms + EPS))
    r = rg_ref[...]
    oa = jnp.concatenate(parts, axis=1) * gnw_ref[...] * (r * _sigmoid(r))
    mixed = _dot(oa.astype(BF16), wa_ref[...]) + _dot(od_ref[...], wb_ref[...])
    o_ref[...] = x_ref[...] + ga_ref[0] * mixed


def _outproj(of, ob, rg, gnw, od, wa, wb, ga, x2d, L, tm):
    T, D = x2d.shape
    per_b = L // tm
    row = lambda i: (i, 0)
    full = lambda i: (0, 0)
    bvec = lambda i: (i // per_b, 0, 0)
    return pl.pallas_call(
        _outproj_kernel,
        grid=(T // tm,),
        in_specs=[pl.BlockSpec((tm, GLA_WIDTH), row), pl.BlockSpec((tm, GLA_WIDTH), row),
                  pl.BlockSpec((tm, GLA_WIDTH), row), pl.BlockSpec((1, GLA_WIDTH), full),
                  pl.BlockSpec((tm, DIFF_WIDTH), row),
                  pl.BlockSpec(wa.shape, full), pl.BlockSpec(wb.shape, full),
                  pl.BlockSpec((1, 1, D), bvec), pl.BlockSpec((tm, D), row)],
        out_specs=pl.BlockSpec((tm, D), row),
        out_shape=jax.ShapeDtypeStruct((T, D), F32),
        compiler_params=_params("arbitrary"),
        name="outproj",
    )(of, ob, rg, gnw, od, wa, wb, ga, x2d)


HALO = 8
FF_CHUNK = 256


def _ffn_kernel(x_ref, xp_ref, xn_ref, sh_ref, sc_ref, gf_ref, nw_ref, wup_ref, cw_ref, cb_ref, wdn_ref, fw_ref,
                o_ref, acc_ref, *, tm, per_b):
    i = pl.program_id(0)
    first = (i % per_b) == 0
    last = (i % per_b) == per_b - 1
    nw, sh, sc = nw_ref[...], sh_ref[0], sc_ref[0]
    x = x_ref[...]
    h_mid = _rms_mod(x, nw, sh, sc)
    h_prev = jnp.where(first, 0.0, _rms_mod(xp_ref[...], nw, sh, sc))
    h_next = jnp.where(last, 0.0, _rms_mod(xn_ref[...], nw, sh, sc))
    h = jnp.concatenate([h_prev, h_mid, h_next], axis=0).astype(BF16)
    rows = tm + 2 * HALO

    def conv(u, col):
        w = cw_ref[:, col:col + FF_CHUNK]
        up = pltpu.roll(u, 1, 0)[HALO:HALO + tm]
        un = pltpu.roll(u, rows - 1, 0)[HALO:HALO + tm]
        return (w[0:1] * up + w[1:2] * u[HALO:HALO + tm] + w[2:3] * un) + cb_ref[:, col:col + FF_CHUNK]

    def up(n):
        cg, cv = n * FF_CHUNK, D_FF + n * FF_CHUNK
        return _dot(h, wup_ref[:, cg:cg + FF_CHUNK]), _dot(h, wup_ref[:, cv:cv + FF_CHUNK])

    n_chunks = D_FF // FF_CHUNK
    u_gate, u_val = up(0)
    for n in range(n_chunks):
        cg, cv = n * FF_CHUNK, D_FF + n * FF_CHUNK
        if n + 1 < n_chunks:
            nxt = up(n + 1)
        gate = conv(u_gate, cg)
        val = conv(u_val, cv)
        a = (gate * _sigmoid(gate) * val).astype(BF16)
        part = _dot(a, wdn_ref[cg:cg + FF_CHUNK, :])
        if n == 0:
            acc_ref[...] = part
        else:
            acc_ref[...] += part
        if n + 1 < n_chunks:
            u_gate, u_val = nxt
    y = x + gf_ref[0] * acc_ref[...]
    ms = jnp.mean(y * y, axis=-1, keepdims=True)
    o_ref[...] = y * lax.rsqrt(ms + EPS) * fw_ref[...]


def _ffn(x1, sh, sc, gf, nw, wup, cw, cb, wdn, fw, L, tm):
    T, D = x1.shape
    per_b = L // tm
    hb = tm // HALO
    n_halo = T // HALO
    row = lambda i: (i, 0)
    full = lambda i: (0, 0)
    bvec = lambda i: (i // per_b, 0, 0)
    return pl.pallas_call(
        functools.partial(_ffn_kernel, tm=tm, per_b=per_b),
        grid=(T // tm,),
        in_specs=[pl.BlockSpec((tm, D), row),
                  pl.BlockSpec((HALO, D), lambda i: (jnp.maximum(i * hb - 1, 0), 0)),
                  pl.BlockSpec((HALO, D), lambda i: (jnp.minimum((i + 1) * hb, n_halo - 1), 0)),
                  pl.BlockSpec((1, 1, D), bvec), pl.BlockSpec((1, 1, D), bvec), pl.BlockSpec((1, 1, D), bvec),
                  pl.BlockSpec((1, D), full),
                  pl.BlockSpec(wup.shape, full), pl.BlockSpec(cw.shape, full), pl.BlockSpec(cb.shape, full),
                  pl.BlockSpec(wdn.shape, full), pl.BlockSpec((1, D), full)],
        out_specs=pl.BlockSpec((tm, D), row),
        out_shape=jax.ShapeDtypeStruct((T, D), F32),
        scratch_shapes=[pltpu.VMEM((tm, D), F32)],
        compiler_params=_params("arbitrary"),
        name="ffn",
    )(x1, x1, x1, sh, sc, gf, nw, wup, cw, cb, wdn, fw)


def kernel(x, c, positions, w_ada, b_ada, attn_norm_w, w_in, gla_dec_w_fwd, gla_dec_b_fwd, gla_dec_w_bwd,
           gla_dec_b_bwd, gla_norm_w, diff_lambda_q1, diff_lambda_k1, diff_lambda_q2, diff_lambda_k2,
           diff_norm_w, rel_bias_table, w_out, ffn_norm_w, w_up, conv_w, conv_b, w_down, final_norm_w):
    B, L, D = x.shape
    T = B * L
    tm = min(512, L)
    x2d = x.reshape(T, D)

    mod = _ada(c, w_ada[0], b_ada[0])
    sh_a, sc_a, g_a, sh_f, sc_f, g_f = [m.reshape(B, 1, D) for m in jnp.split(mod, N_MOD, axis=-1)]

    o_lr = 2 * GLA_QK + 2 * GLA_WIDTH
    o_d = o_lr + 2 * GLA_RANK
    w = w_in[0]
    wg = w[:, :o_lr].astype(BF16)
    wlr = jnp.pad(w[:, o_lr:o_d], ((0, 0), (0, 128 - 2 * GLA_RANK))).astype(BF16)
    wd = w[:, o_d:].astype(BF16)
    decw = jnp.zeros((128, 2 * GLA_QK), F32)
    decw = decw.at[:GLA_RANK, :GLA_QK].set(gla_dec_w_fwd[0]).at[GLA_RANK:2 * GLA_RANK, GLA_QK:].set(gla_dec_w_bwd[0])
    decb = jnp.concatenate([gla_dec_b_fwd[0], gla_dec_b_bwd[0]]).reshape(1, 2 * GLA_QK)

    qg, kg, vg, rg, laf, lab, qd, kd, vd = _inproj(
        x2d, sh_a, sc_a, attn_norm_w[0].reshape(1, D), wg, wlr, wd, decw.astype(BF16), decb, L, tm)

    o_f, o_b = _gla(qg, kg, vg, laf, lab, B, L, min(512, L))

    qT = jnp.swapaxes(qd.reshape(B, L, DIFF_WIDTH), 1, 2)
    vT = jnp.swapaxes(vd.reshape(B, L, DIFF_WIDTH), 1, 2)
    o_d = _diffattn(qT, kd.reshape(B, L, DIFF_WIDTH), vT, positions, rel_bias_table,
                    diff_lambda_q1[0], diff_lambda_k1[0], diff_lambda_q2[0], diff_lambda_k2[0],
                    diff_norm_w[0], min(512, L), min(2048, L // 2))

    wo = w_out[0].astype(BF16)
    x1 = _outproj(o_f, o_b, rg, gla_norm_w[0].reshape(1, GLA_WIDTH), o_d.reshape(T, DIFF_WIDTH),
                  wo[:GLA_WIDTH], wo[GLA_WIDTH:], g_a, x2d, L, tm)

    out = _ffn(x1, sh_f, sc_f, g_f, ffn_norm_w[0].reshape(1, D), w_up[0].astype(BF16), conv_w[0],
               conv_b[0].reshape(1, 2 * D_FF), w_down[0].astype(BF16), final_norm_w.reshape(1, D), L, tm)
    return out.reshape(B, L, D)
```

```python
import functools
import math

import jax
import jax.numpy as jnp
from jax import lax
from jax.experimental import pallas as pl
from jax.experimental.pallas import tpu as pltpu

F32 = jnp.float32
BF16 = jnp.bfloat16

D_MODEL = 1024
GLA_HEADS = 4
GLA_DK = 64
GLA_DV = 128
GLA_RANK = 16
GLA_GATE_TEMP = 16.0
GLA_CHUNK = 64
GLA_QK = GLA_HEADS * GLA_DK
GLA_WIDTH = GLA_HEADS * GLA_DV
DIFF_HEADS = 4
DIFF_DH = 64
DIFF_DV = 2 * DIFF_DH
DIFF_WIDTH = DIFF_HEADS * DIFF_DV
REL_BUCKETS = 32
D_FF = 2816
N_MOD = 6
EPS = 1e-6
LAMBDA_INIT = 0.8 - 0.6 * math.exp(-0.3 * 0)
LOG2E = 1.4426950408889634

BUCKET_STEPS = (1, 2, 3, 4, 5, 6, 7, 8, 12, 16, 23, 32, 46, 64, 91)
FAR_DIST = BUCKET_STEPS[-1]

VMEM_LIMIT = 56 * 1024 * 1024


def _split_bf16(a):
    hi = a.astype(BF16)
    lo = (a - hi.astype(F32)).astype(BF16)
    return hi, lo


def _dot(a, b):
    return jnp.dot(a, b, preferred_element_type=F32)


def _sigmoid(v):
    return 1.0 / (1.0 + jnp.exp(-v))


def _params(*sem):
    return pltpu.CompilerParams(dimension_semantics=sem, vmem_limit_bytes=VMEM_LIMIT)


def _ada_kernel(c_ref, w_ref, b_ref, o_ref):
    c = c_ref[...]
    s_hi, s_lo = _split_bf16(c * _sigmoid(c))
    w_hi, w_lo = _split_bf16(w_ref[...])
    o_ref[...] = (_dot(s_hi, w_hi) + _dot(s_lo, w_hi) + _dot(s_hi, w_lo)) + b_ref[...]


def _ada(c, w, b):
    B, D = c.shape
    N = w.shape[1]
    tn = 1024
    return pl.pallas_call(
        _ada_kernel,
        grid=(N // tn,),
        in_specs=[pl.BlockSpec((B, D), lambda j: (0, 0)),
                  pl.BlockSpec((D, tn), lambda j: (0, j)),
                  pl.BlockSpec((1, tn), lambda j: (0, j))],
        out_specs=pl.BlockSpec((B, tn), lambda j: (0, j)),
        out_shape=jax.ShapeDtypeStruct((B, N), F32),
        compiler_params=_params("arbitrary"),
        name="ada",
    )(c, w, b.reshape(1, N))


def _rms_mod(x, nw, sh, sc):
    ms = jnp.mean(x * x, axis=-1, keepdims=True)
    return (x * lax.rsqrt(ms + EPS) * nw) * (1.0 + sc) + sh


def _inproj_kernel(x_ref, sh_ref, sc_ref, nw_ref, wg_ref, wlr_ref, wd_ref, decw_ref, decb_ref,
                   qg_ref, kg_ref, vg_ref, rg_ref, laf_ref, lab_ref, qd_ref, kd_ref, vd_ref):
    h = _rms_mod(x_ref[...], nw_ref[...], sh_ref[0], sc_ref[0]).astype(BF16)
    g = _dot(h, wg_ref[...])
    qg_ref[...] = g[:, :GLA_QK] * (GLA_DK ** -0.5)
    kg_ref[...] = g[:, GLA_QK:2 * GLA_QK]
    vg_ref[...] = g[:, 2 * GLA_QK:2 * GLA_QK + GLA_WIDTH]
    rg_ref[...] = g[:, 2 * GLA_QK + GLA_WIDTH:]
    lr = _dot(h, wlr_ref[...])
    z = _dot(lr.astype(BF16), decw_ref[...]) + decb_ref[...]
    la = (jnp.minimum(z, 0.0) - jnp.log(1.0 + jnp.exp(-jnp.abs(z)))) * (1.0 / GLA_GATE_TEMP)
    laf_ref[...] = la[:, :GLA_QK]
    lab_ref[...] = la[:, GLA_QK:]
    d = _dot(h, wd_ref[...])
    qd_ref[...] = (d[:, :DIFF_WIDTH] * (DIFF_DH ** -0.5 * LOG2E)).astype(BF16)
    kd_ref[...] = d[:, DIFF_WIDTH:2 * DIFF_WIDTH].astype(BF16)
    vd_ref[...] = d[:, 2 * DIFF_WIDTH:].astype(BF16)


def _inproj(x2d, sh, sc, nw, wg, wlr, wd, decw, decb, L, tm):
    T, D = x2d.shape
    per_b = L // tm
    row = lambda i: (i, 0)
    full = lambda i: (0, 0)
    bvec = lambda i: (i // per_b, 0, 0)
    outs = [(GLA_QK, F32), (GLA_QK, F32), (GLA_WIDTH, F32), (GLA_WIDTH, F32), (GLA_QK, F32), (GLA_QK, F32),
            (DIFF_WIDTH, BF16), (DIFF_WIDTH, BF16), (DIFF_WIDTH, BF16)]
    return pl.pallas_call(
        _inproj_kernel,
        grid=(T // tm,),
        in_specs=[pl.BlockSpec((tm, D), row),
                  pl.BlockSpec((1, 1, D), bvec), pl.BlockSpec((1, 1, D), bvec),
                  pl.BlockSpec((1, D), full),
                  pl.BlockSpec(wg.shape, full), pl.BlockSpec(wlr.shape, full), pl.BlockSpec(wd.shape, full),
                  pl.BlockSpec(decw.shape, full), pl.BlockSpec(decb.shape, full)],
        out_specs=[pl.BlockSpec((tm, n), row) for n, _ in outs],
        out_shape=[jax.ShapeDtypeStruct((T, n), dt) for n, dt in outs],
        compiler_params=_params("arbitrary"),
        name="inproj",
    )(x2d, sh, sc, nw, wg, wlr, wd, decw, decb)


def _gla_kernel(qf_ref, kf_ref, vf_ref, laf_ref, qb_ref, kb_ref, vb_ref, lab_ref,
                of_ref, ob_ref, sf_ref, sb_ref, *, n_chunks, group):
    C = GLA_CHUNK

    @pl.when(pl.program_id(1) == 0)
    def _():
        sf_ref[...] = jnp.zeros_like(sf_ref)
        sb_ref[...] = jnp.zeros_like(sb_ref)

    r = lax.broadcasted_iota(jnp.int32, (C, C), 0)
    s = lax.broadcasted_iota(jnp.int32, (C, C), 1)
    tri_f = jnp.where(s <= r, 1.0, 0.0).astype(BF16)
    tri_b = jnp.where(s >= r, 1.0, 0.0).astype(BF16)
    r4 = lax.broadcasted_iota(jnp.int32, (GLA_HEADS * C, C), 0) % C
    s4 = lax.broadcasted_iota(jnp.int32, (GLA_HEADS * C, C), 1)
    mask_f = s4 <= r4
    mask_b = s4 > r4
    lane = lax.broadcasted_iota(jnp.int32, (C, GLA_QK), 1)
    head_masks = [(lane >= hd * GLA_DK) & (lane < (hd + 1) * GLA_DK) for hd in range(GLA_HEADS)]
    nt = (((1,), (1,)), ((), ()))
    tn = (((0,), (0,)), ((), ()))

    def per_head(x):
        zero = jnp.zeros_like(x)
        return [jnp.where(mk, x, zero) for mk in head_masks]

    dirs = ((qf_ref, kf_ref, vf_ref, laf_ref, of_ref, sf_ref, tri_f, mask_f, C - 1, False),
            (qb_ref, kb_ref, vb_ref, lab_ref, ob_ref, sb_ref, tri_b, mask_b, 0, True))

    def run_group(g):
        work = []
        for (q_ref, k_ref, v_ref, la_ref, o_ref, s_ref, tri, att_mask, last_row, rev) in dirs:
            for i in range(group):
                c = g * group + i
                c = (n_chunks - 1 - c) if rev else c
                work.append(dict(rows=pl.ds(pl.multiple_of(c * C, C), C), q_ref=q_ref, k_ref=k_ref, v_ref=v_ref,
                                 la_ref=la_ref, o_ref=o_ref, s_ref=s_ref, tri=tri, att_mask=att_mask,
                                 last_row=last_row))
        for w in work:
            la_hi, la_lo = _split_bf16(w["la_ref"][w["rows"], :])
            w["b"] = _dot(w["tri"], la_hi) + _dot(w["tri"], la_lo)
        for w in work:
            b = w["b"]
            tot = b[w["last_row"]:w["last_row"] + 1, :]
            q = w["q_ref"][w["rows"], :]
            k = w["k_ref"][w["rows"], :]
            w["dec"] = jnp.exp(tot)
            w["qs"] = jnp.concatenate(per_head((q * jnp.exp(b)).astype(BF16)), axis=0)
            w["k_in"] = (k * jnp.exp(-b)).astype(BF16)
            w["k_st"] = per_head((k * jnp.exp(tot - b)).astype(BF16))
            w["v"] = w["v_ref"][w["rows"], :].astype(BF16)
        for w in work:
            att = lax.dot_general(w["qs"], w["k_in"], nt, preferred_element_type=F32)
            w["att"] = jnp.where(w["att_mask"], att, 0.0).astype(BF16)
        for w in work:
            w["o"] = jnp.concatenate(
                [_dot(w["att"][hd * C:(hd + 1) * C, :], w["v"][:, hd * GLA_DV:(hd + 1) * GLA_DV])
                 for hd in range(GLA_HEADS)], axis=1)
        for w in work:
            upd = None
            for hd in range(GLA_HEADS):
                t = lax.dot_general(w["v"][:, hd * GLA_DV:(hd + 1) * GLA_DV], w["k_st"][hd], tn,
                                    preferred_element_type=F32)
                upd = t if upd is None else upd + t
            w["upd"] = upd
        for d in range(2):
            s_ref = dirs[d][5]
            st = s_ref[...]
            for w in work[d * group:(d + 1) * group]:
                oi = lax.dot_general(w["qs"], st.astype(BF16), nt, preferred_element_type=F32)
                oi = jnp.concatenate([oi[hd * C:(hd + 1) * C, :] for hd in range(GLA_HEADS)], axis=1)
                w["o_ref"][w["rows"], :] = w["o"] + oi
                st = st * w["dec"] + w["upd"]
            s_ref[...] = st

    n_groups = n_chunks // group
    if n_groups == 1:
        run_group(0)
    else:
        def body(g, carry):
            run_group(g)
            return carry
        lax.fori_loop(0, n_groups, body, 0)


def _gla(qg, kg, vg, laf, lab, B, L, tb):
    T = B * L
    J = L // tb
    fwd = lambda b, j: (b * J + j, 0)
    bwd = lambda b, j: (b * J + (J - 1 - j), 0)
    qk = lambda im: pl.BlockSpec((tb, GLA_QK), im)
    vv = lambda im: pl.BlockSpec((tb, GLA_WIDTH), im)
    return pl.pallas_call(
        functools.partial(_gla_kernel, n_chunks=tb // GLA_CHUNK, group=min(4, tb // GLA_CHUNK)),
        grid=(B, J),
        in_specs=[qk(fwd), qk(fwd), vv(fwd), qk(fwd), qk(bwd), qk(bwd), vv(bwd), qk(bwd)],
        out_specs=[vv(fwd), vv(bwd)],
        out_shape=[jax.ShapeDtypeStruct((T, GLA_WIDTH), F32)] * 2,
        scratch_shapes=[pltpu.VMEM((GLA_DV, GLA_QK), F32)] * 2,
        compiler_params=_params("arbitrary", "arbitrary"),
        name="gla",
    )(qg, kg, vg, laf, qg, kg, vg, lab)


def _bias_from_rel(rel, tab_ref, hd):
    n = jnp.abs(rel)
    half = REL_BUCKETS // 2
    neg = jnp.full(rel.shape, tab_ref[0, hd], F32)
    pos = jnp.full(rel.shape, tab_ref[half, hd], F32)
    for j, t in enumerate(BUCKET_STEPS):
        ge = n >= t
        neg = jnp.where(ge, tab_ref[j + 1, hd], neg)
        pos = jnp.where(ge, tab_ref[half + j + 1, hd], pos)
    return jnp.where(rel > 0, pos, neg)


KEY_CHUNK = 128
COL_TILE = 256
CLS_LEFT, CLS_RIGHT, CLS_STRIP0, CLS_GENERIC = 0, 1, 2, 99
STEP_LEFT, STEP_RIGHT, STEP_MIXED = 0, 1, 2


def _strips_kernel(tab_ref, o_ref, *, tq, n_strips):
    hd = pl.program_id(0)
    kk = lax.broadcasted_iota(jnp.int32, (KEY_CHUNK, tq), 0)
    qq = lax.broadcasted_iota(jnp.int32, (KEY_CHUNK, tq), 1)
    o_ref[0, CLS_LEFT] = jnp.full((KEY_CHUNK, tq), tab_ref[REL_BUCKETS // 2 - 1, hd], F32)
    o_ref[0, CLS_RIGHT] = jnp.full((KEY_CHUNK, tq), tab_ref[REL_BUCKETS - 1, hd], F32)
    for j in range(n_strips):
        o_ref[0, CLS_STRIP0 + j] = _bias_from_rel(KEY_CHUNK * (j - 1) + kk - qq, tab_ref, hd)


def _strips(table, tq, n_strips):
    H = table.shape[1]
    return pl.pallas_call(
        functools.partial(_strips_kernel, tq=tq, n_strips=n_strips),
        grid_spec=pltpu.PrefetchScalarGridSpec(
            num_scalar_prefetch=1, grid=(H,), in_specs=[],
            out_specs=pl.BlockSpec((1, CLS_STRIP0 + n_strips, KEY_CHUNK, tq), lambda h, *_: (h, 0, 0, 0))),
        out_shape=jax.ShapeDtypeStruct((H, CLS_STRIP0 + n_strips, KEY_CHUNK, tq), F32),
        compiler_params=_params("arbitrary"),
        name="bias_strips",
    )(table)


def _diff_finish(acc, l, lq1_ref, lk1_ref, lq2_ref, lk2_ref, nw_ref, o_ref, tq):
    lam = (jnp.exp(jnp.sum(lq1_ref[...] * lk1_ref[...], axis=1, keepdims=True))
           - jnp.exp(jnp.sum(lq2_ref[...] * lk2_ref[...], axis=1, keepdims=True)) + LAMBDA_INIT)
    o = acc[:, :tq] / l[:, :tq] - lam * (acc[:, tq:] / l[:, tq:])
    ms = jnp.mean(o * o, axis=0, keepdims=True)
    nw = jnp.concatenate([nw_ref[...]] * (tq // 128), axis=1)
    o = o * lax.rsqrt(ms + EPS) * nw * (1.0 - LAMBDA_INIT)
    o_ref[0] = o.T.astype(o_ref.dtype)


def _stack_maps(qT):
    row = lax.broadcasted_iota(jnp.int32, qT.shape, 0)
    zero = jnp.zeros_like(qT)
    return jnp.concatenate([jnp.where(row < DIFF_DH, qT, zero), jnp.where(row >= DIFF_DH, qT, zero)], axis=1)


ONES_ROWS = 16


def _diff_fast_kernel(chunkcls_ref, tab_ref,
                      qT_ref, k_ref, vT_ref, strip_ref, lq1_ref, lk1_ref, lq2_ref, lk2_ref, nw_ref,
                      o_ref, m_ref, acc_ref, sa_ref, sb_ref, smaxa_ref, smaxb_ref, *, tq, tk, n_kt):
    hd = pl.program_id(1)
    r = pl.program_id(0) * pl.num_programs(2) + pl.program_id(2)
    n_ct = 2 * tq // COL_TILE
    n_chunk = tk // KEY_CHUNK
    qstack = _stack_maps(qT_ref[0])
    bias_max = functools.reduce(jnp.maximum, [tab_ref[i, hd] for i in range(REL_BUCKETS)])
    m_ref[...] = jnp.full(m_ref.shape, -1e30, F32)
    acc_ref[...] = jnp.zeros(acc_ref.shape, F32)

    def scores_tile(kt, buf, j):
        s_ref, smax_ref = buf
        kblk = k_ref[0, pl.ds(pl.multiple_of(kt * tk, tk), tk), :]
        cols = slice(j * COL_TILE, (j + 1) * COL_TILE)
        s = _dot(kblk, qstack[:, cols])
        s_ref[:, cols] = s
        smax_ref[:, cols] = jnp.max(s, axis=0, keepdims=True)

    def chain(kt, buf, j):
        s_ref, smax_ref = buf
        cols = slice(j * COL_TILE, (j + 1) * COL_TILE)
        q0 = (j * COL_TILE) % tq
        m_old = m_ref[:, cols]
        m_new = jnp.maximum(m_old, smax_ref[:, cols] + bias_max)
        alpha = jnp.exp2(m_old - m_new)
        ps = []
        for c in range(n_chunk):
            rows = slice(c * KEY_CHUNK, (c + 1) * KEY_CHUNK)
            bias = strip_ref[0, chunkcls_ref[r, kt * n_chunk + c], :, q0:q0 + COL_TILE]
            ps.append(jnp.exp2((s_ref[rows, cols] + bias) - m_new).astype(BF16))
        vblk = vT_ref[0, 0, :, pl.ds(pl.multiple_of(kt * tk, tk), tk)]
        acc_ref[:, cols] = alpha * acc_ref[:, cols] + _dot(vblk, jnp.concatenate(ps, axis=0))
        m_ref[:, cols] = m_new

    def step(kt, cur, nxt):
        for j in range(n_ct):
            if nxt is not None:
                scores_tile(kt + 1, nxt, j)
            chain(kt, cur, j)

    buf_a = (sa_ref, smaxa_ref)
    buf_b = (sb_ref, smaxb_ref)
    for j in range(n_ct):
        scores_tile(0, buf_a, j)

    def pair(kk, carry):
        step(2 * kk, buf_a, buf_b)
        step(2 * kk + 1, buf_b, buf_a)
        return carry

    lax.fori_loop(0, n_kt // 2 - 1, pair, 0)
    step(n_kt - 2, buf_a, buf_b)
    step(n_kt - 1, buf_b, None)
    acc = acc_ref[...]
    _diff_finish(acc[:DIFF_DV], acc[DIFF_DV:DIFF_DV + 1], lq1_ref, lk1_ref, lq2_ref, lk2_ref, nw_ref, o_ref, tq)


def _diff_kernel(stepcls_ref, chunkcls_ref, tab_ref,
                 qT_ref, k_ref, vT_ref, strip_ref, posq_ref, posk_ref, lq1_ref, lk1_ref, lq2_ref, lk2_ref, nw_ref,
                 o_ref, m_ref, l_ref, acc_ref, sa_ref, sb_ref, smaxa_ref, smaxb_ref, bias_ref,
                 *, tq, tk, n_kt, n_strips):
    hd = pl.program_id(1)
    r = pl.program_id(0) * pl.num_programs(2) + pl.program_id(2)
    half = REL_BUCKETS // 2
    n_ct = 2 * tq // COL_TILE
    n_chunk = tk // KEY_CHUNK

    qstack = _stack_maps(qT_ref[0])

    m_ref[...] = jnp.full(m_ref.shape, -1e30, F32)
    l_ref[...] = jnp.zeros(l_ref.shape, F32)
    acc_ref[...] = jnp.zeros(acc_ref.shape, F32)

    bias_max = functools.reduce(jnp.maximum, [tab_ref[i, hd] for i in range(REL_BUCKETS)])

    def scores_tile(kt, buf, j):
        s_ref, smax_ref = buf
        kblk = k_ref[0, pl.ds(pl.multiple_of(kt * tk, tk), tk), :]
        cols = slice(j * COL_TILE, (j + 1) * COL_TILE)
        s = _dot(kblk, qstack[:, cols])
        s_ref[:, cols] = s
        smax_ref[:, cols] = jnp.max(s, axis=0, keepdims=True)

    def scores(kt, buf):
        for j in range(n_ct):
            scores_tile(kt, buf, j)

    def build_bias(kt):
        posq = posq_ref[0]
        for c in range(n_chunk):
            cc = chunkcls_ref[r, kt * n_chunk + c]
            rows = slice(c * KEY_CHUNK, (c + 1) * KEY_CHUNK)

            @pl.when(cc == CLS_LEFT)
            def _():
                bias_ref[rows, :] = jnp.full((KEY_CHUNK, tq), tab_ref[half - 1, hd], F32)

            @pl.when(cc == CLS_RIGHT)
            def _():
                bias_ref[rows, :] = jnp.full((KEY_CHUNK, tq), tab_ref[REL_BUCKETS - 1, hd], F32)

            @pl.when(jnp.logical_and(cc >= CLS_STRIP0, cc < CLS_STRIP0 + n_strips))
            def _():
                bias_ref[rows, :] = strip_ref[0, cc]

            @pl.when(cc == CLS_GENERIC)
            def _():
                k0 = pl.multiple_of(kt * tk + c * KEY_CHUNK, KEY_CHUNK)
                pk = posk_ref[0, pl.ds(k0, KEY_CHUNK), :]
                pk = jnp.concatenate([pk] * (tq // 128), axis=1)
                bias_ref[rows, :] = _bias_from_rel(pk - posq, tab_ref, hd)

    def softmax_pv(kt, cur, const, nxt):
        s_ref, smax_ref = cur
        vblk = vT_ref[0, :, pl.ds(pl.multiple_of(kt * tk, tk), tk)]
        for j in range(n_ct):
            if nxt is not None:
                scores_tile(kt + 1, nxt, j)
            cols = slice(j * COL_TILE, (j + 1) * COL_TILE)
            shift = bias_max if const is None else const
            m_old = m_ref[:, cols]
            m_new = jnp.maximum(m_old, smax_ref[:, cols] + shift)
            alpha = jnp.exp2(m_old - m_new)
            if const is None:
                q0 = (j * COL_TILE) % tq
                p = jnp.exp2((s_ref[:, cols] + bias_ref[:, q0:q0 + COL_TILE]) - m_new)
            else:
                p = jnp.exp2(s_ref[:, cols] - (m_new - shift))
            l_ref[:, cols] = alpha * l_ref[:, cols] + jnp.sum(p, axis=0, keepdims=True)
            acc_ref[:, cols] = alpha * acc_ref[:, cols] + _dot(vblk, p.astype(BF16))
            m_ref[:, cols] = m_new

    def step(kt, cur, nxt):
        cls = stepcls_ref[r, kt]

        @pl.when(cls == STEP_MIXED)
        def _():
            build_bias(kt)
            softmax_pv(kt, cur, None, nxt)

        @pl.when(cls != STEP_MIXED)
        def _():
            const = jnp.where(cls == STEP_LEFT, tab_ref[half - 1, hd], tab_ref[REL_BUCKETS - 1, hd])
            softmax_pv(kt, cur, const, nxt)

    buf_a = (sa_ref, smaxa_ref)
    buf_b = (sb_ref, smaxb_ref)
    scores(0, buf_a)

    def pair(kk, carry):
        step(2 * kk, buf_a, buf_b)
        step(2 * kk + 1, buf_b, buf_a)
        return carry

    lax.fori_loop(0, n_kt // 2 - 1, pair, 0)
    step(n_kt - 2, buf_a, buf_b)
    step(n_kt - 1, buf_b, None)

    _diff_finish(acc_ref[...], l_ref[...], lq1_ref, lk1_ref, lq2_ref, lk2_ref, nw_ref, o_ref, tq)


def _classify(positions, tq, tk, n_strips):
    B, L = positions.shape
    nq, nc = L // tq, L // KEY_CHUNK
    pq = positions.reshape(B, nq, tq)
    pk = positions.reshape(B, nc, KEY_CHUNK)
    qlo, qhi, q0 = pq.min(axis=2), pq.max(axis=2), pq[:, :, 0]
    klo, khi, k0 = pk.min(axis=2), pk.max(axis=2), pk[:, :, 0]
    q_run = jnp.all(pq == q0[:, :, None] + jnp.arange(tq, dtype=positions.dtype), axis=2)
    k_run = jnp.all(pk == k0[:, :, None] + jnp.arange(KEY_CHUNK, dtype=positions.dtype), axis=2)
    left = (qlo[:, :, None] - khi[:, None, :]) >= FAR_DIST
    right = (klo[:, None, :] - qhi[:, :, None]) >= FAR_DIST
    d = k0[:, None, :] - q0[:, :, None]
    j = d // KEY_CHUNK + 1
    strip_ok = (q_run[:, :, None] & k_run[:, None, :] & (d % KEY_CHUNK == 0) & (j >= 0) & (j < n_strips))
    cls = jnp.where(left, CLS_LEFT, jnp.where(right, CLS_RIGHT, jnp.where(strip_ok, CLS_STRIP0 + j, CLS_GENERIC)))
    cls = cls.astype(jnp.int32)
    per_step = cls.reshape(B, nq, L // tk, tk // KEY_CHUNK)
    step = jnp.where(jnp.all(per_step == CLS_LEFT, axis=3), STEP_LEFT,
                     jnp.where(jnp.all(per_step == CLS_RIGHT, axis=3), STEP_RIGHT, STEP_MIXED)).astype(jnp.int32)
    return step.reshape(B * nq, L // tk), cls.reshape(B * nq, nc)


def _diffattn(qT, kd, vT, positions, table, lq1, lk1, lq2, lk2, nw, tq, tk):
    B, _, L = qT.shape
    nq, nk = L // tq, L // tk
    n_strips = tq // KEY_CHUNK + 2
    n_cls = CLS_STRIP0 + n_strips
    table2 = table * LOG2E
    strips = _strips(table2, tq, n_strips)
    stepcls, chunkcls = _classify(positions, tq, tk, n_strips)
    nw_b = jnp.broadcast_to(nw.reshape(DIFF_DV, 1), (DIFF_DV, 128))
    vecs = [a.reshape(1, DIFF_DH) for a in (lq1, lk1, lq2, lk2)]
    full2 = lambda b, h, i, *_: (0, 0)
    q_spec = pl.BlockSpec((1, DIFF_DV, tq), lambda b, h, i, *_: (b, h, i))
    k_spec = pl.BlockSpec((1, L, DIFF_DV), lambda b, h, i, *_: (b, 0, h))
    strip_spec = pl.BlockSpec((1, n_cls, KEY_CHUNK, tq), lambda b, h, i, *_: (h, 0, 0, 0))
    tail_specs = [pl.BlockSpec((1, DIFF_DH), full2)] * 4 + [pl.BlockSpec((DIFF_DV, 128), full2)]
    out_spec = pl.BlockSpec((1, tq, DIFF_DV), lambda b, h, i, *_: (b, i, h))
    out_shape = jax.ShapeDtypeStruct((B, L, DIFF_WIDTH), BF16)
    s_buf = pltpu.VMEM((tk, 2 * tq), F32)
    row_buf = pltpu.VMEM((1, 2 * tq), F32)

    def fast(_):
        ones = jnp.ones((B, DIFF_HEADS, ONES_ROWS, L), BF16)
        vT1 = jnp.concatenate([vT.reshape(B, DIFF_HEADS, DIFF_DV, L), ones], axis=2)
        return pl.pallas_call(
            functools.partial(_diff_fast_kernel, tq=tq, tk=tk, n_kt=nk),
            grid_spec=pltpu.PrefetchScalarGridSpec(
                num_scalar_prefetch=2, grid=(B, DIFF_HEADS, nq),
                in_specs=[q_spec, k_spec,
                          pl.BlockSpec((1, 1, DIFF_DV + ONES_ROWS, L), lambda b, h, i, *_: (b, h, 0, 0)),
                          strip_spec] + tail_specs,
                out_specs=out_spec,
                scratch_shapes=[row_buf, pltpu.VMEM((DIFF_DV + ONES_ROWS, 2 * tq), F32),
                                s_buf, s_buf, row_buf, row_buf]),
            out_shape=out_shape,
            compiler_params=_params("arbitrary", "arbitrary", "arbitrary"),
            name="diffattn",
        )(chunkcls, table2, qT, kd, vT1, strips, *vecs, nw_b)

    def general(_):
        posq = positions.reshape(B, 1, L)
        posk = jnp.broadcast_to(positions[:, :, None], (B, L, 128))
        return pl.pallas_call(
            functools.partial(_diff_kernel, tq=tq, tk=tk, n_kt=nk, n_strips=n_strips),
            grid_spec=pltpu.PrefetchScalarGridSpec(
                num_scalar_prefetch=3, grid=(B, DIFF_HEADS, nq),
                in_specs=[q_spec, k_spec,
                          pl.BlockSpec((1, DIFF_DV, L), lambda b, h, i, *_: (b, h, 0)),
                          strip_spec,
                          pl.BlockSpec((1, 1, tq), lambda b, h, i, *_: (b, 0, i)),
                          pl.BlockSpec((1, L, 128), lambda b, h, i, *_: (b, 0, 0))] + tail_specs,
                out_specs=out_spec,
                scratch_shapes=[row_buf, row_buf, pltpu.VMEM((DIFF_DV, 2 * tq), F32), s_buf, s_buf,
                                row_buf, row_buf, pltpu.VMEM((tk, tq), F32)]),
            out_shape=out_shape,
            compiler_params=_params("arbitrary", "arbitrary", "arbitrary"),
            name="diffattn_general",
        )(stepcls, chunkcls, table2, qT, kd, vT, strips, posq, posk, *vecs, nw_b)

    return lax.cond(jnp.all(chunkcls != CLS_GENERIC), fast, general, 0)


def _outproj_kernel(of_ref, ob_ref, rg_ref, gnw_ref, od_ref, wa_ref, wb_ref, ga_ref, x_ref, o_ref):
    o = of_ref[...] + ob_ref[...]
    parts = []
    for hd in range(GLA_HEADS):
        oh = o[:, hd * GLA_DV:(hd + 1) * GLA_DV]
        ms = jnp.mean(oh * oh, axis=-1, keepdims=True)
        parts.append(oh * lax.rsqrt(ms + EPS))
    r = rg_ref[...]
    oa = jnp.concatenate(parts, axis=1) * gnw_ref[...] * (r * _sigmoid(r))
    mixed = _dot(oa.astype(BF16), wa_ref[...]) + _dot(od_ref[...], wb_ref[...])
    o_ref[...] = x_ref[...] + ga_ref[0] * mixed


def _outproj(of, ob, rg, gnw, od, wa, wb, ga, x2d, L, tm):
    T, D = x2d.shape
    per_b = L // tm
    row = lambda i: (i, 0)
    full = lambda i: (0, 0)
    bvec = lambda i: (i // per_b, 0, 0)
    return pl.pallas_call(
        _outproj_kernel,
        grid=(T // tm,),
        in_specs=[pl.BlockSpec((tm, GLA_WIDTH), row), pl.BlockSpec((tm, GLA_WIDTH), row),
                  pl.BlockSpec((tm, GLA_WIDTH), row), pl.BlockSpec((1, GLA_WIDTH), full),
                  pl.BlockSpec((tm, DIFF_WIDTH), row),
                  pl.BlockSpec(wa.shape, full), pl.BlockSpec(wb.shape, full),
                  pl.BlockSpec((1, 1, D), bvec), pl.BlockSpec((tm, D), row)],
        out_specs=pl.BlockSpec((tm, D), row),
        out_shape=jax.ShapeDtypeStruct((T, D), F32),
        compiler_params=_params("arbitrary"),
        name="outproj",
    )(of, ob, rg, gnw, od, wa, wb, ga, x2d)


HALO = 8
FF_CHUNK = 256


def _ffn_kernel(x_ref, xp_ref, xn_ref, sh_ref, sc_ref, gf_ref, nw_ref, wup_ref, cw_ref, cb_ref, wdn_ref, fw_ref,
                o_ref, acc_ref, *, tm, per_b):
    i = pl.program_id(0)
    first = (i % per_b) == 0
    last = (i % per_b) == per_b - 1
    nw, sh, sc = nw_ref[...], sh_ref[0], sc_ref[0]
    x = x_ref[...]
    h_mid = _rms_mod(x, nw, sh, sc)
    h_prev = jnp.where(first, 0.0, _rms_mod(xp_ref[...], nw, sh, sc))
    h_next = jnp.where(last, 0.0, _rms_mod(xn_ref[...], nw, sh, sc))
    h = jnp.concatenate([h_prev, h_mid, h_next], axis=0).astype(BF16)
    rows = tm + 2 * HALO

    def conv(u, col):
        w = cw_ref[:, col:col + FF_CHUNK]
        up = pltpu.roll(u, 1, 0)[HALO:HALO + tm]
        un = pltpu.roll(u, rows - 1, 0)[HALO:HALO + tm]
        return (w[0:1] * up + w[1:2] * u[HALO:HALO + tm] + w[2:3] * un) + cb_ref[:, col:col + FF_CHUNK]

    def up(n):
        cg, cv = n * FF_CHUNK, D_FF + n * FF_CHUNK
        return _dot(h, wup_ref[:, cg:cg + FF_CHUNK]), _dot(h, wup_ref[:, cv:cv + FF_CHUNK])

    n_chunks = D_FF // FF_CHUNK
    u_gate, u_val = up(0)
    for n in range(n_chunks):
        cg, cv = n * FF_CHUNK, D_FF + n * FF_CHUNK
        if n + 1 < n_chunks:
            nxt = up(n + 1)
        gate = conv(u_gate, cg)
        val = conv(u_val, cv)
        a = (gate * _sigmoid(gate) * val).astype(BF16)
        part = _dot(a, wdn_ref[cg:cg + FF_CHUNK, :])
        if n == 0:
            acc_ref[...] = part
        else:
            acc_ref[...] += part
        if n + 1 < n_chunks:
            u_gate, u_val = nxt
    y = x + gf_ref[0] * acc_ref[...]
    ms = jnp.mean(y * y, axis=-1, keepdims=True)
    o_ref[...] = y * lax.rsqrt(ms + EPS) * fw_ref[...]


def _ffn(x1, sh, sc, gf, nw, wup, cw, cb, wdn, fw, L, tm):
    T, D = x1.shape
    per_b = L // tm
    hb = tm // HALO
    n_halo = T // HALO
    row = lambda i: (i, 0)
    full = lambda i: (0, 0)
    bvec = lambda i: (i // per_b, 0, 0)
    return pl.pallas_call(
        functools.partial(_ffn_kernel, tm=tm, per_b=per_b),
        grid=(T // tm,),
        in_specs=[pl.BlockSpec((tm, D), row),
                  pl.BlockSpec((HALO, D), lambda i: (jnp.maximum(i * hb - 1, 0), 0)),
                  pl.BlockSpec((HALO, D), lambda i: (jnp.minimum((i + 1) * hb, n_halo - 1), 0)),
                  pl.BlockSpec((1, 1, D), bvec), pl.BlockSpec((1, 1, D), bvec), pl.BlockSpec((1, 1, D), bvec),
                  pl.BlockSpec((1, D), full),
                  pl.BlockSpec(wup.shape, full), pl.BlockSpec(cw.shape, full), pl.BlockSpec(cb.shape, full),
                  pl.BlockSpec(wdn.shape, full), pl.BlockSpec((1, D), full)],
        out_specs=pl.BlockSpec((tm, D), row),
        out_shape=jax.ShapeDtypeStruct((T, D), F32),
        scratch_shapes=[pltpu.VMEM((tm, D), F32)],
        compiler_params=_params("arbitrary"),
        name="ffn",
    )(x1, x1, x1, sh, sc, gf, nw, wup, cw, cb, wdn, fw)


def kernel(x, c, positions, w_ada, b_ada, attn_norm_w, w_in, gla_dec_w_fwd, gla_dec_b_fwd, gla_dec_w_bwd,
           gla_dec_b_bwd, gla_norm_w, diff_lambda_q1, diff_lambda_k1, diff_lambda_q2, diff_lambda_k2,
           diff_norm_w, rel_bias_table, w_out, ffn_norm_w, w_up, conv_w, conv_b, w_down, final_norm_w):
    B, L, D = x.shape
    T = B * L
    tm = min(512, L)
    x2d = x.reshape(T, D)

    mod = _ada(c, w_ada[0], b_ada[0])
    sh_a, sc_a, g_a, sh_f, sc_f, g_f = [m.reshape(B, 1, D) for m in jnp.split(mod, N_MOD, axis=-1)]

    o_lr = 2 * GLA_QK + 2 * GLA_WIDTH
    o_d = o_lr + 2 * GLA_RANK
    w = w_in[0]
    wg = w[:, :o_lr].astype(BF16)
    wlr = jnp.pad(w[:, o_lr:o_d], ((0, 0), (0, 128 - 2 * GLA_RANK))).astype(BF16)
    wd = w[:, o_d:].astype(BF16)
    decw = jnp.zeros((128, 2 * GLA_QK), F32)
    decw = decw.at[:GLA_RANK, :GLA_QK].set(gla_dec_w_fwd[0]).at[GLA_RANK:2 * GLA_RANK, GLA_QK:].set(gla_dec_w_bwd[0])
    decb = jnp.concatenate([gla_dec_b_fwd[0], gla_dec_b_bwd[0]]).reshape(1, 2 * GLA_QK)

    qg, kg, vg, rg, laf, lab, qd, kd, vd = _inproj(
        x2d, sh_a, sc_a, attn_norm_w[0].reshape(1, D), wg, wlr, wd, decw.astype(BF16), decb, L, tm)

    o_f, o_b = _gla(qg, kg, vg, laf, lab, B, L, min(512, L))

    qT = jnp.swapaxes(qd.reshape(B, L, DIFF_WIDTH), 1, 2)
    vT = jnp.swapaxes(vd.reshape(B, L, DIFF_WIDTH), 1, 2)
    o_d = _diffattn(qT, kd.reshape(B, L, DIFF_WIDTH), vT, positions, rel_bias_table,
                    diff_lambda_q1[0], diff_lambda_k1[0], diff_lambda_q2[0], diff_lambda_k2[0],
                    diff_norm_w[0], min(512, L), min(1024, L // 2))

    wo = w_out[0].astype(BF16)
    x1 = _outproj(o_f, o_b, rg, gla_norm_w[0].reshape(1, GLA_WIDTH), o_d.reshape(T, DIFF_WIDTH),
                  wo[:GLA_WIDTH], wo[GLA_WIDTH:], g_a, x2d, L, tm)

    out = _ffn(x1, sh_f, sc_f, g_f, ffn_norm_w[0].reshape(1, D), w_up[0].astype(BF16), conv_w[0],
               conv_b[0].reshape(1, 2 * D_FF), w_down[0].astype(BF16), final_norm_w.reshape(1, D), L, tm)
    return out.reshape(B, L, D)
```

```python
import functools
import math

import jax
import jax.numpy as jnp
from jax import lax
from jax.experimental import pallas as pl
from jax.experimental.pallas import tpu as pltpu

F32 = jnp.float32
BF16 = jnp.bfloat16

D_MODEL = 1024
GLA_HEADS = 4
GLA_DK = 64
GLA_DV = 128
GLA_RANK = 16
GLA_GATE_TEMP = 16.0
GLA_CHUNK = 64
GLA_QK = GLA_HEADS * GLA_DK
GLA_WIDTH = GLA_HEADS * GLA_DV
DIFF_HEADS = 4
DIFF_DH = 64
DIFF_DV = 2 * DIFF_DH
DIFF_WIDTH = DIFF_HEADS * DIFF_DV
REL_BUCKETS = 32
D_FF = 2816
N_MOD = 6
EPS = 1e-6
LAMBDA_INIT = 0.8 - 0.6 * math.exp(-0.3 * 0)
LOG2E = 1.4426950408889634
ONES_ROWS = 16

BUCKET_STEPS = (1, 2, 3, 4, 5, 6, 7, 8, 12, 16, 23, 32, 46, 64, 91)
FAR_DIST = BUCKET_STEPS[-1]

VMEM_LIMIT = 56 * 1024 * 1024


def _split_bf16(a):
    hi = a.astype(BF16)
    lo = (a - hi.astype(F32)).astype(BF16)
    return hi, lo


def _dot(a, b):
    return jnp.dot(a, b, preferred_element_type=F32)


def _sigmoid(v):
    return 1.0 / (1.0 + jnp.exp(-v))


def _params(*sem):
    return pltpu.CompilerParams(dimension_semantics=sem, vmem_limit_bytes=VMEM_LIMIT)


def _ada_kernel(c_ref, w_ref, b_ref, o_ref):
    c = c_ref[...]
    s_hi, s_lo = _split_bf16(c * _sigmoid(c))
    w_hi, w_lo = _split_bf16(w_ref[...])
    o_ref[...] = (_dot(s_hi, w_hi) + _dot(s_lo, w_hi) + _dot(s_hi, w_lo)) + b_ref[...]


def _ada(c, w, b):
    B, D = c.shape
    N = w.shape[1]
    tn = 1024
    return pl.pallas_call(
        _ada_kernel,
        grid=(N // tn,),
        in_specs=[pl.BlockSpec((B, D), lambda j: (0, 0)),
                  pl.BlockSpec((D, tn), lambda j: (0, j)),
                  pl.BlockSpec((1, tn), lambda j: (0, j))],
        out_specs=pl.BlockSpec((B, tn), lambda j: (0, j)),
        out_shape=jax.ShapeDtypeStruct((B, N), F32),
        compiler_params=_params("arbitrary"),
        name="ada",
    )(c, w, b.reshape(1, N))


def _rms_mod(x, nw, sh, sc):
    ms = jnp.mean(x * x, axis=-1, keepdims=True)
    return (x * lax.rsqrt(ms + EPS) * nw) * (1.0 + sc) + sh


def _inproj_kernel(x_ref, sh_ref, sc_ref, nw_ref, wg_ref, wlr_ref, wd_ref, decw_ref, decb_ref,
                   qg_ref, kg_ref, vg_ref, rg_ref, laf_ref, lab_ref, qd_ref, kd_ref, vd_ref):
    h = _rms_mod(x_ref[...], nw_ref[...], sh_ref[0], sc_ref[0]).astype(BF16)
    g = _dot(h, wg_ref[...])
    qg_ref[...] = g[:, :GLA_QK] * (GLA_DK ** -0.5)
    kg_ref[...] = g[:, GLA_QK:2 * GLA_QK]
    vg_ref[...] = g[:, 2 * GLA_QK:2 * GLA_QK + GLA_WIDTH].astype(BF16)
    rg_ref[...] = g[:, 2 * GLA_QK + GLA_WIDTH:]
    lr = _dot(h, wlr_ref[...])
    z = _dot(lr.astype(BF16), decw_ref[...]) + decb_ref[...]
    la = (jnp.minimum(z, 0.0) - jnp.log(1.0 + jnp.exp(-jnp.abs(z)))) * (1.0 / GLA_GATE_TEMP)
    laf_ref[...] = la[:, :GLA_QK]
    lab_ref[...] = la[:, GLA_QK:]
    d = _dot(h, wd_ref[...])
    qd_ref[0] = (d[:, :DIFF_WIDTH] * (DIFF_DH ** -0.5 * LOG2E)).T.astype(BF16)
    kd_ref[...] = d[:, DIFF_WIDTH:2 * DIFF_WIDTH].astype(BF16)
    vt = d[:, 2 * DIFF_WIDTH:].T.astype(BF16)
    for hd in range(DIFF_HEADS):
        vd_ref[0, hd, :DIFF_DV, :] = vt[hd * DIFF_DV:(hd + 1) * DIFF_DV, :]
    vd_ref[0, :, DIFF_DV:, :] = jnp.ones((DIFF_HEADS, ONES_ROWS, vt.shape[1]), BF16)


def _inproj(x2d, sh, sc, nw, wg, wlr, wd, decw, decb, L, tm):
    T, D = x2d.shape
    per_b = L // tm
    row = lambda i: (i, 0)
    full = lambda i: (0, 0)
    bvec = lambda i: (i // per_b, 0, 0)
    B = T // L
    outs = [(GLA_QK, F32), (GLA_QK, F32), (GLA_WIDTH, BF16), (GLA_WIDTH, F32), (GLA_QK, F32), (GLA_QK, F32)]
    col = lambda i: (i // per_b, 0, i % per_b)
    col4 = lambda i: (i // per_b, 0, 0, i % per_b)
    return pl.pallas_call(
        _inproj_kernel,
        grid=(T // tm,),
        in_specs=[pl.BlockSpec((tm, D), row),
                  pl.BlockSpec((1, 1, D), bvec), pl.BlockSpec((1, 1, D), bvec),
                  pl.BlockSpec((1, D), full),
                  pl.BlockSpec(wg.shape, full), pl.BlockSpec(wlr.shape, full), pl.BlockSpec(wd.shape, full),
                  pl.BlockSpec(decw.shape, full), pl.BlockSpec(decb.shape, full)],
        out_specs=[pl.BlockSpec((tm, n), row) for n, _ in outs] + [
            pl.BlockSpec((1, DIFF_WIDTH, tm), col), pl.BlockSpec((tm, DIFF_WIDTH), row),
            pl.BlockSpec((1, DIFF_HEADS, DIFF_DV + ONES_ROWS, tm), col4)],
        out_shape=[jax.ShapeDtypeStruct((T, n), dt) for n, dt in outs] + [
            jax.ShapeDtypeStruct((B, DIFF_WIDTH, L), BF16), jax.ShapeDtypeStruct((T, DIFF_WIDTH), BF16),
            jax.ShapeDtypeStruct((B, DIFF_HEADS, DIFF_DV + ONES_ROWS, L), BF16)],
        compiler_params=_params("arbitrary"),
        name="inproj",
    )(x2d, sh, sc, nw, wg, wlr, wd, decw, decb)


def _gla_kernel(qf_ref, kf_ref, vf_ref, laf_ref, qb_ref, kb_ref, vb_ref, lab_ref,
                of_ref, ob_ref, sf_ref, sb_ref, *, n_chunks, group):
    C = GLA_CHUNK

    @pl.when(pl.program_id(1) == 0)
    def _():
        sf_ref[...] = jnp.zeros_like(sf_ref)
        sb_ref[...] = jnp.zeros_like(sb_ref)

    r = lax.broadcasted_iota(jnp.int32, (C, C), 0)
    s = lax.broadcasted_iota(jnp.int32, (C, C), 1)
    tri_f = jnp.where(s <= r, 1.0, 0.0).astype(BF16)
    tri_b = jnp.where(s >= r, 1.0, 0.0).astype(BF16)
    r4 = lax.broadcasted_iota(jnp.int32, (GLA_HEADS * C, C), 0) % C
    s4 = lax.broadcasted_iota(jnp.int32, (GLA_HEADS * C, C), 1)
    mask_f = s4 <= r4
    mask_b = s4 > r4
    lane = lax.broadcasted_iota(jnp.int32, (C, GLA_QK), 1)
    head_masks = [(lane >= hd * GLA_DK) & (lane < (hd + 1) * GLA_DK) for hd in range(GLA_HEADS)]
    nt = (((1,), (1,)), ((), ()))
    tn = (((0,), (0,)), ((), ()))

    def per_head(x):
        zero = jnp.zeros_like(x)
        return [jnp.where(mk, x, zero) for mk in head_masks]

    dirs = ((qf_ref, kf_ref, vf_ref, laf_ref, of_ref, sf_ref, tri_f, mask_f, C - 1, False),
            (qb_ref, kb_ref, vb_ref, lab_ref, ob_ref, sb_ref, tri_b, mask_b, 0, True))

    def run_group(g):
        work = []
        for (q_ref, k_ref, v_ref, la_ref, o_ref, s_ref, tri, att_mask, last_row, rev) in dirs:
            for i in range(group):
                c = g * group + i
                c = (n_chunks - 1 - c) if rev else c
                work.append(dict(rows=pl.ds(pl.multiple_of(c * C, C), C), q_ref=q_ref, k_ref=k_ref, v_ref=v_ref,
                                 la_ref=la_ref, o_ref=o_ref, s_ref=s_ref, tri=tri, att_mask=att_mask,
                                 last_row=last_row))
        for w in work:
            la_hi, la_lo = _split_bf16(w["la_ref"][w["rows"], :])
            w["b"] = _dot(w["tri"], la_hi) + _dot(w["tri"], la_lo)
        for w in work:
            b = w["b"]
            tot = b[w["last_row"]:w["last_row"] + 1, :]
            q = w["q_ref"][w["rows"], :]
            k = w["k_ref"][w["rows"], :]
            w["dec"] = jnp.exp(tot)
            w["qs"] = jnp.concatenate(per_head((q * jnp.exp(b)).astype(BF16)), axis=0)
            w["k_in"] = (k * jnp.exp(-b)).astype(BF16)
            w["k_st"] = per_head((k * jnp.exp(tot - b)).astype(BF16))
            w["v"] = w["v_ref"][w["rows"], :]
        for w in work:
            att = lax.dot_general(w["qs"], w["k_in"], nt, preferred_element_type=F32)
            w["att"] = jnp.where(w["att_mask"], att, 0.0).astype(BF16)
        for w in work:
            w["o"] = jnp.concatenate(
                [_dot(w["att"][hd * C:(hd + 1) * C, :], w["v"][:, hd * GLA_DV:(hd + 1) * GLA_DV])
                 for hd in range(GLA_HEADS)], axis=1)
        for w in work:
            upd = None
            for hd in range(GLA_HEADS):
                t = lax.dot_general(w["v"][:, hd * GLA_DV:(hd + 1) * GLA_DV], w["k_st"][hd], tn,
                                    preferred_element_type=F32)
                upd = t if upd is None else upd + t
            w["upd"] = upd
        for d in range(2):
            s_ref = dirs[d][5]
            st = s_ref[...]
            for w in work[d * group:(d + 1) * group]:
                oi = lax.dot_general(w["qs"], st.astype(BF16), nt, preferred_element_type=F32)
                oi = jnp.concatenate([oi[hd * C:(hd + 1) * C, :] for hd in range(GLA_HEADS)], axis=1)
                w["o_ref"][w["rows"], :] = w["o"] + oi
                st = st * w["dec"] + w["upd"]
            s_ref[...] = st

    n_groups = n_chunks // group
    if n_groups == 1:
        run_group(0)
    else:
        def body(g, carry):
            run_group(g)
            return carry
        lax.fori_loop(0, n_groups, body, 0)


def _gla(qg, kg, vg, laf, lab, B, L, tb):
    T = B * L
    J = L // tb
    fwd = lambda b, j: (b * J + j, 0)
    bwd = lambda b, j: (b * J + (J - 1 - j), 0)
    qk = lambda im: pl.BlockSpec((tb, GLA_QK), im)
    vv = lambda im: pl.BlockSpec((tb, GLA_WIDTH), im)
    return pl.pallas_call(
        functools.partial(_gla_kernel, n_chunks=tb // GLA_CHUNK, group=min(4, tb // GLA_CHUNK)),
        grid=(B, J),
        in_specs=[qk(fwd), qk(fwd), vv(fwd), qk(fwd), qk(bwd), qk(bwd), vv(bwd), qk(bwd)],
        out_specs=[vv(fwd), vv(bwd)],
        out_shape=[jax.ShapeDtypeStruct((T, GLA_WIDTH), F32)] * 2,
        scratch_shapes=[pltpu.VMEM((GLA_DV, GLA_QK), F32)] * 2,
        compiler_params=_params("arbitrary", "arbitrary"),
        name="gla",
    )(qg, kg, vg, laf, qg, kg, vg, lab)


def _bias_from_rel(rel, tab_ref, hd):
    n = jnp.abs(rel)
    half = REL_BUCKETS // 2
    neg = jnp.full(rel.shape, tab_ref[0, hd], F32)
    pos = jnp.full(rel.shape, tab_ref[half, hd], F32)
    for j, t in enumerate(BUCKET_STEPS):
        ge = n >= t
        neg = jnp.where(ge, tab_ref[j + 1, hd], neg)
        pos = jnp.where(ge, tab_ref[half + j + 1, hd], pos)
    return jnp.where(rel > 0, pos, neg)


KEY_CHUNK = 128
COL_TILE = 256
ROW_PAD = 128
CLS_LEFT, CLS_RIGHT, CLS_STRIP0, CLS_GENERIC = 0, 1, 2, 99
STEP_LEFT, STEP_RIGHT, STEP_MIXED = 0, 1, 2


def _strips_kernel(tab_ref, o_ref, *, tq, n_strips):
    hd = pl.program_id(0)
    kk = lax.broadcasted_iota(jnp.int32, (KEY_CHUNK, tq), 0)
    qq = lax.broadcasted_iota(jnp.int32, (KEY_CHUNK, tq), 1)
    o_ref[0, CLS_LEFT] = jnp.full((KEY_CHUNK, tq), tab_ref[REL_BUCKETS // 2 - 1, hd], F32)
    o_ref[0, CLS_RIGHT] = jnp.full((KEY_CHUNK, tq), tab_ref[REL_BUCKETS - 1, hd], F32)
    for j in range(n_strips):
        o_ref[0, CLS_STRIP0 + j] = _bias_from_rel(KEY_CHUNK * (j - 1) + kk - qq, tab_ref, hd)


def _strips(table, tq, n_strips):
    H = table.shape[1]
    return pl.pallas_call(
        functools.partial(_strips_kernel, tq=tq, n_strips=n_strips),
        grid_spec=pltpu.PrefetchScalarGridSpec(
            num_scalar_prefetch=1, grid=(H,), in_specs=[],
            out_specs=pl.BlockSpec((1, CLS_STRIP0 + n_strips, KEY_CHUNK, tq), lambda h, *_: (h, 0, 0, 0))),
        out_shape=jax.ShapeDtypeStruct((H, CLS_STRIP0 + n_strips, KEY_CHUNK, tq), F32),
        compiler_params=_params("arbitrary"),
        name="bias_strips",
    )(table)


def _diff_finish(acc, l, lq1_ref, lk1_ref, lq2_ref, lk2_ref, nw_ref, o_ref, tq):
    lam = (jnp.exp(jnp.sum(lq1_ref[...] * lk1_ref[...], axis=1, keepdims=True))
           - jnp.exp(jnp.sum(lq2_ref[...] * lk2_ref[...], axis=1, keepdims=True)) + LAMBDA_INIT)
    o = acc[:, :tq] / l[:, :tq] - lam * (acc[:, tq:] / l[:, tq:])
    ms = jnp.mean(o * o, axis=0, keepdims=True)
    nw = jnp.concatenate([nw_ref[...]] * (tq // 128), axis=1)
    o = o * lax.rsqrt(ms + EPS) * nw * (1.0 - LAMBDA_INIT)
    o_ref[0] = o.T.astype(o_ref.dtype)


def _stack_maps(qT):
    row = lax.broadcasted_iota(jnp.int32, qT.shape, 0)
    zero = jnp.zeros_like(qT)
    return jnp.concatenate([jnp.where(row < DIFF_DH, qT, zero), jnp.where(row >= DIFF_DH, qT, zero)], axis=1)


def _diff_fast_kernel(chunkcls_ref, tab_ref,
                      qT_ref, k_ref, vT_ref, strip_ref, lq1_ref, lk1_ref, lq2_ref, lk2_ref, nw_ref,
                      o_ref, m_ref, acc_ref, sa_ref, sb_ref, smaxa_ref, smaxb_ref, *, tq, tk, n_kt):
    hd = pl.program_id(1)
    r = pl.program_id(0) * pl.num_programs(2) + pl.program_id(2)
    n_ct = 2 * tq // COL_TILE
    n_chunk = tk // KEY_CHUNK
    qstack = _stack_maps(qT_ref[0])
    bias_max = functools.reduce(jnp.maximum, [tab_ref[i, hd] for i in range(REL_BUCKETS)])
    m_ref[...] = jnp.full(m_ref.shape, -1e30, F32)
    acc_ref[...] = jnp.zeros(acc_ref.shape, F32)

    def scores_tile(kt, buf, j):
        s_ref, smax_ref = buf
        kblk = k_ref[0, pl.ds(pl.multiple_of(kt * tk, tk), tk), :]
        cols = slice(j * COL_TILE, (j + 1) * COL_TILE)
        s = _dot(kblk, qstack[:, cols])
        s_ref[:, cols] = s
        smax_ref[:, cols] = jnp.max(s, axis=0, keepdims=True)

    def chain(kt, buf, j):
        s_ref, smax_ref = buf
        cols = slice(j * COL_TILE, (j + 1) * COL_TILE)
        q0 = (j * COL_TILE) % tq
        m_old = m_ref[:, cols]
        m_new = jnp.maximum(m_old, smax_ref[:, cols] + bias_max)
        alpha = jnp.exp2(m_old - m_new)
        ps = []
        for c in range(n_chunk):
            rows = slice(c * KEY_CHUNK, (c + 1) * KEY_CHUNK)
            bias = strip_ref[0, chunkcls_ref[r, kt * n_chunk + c], :, q0:q0 + COL_TILE]
            ps.append(jnp.exp2((s_ref[rows, cols] + bias) - m_new).astype(BF16))
        vblk = vT_ref[0, 0, :, pl.ds(pl.multiple_of(kt * tk, tk), tk)]
        acc_ref[:, cols] = alpha * acc_ref[:, cols] + _dot(vblk, jnp.concatenate(ps, axis=0))
        m_ref[:, cols] = m_new

    def step(kt, cur, nxt):
        for j in range(n_ct):
            if nxt is not None:
                scores_tile(kt + 1, nxt, j)
            chain(kt, cur, j)

    buf_a = (sa_ref, smaxa_ref)
    buf_b = (sb_ref, smaxb_ref)
    for j in range(n_ct):
        scores_tile(0, buf_a, j)

    def pair(kk, carry):
        step(2 * kk, buf_a, buf_b)
        step(2 * kk + 1, buf_b, buf_a)
        return carry

    lax.fori_loop(0, n_kt // 2 - 1, pair, 0)
    step(n_kt - 2, buf_a, buf_b)
    step(n_kt - 1, buf_b, None)
    acc = acc_ref[...]
    _diff_finish(acc[:DIFF_DV], acc[DIFF_DV:DIFF_DV + 1], lq1_ref, lk1_ref, lq2_ref, lk2_ref, nw_ref, o_ref, tq)


def _diff_kernel(stepcls_ref, chunkcls_ref, tab_ref,
                 qT_ref, k_ref, vT_ref, strip_ref, posq_ref, posk_ref, lq1_ref, lk1_ref, lq2_ref, lk2_ref, nw_ref,
                 o_ref, m_ref, l_ref, acc_ref, sa_ref, sb_ref, smaxa_ref, smaxb_ref, bias_ref,
                 *, tq, tk, n_kt, n_strips):
    hd = pl.program_id(1)
    r = pl.program_id(0) * pl.num_programs(2) + pl.program_id(2)
    half = REL_BUCKETS // 2
    n_ct = 2 * tq // COL_TILE
    n_chunk = tk // KEY_CHUNK

    qstack = _stack_maps(qT_ref[0])

    m_ref[...] = jnp.full(m_ref.shape, -1e30, F32)
    l_ref[...] = jnp.zeros(l_ref.shape, F32)
    acc_ref[...] = jnp.zeros(acc_ref.shape, F32)

    bias_max = functools.reduce(jnp.maximum, [tab_ref[i, hd] for i in range(REL_BUCKETS)])

    def scores_tile(kt, buf, j):
        s_ref, smax_ref = buf
        kblk = k_ref[0, pl.ds(pl.multiple_of(kt * tk, tk), tk), :]
        cols = slice(j * COL_TILE, (j + 1) * COL_TILE)
        s = _dot(kblk, qstack[:, cols])
        s_ref[:, cols] = s
        smax_ref[:, cols] = jnp.max(s, axis=0, keepdims=True)

    def scores(kt, buf):
        for j in range(n_ct):
            scores_tile(kt, buf, j)

    def build_bias(kt):
        posq = posq_ref[0]
        for c in range(n_chunk):
            cc = chunkcls_ref[r, kt * n_chunk + c]
            rows = slice(c * KEY_CHUNK, (c + 1) * KEY_CHUNK)

            @pl.when(cc == CLS_LEFT)
            def _():
                bias_ref[rows, :] = jnp.full((KEY_CHUNK, tq), tab_ref[half - 1, hd], F32)

            @pl.when(cc == CLS_RIGHT)
            def _():
                bias_ref[rows, :] = jnp.full((KEY_CHUNK, tq), tab_ref[REL_BUCKETS - 1, hd], F32)

            @pl.when(jnp.logical_and(cc >= CLS_STRIP0, cc < CLS_STRIP0 + n_strips))
            def _():
                bias_ref[rows, :] = strip_ref[0, cc]

            @pl.when(cc == CLS_GENERIC)
            def _():
                k0 = pl.multiple_of(kt * tk + c * KEY_CHUNK, KEY_CHUNK)
                pk = posk_ref[0, pl.ds(k0, KEY_CHUNK), :]
                pk = jnp.concatenate([pk] * (tq // 128), axis=1)
                bias_ref[rows, :] = _bias_from_rel(pk - posq, tab_ref, hd)

    def softmax_pv(kt, cur, const, nxt):
        s_ref, smax_ref = cur
        vblk = vT_ref[0, 0, :DIFF_DV, pl.ds(pl.multiple_of(kt * tk, tk), tk)]
        for j in range(n_ct):
            if nxt is not None:
                scores_tile(kt + 1, nxt, j)
            cols = slice(j * COL_TILE, (j + 1) * COL_TILE)
            shift = bias_max if const is None else const
            m_old = m_ref[:, cols]
            m_new = jnp.maximum(m_old, smax_ref[:, cols] + shift)
            alpha = jnp.exp2(m_old - m_new)
            if const is None:
                q0 = (j * COL_TILE) % tq
                p = jnp.exp2((s_ref[:, cols] + bias_ref[:, q0:q0 + COL_TILE]) - m_new)
            else:
                p = jnp.exp2(s_ref[:, cols] - (m_new - shift))
            l_ref[:, cols] = alpha * l_ref[:, cols] + jnp.sum(p, axis=0, keepdims=True)
            acc_ref[:, cols] = alpha * acc_ref[:, cols] + _dot(vblk, p.astype(BF16))
            m_ref[:, cols] = m_new

    def step(kt, cur, nxt):
        cls = stepcls_ref[r, kt]

        @pl.when(cls == STEP_MIXED)
        def _():
            build_bias(kt)
            softmax_pv(kt, cur, None, nxt)

        @pl.when(cls != STEP_MIXED)
        def _():
            const = jnp.where(cls == STEP_LEFT, tab_ref[half - 1, hd], tab_ref[REL_BUCKETS - 1, hd])
            softmax_pv(kt, cur, const, nxt)

    buf_a = (sa_ref, smaxa_ref)
    buf_b = (sb_ref, smaxb_ref)
    scores(0, buf_a)

    def pair(kk, carry):
        step(2 * kk, buf_a, buf_b)
        step(2 * kk + 1, buf_b, buf_a)
        return carry

    lax.fori_loop(0, n_kt // 2 - 1, pair, 0)
    step(n_kt - 2, buf_a, buf_b)
    step(n_kt - 1, buf_b, None)

    _diff_finish(acc_ref[...], l_ref[...], lq1_ref, lk1_ref, lq2_ref, lk2_ref, nw_ref, o_ref, tq)


def _classify(positions, tq, tk, n_strips):
    B, L = positions.shape
    nq, nc = L // tq, L // KEY_CHUNK
    pq = positions.reshape(B, nq, tq)
    pk = positions.reshape(B, nc, KEY_CHUNK)
    qlo, qhi, q0 = pq.min(axis=2), pq.max(axis=2), pq[:, :, 0]
    klo, khi, k0 = pk.min(axis=2), pk.max(axis=2), pk[:, :, 0]
    q_run = jnp.all(pq == q0[:, :, None] + jnp.arange(tq, dtype=positions.dtype), axis=2)
    k_run = jnp.all(pk == k0[:, :, None] + jnp.arange(KEY_CHUNK, dtype=positions.dtype), axis=2)
    left = (qlo[:, :, None] - khi[:, None, :]) >= FAR_DIST
    right = (klo[:, None, :] - qhi[:, :, None]) >= FAR_DIST
    d = k0[:, None, :] - q0[:, :, None]
    j = d // KEY_CHUNK + 1
    strip_ok = (q_run[:, :, None] & k_run[:, None, :] & (d % KEY_CHUNK == 0) & (j >= 0) & (j < n_strips))
    cls = jnp.where(left, CLS_LEFT, jnp.where(right, CLS_RIGHT, jnp.where(strip_ok, CLS_STRIP0 + j, CLS_GENERIC)))
    cls = cls.astype(jnp.int32)
    per_step = cls.reshape(B, nq, L // tk, tk // KEY_CHUNK)
    step = jnp.where(jnp.all(per_step == CLS_LEFT, axis=3), STEP_LEFT,
                     jnp.where(jnp.all(per_step == CLS_RIGHT, axis=3), STEP_RIGHT, STEP_MIXED)).astype(jnp.int32)
    return step.reshape(B * nq, L // tk), cls.reshape(B * nq, nc)


def _diffattn(qT, kd, vT, positions, table, lq1, lk1, lq2, lk2, nw, tq, tk):
    B, _, L = qT.shape
    nq, nk = L // tq, L // tk
    n_strips = tq // KEY_CHUNK + 2
    n_cls = CLS_STRIP0 + n_strips
    table2 = table * LOG2E
    strips = _strips(table2, tq, n_strips)
    stepcls, chunkcls = _classify(positions, tq, tk, n_strips)
    nw_b = jnp.broadcast_to(nw.reshape(DIFF_DV, 1), (DIFF_DV, 128))
    vecs = [a.reshape(1, DIFF_DH) for a in (lq1, lk1, lq2, lk2)]
    full2 = lambda b, h, i, *_: (0, 0)
    q_spec = pl.BlockSpec((1, DIFF_DV, tq), lambda b, h, i, *_: (b, h, i))
    k_spec = pl.BlockSpec((1, L, DIFF_DV), lambda b, h, i, *_: (b, 0, h))
    v_spec = pl.BlockSpec((1, 1, DIFF_DV + ONES_ROWS, L), lambda b, h, i, *_: (b, h, 0, 0))
    strip_spec = pl.BlockSpec((1, n_cls, KEY_CHUNK, tq), lambda b, h, i, *_: (h, 0, 0, 0))
    tail_specs = [pl.BlockSpec((1, DIFF_DH), full2)] * 4 + [pl.BlockSpec((DIFF_DV, 128), full2)]
    out_spec = pl.BlockSpec((1, tq, DIFF_DV), lambda b, h, i, *_: (b, i, h))
    out_shape = jax.ShapeDtypeStruct((B, L, DIFF_WIDTH), BF16)
    s_buf = pltpu.VMEM((tk, 2 * tq), F32)
    s_pad = pltpu.VMEM((tk, 2 * tq + ROW_PAD), F32)
    row_buf = pltpu.VMEM((1, 2 * tq), F32)

    def fast(_):
        return pl.pallas_call(
            functools.partial(_diff_fast_kernel, tq=tq, tk=tk, n_kt=nk),
            grid_spec=pltpu.PrefetchScalarGridSpec(
                num_scalar_prefetch=2, grid=(B, DIFF_HEADS, nq),
                in_specs=[q_spec, k_spec, v_spec, strip_spec] + tail_specs,
                out_specs=out_spec,
                scratch_shapes=[row_buf, pltpu.VMEM((DIFF_DV + ONES_ROWS, 2 * tq), F32),
                                s_pad, s_pad, row_buf, row_buf]),
            out_shape=out_shape,
            compiler_params=_params("arbitrary", "arbitrary", "arbitrary"),
            name="diffattn",
        )(chunkcls, table2, qT, kd, vT, strips, *vecs, nw_b)

    def general(_):
        posq = positions.reshape(B, 1, L)
        posk = jnp.broadcast_to(positions[:, :, None], (B, L, 128))
        return pl.pallas_call(
            functools.partial(_diff_kernel, tq=tq, tk=tk, n_kt=nk, n_strips=n_strips),
            grid_spec=pltpu.PrefetchScalarGridSpec(
                num_scalar_prefetch=3, grid=(B, DIFF_HEADS, nq),
                in_specs=[q_spec, k_spec, v_spec, strip_spec,
                          pl.BlockSpec((1, 1, tq), lambda b, h, i, *_: (b, 0, i)),
                          pl.BlockSpec((1, L, 128), lambda b, h, i, *_: (b, 0, 0))] + tail_specs,
                out_specs=out_spec,
                scratch_shapes=[row_buf, row_buf, pltpu.VMEM((DIFF_DV, 2 * tq), F32), s_buf, s_buf,
                                row_buf, row_buf, pltpu.VMEM((tk, tq), F32)]),
            out_shape=out_shape,
            compiler_params=_params("arbitrary", "arbitrary", "arbitrary"),
            name="diffattn_general",
        )(stepcls, chunkcls, table2, qT, kd, vT, strips, posq, posk, *vecs, nw_b)

    return lax.cond(jnp.all(chunkcls != CLS_GENERIC), fast, general, 0)


def _outproj_kernel(of_ref, ob_ref, rg_ref, gnw_ref, od_ref, wa_ref, wb_ref, ga_ref, x_ref, o_ref):
    o = of_ref[...] + ob_ref[...]
    parts = []
    for hd in range(GLA_HEADS):
        oh = o[:, hd * GLA_DV:(hd + 1) * GLA_DV]
        ms = jnp.mean(oh * oh, axis=-1, keepdims=True)
        parts.append(oh * lax.rsqrt(ms + EPS))
    r = rg_ref[...]
    oa = jnp.concatenate(parts, axis=1) * gnw_ref[...] * (r * _sigmoid(r))
    mixed = _dot(oa.astype(BF16), wa_ref[...]) + _dot(od_ref[...], wb_ref[...])
    o_ref[...] = x_ref[...] + ga_ref[0] * mixed


def _outproj(of, ob, rg, gnw, od, wa, wb, ga, x2d, L, tm):
    T, D = x2d.shape
    per_b = L // tm
    row = lambda i: (i, 0)
    full = lambda i: (0, 0)
    bvec = lambda i: (i // per_b, 0, 0)
    return pl.pallas_call(
        _outproj_kernel,
        grid=(T // tm,),
        in_specs=[pl.BlockSpec((tm, GLA_WIDTH), row), pl.BlockSpec((tm, GLA_WIDTH), row),
                  pl.BlockSpec((tm, GLA_WIDTH), row), pl.BlockSpec((1, GLA_WIDTH), full),
                  pl.BlockSpec((tm, DIFF_WIDTH), row),
                  pl.BlockSpec(wa.shape, full), pl.BlockSpec(wb.shape, full),
                  pl.BlockSpec((1, 1, D), bvec), pl.BlockSpec((tm, D), row)],
        out_specs=pl.BlockSpec((tm, D), row),
        out_shape=jax.ShapeDtypeStruct((T, D), F32),
        compiler_params=_params("arbitrary"),
        name="outproj",
    )(of, ob, rg, gnw, od, wa, wb, ga, x2d)


HALO = 8
FF_CHUNK = 256


def _ffn_kernel(x_ref, xp_ref, xn_ref, sh_ref, sc_ref, gf_ref, nw_ref, wup_ref, cw_ref, cb_ref, wdn_ref, fw_ref,
                o_ref, acc_ref, *, tm, per_b):
    i = pl.program_id(0)
    first = (i % per_b) == 0
    last = (i % per_b) == per_b - 1
    nw, sh, sc = nw_ref[...], sh_ref[0], sc_ref[0]
    x = x_ref[...]
    h_mid = _rms_mod(x, nw, sh, sc)
    h_prev = jnp.where(first, 0.0, _rms_mod(xp_ref[...], nw, sh, sc))
    h_next = jnp.where(last, 0.0, _rms_mod(xn_ref[...], nw, sh, sc))
    h = jnp.concatenate([h_prev, h_mid, h_next], axis=0).astype(BF16)
    rows = tm + 2 * HALO

    def conv(u, col):
        w = cw_ref[:, col:col + FF_CHUNK]
        up = pltpu.roll(u, 1, 0)[HALO:HALO + tm]
        un = pltpu.roll(u, rows - 1, 0)[HALO:HALO + tm]
        return (w[0:1] * up + w[1:2] * u[HALO:HALO + tm] + w[2:3] * un) + cb_ref[:, col:col + FF_CHUNK]

    def up(n):
        cg, cv = n * FF_CHUNK, D_FF + n * FF_CHUNK
        return _dot(h, wup_ref[:, cg:cg + FF_CHUNK]), _dot(h, wup_ref[:, cv:cv + FF_CHUNK])

    n_chunks = D_FF // FF_CHUNK
    u_gate, u_val = up(0)
    for n in range(n_chunks):
        cg, cv = n * FF_CHUNK, D_FF + n * FF_CHUNK
        if n + 1 < n_chunks:
            nxt = up(n + 1)
        gate = conv(u_gate, cg)
        val = conv(u_val, cv)
        a = (gate * _sigmoid(gate) * val).astype(BF16)
        part = _dot(a, wdn_ref[cg:cg + FF_CHUNK, :])
        if n == 0:
            acc_ref[...] = part
        else:
            acc_ref[...] += part
        if n + 1 < n_chunks:
            u_gate, u_val = nxt
    y = x + gf_ref[0] * acc_ref[...]
    ms = jnp.mean(y * y, axis=-1, keepdims=True)
    o_ref[...] = y * lax.rsqrt(ms + EPS) * fw_ref[...]


def _ffn(x1, sh, sc, gf, nw, wup, cw, cb, wdn, fw, L, tm):
    T, D = x1.shape
    per_b = L // tm
    hb = tm // HALO
    n_halo = T // HALO
    row = lambda i: (i, 0)
    full = lambda i: (0, 0)
    bvec = lambda i: (i // per_b, 0, 0)
    return pl.pallas_call(
        functools.partial(_ffn_kernel, tm=tm, per_b=per_b),
        grid=(T // tm,),
        in_specs=[pl.BlockSpec((tm, D), row),
                  pl.BlockSpec((HALO, D), lambda i: (jnp.maximum(i * hb - 1, 0), 0)),
                  pl.BlockSpec((HALO, D), lambda i: (jnp.minimum((i + 1) * hb, n_halo - 1), 0)),
                  pl.BlockSpec((1, 1, D), bvec), pl.BlockSpec((1, 1, D), bvec), pl.BlockSpec((1, 1, D), bvec),
                  pl.BlockSpec((1, D), full),
                  pl.BlockSpec(wup.shape, full), pl.BlockSpec(cw.shape, full), pl.BlockSpec(cb.shape, full),
                  pl.BlockSpec(wdn.shape, full), pl.BlockSpec((1, D), full)],
        out_specs=pl.BlockSpec((tm, D), row),
        out_shape=jax.ShapeDtypeStruct((T, D), F32),
        scratch_shapes=[pltpu.VMEM((tm, D), F32)],
        compiler_params=_params("arbitrary"),
        name="ffn",
    )(x1, x1, x1, sh, sc, gf, nw, wup, cw, cb, wdn, fw)


def kernel(x, c, positions, w_ada, b_ada, attn_norm_w, w_in, gla_dec_w_fwd, gla_dec_b_fwd, gla_dec_w_bwd,
           gla_dec_b_bwd, gla_norm_w, diff_lambda_q1, diff_lambda_k1, diff_lambda_q2, diff_lambda_k2,
           diff_norm_w, rel_bias_table, w_out, ffn_norm_w, w_up, conv_w, conv_b, w_down, final_norm_w):
    B, L, D = x.shape
    T = B * L
    tm = min(512, L)
    x2d = x.reshape(T, D)

    mod = _ada(c, w_ada[0], b_ada[0])
    sh_a, sc_a, g_a, sh_f, sc_f, g_f = [m.reshape(B, 1, D) for m in jnp.split(mod, N_MOD, axis=-1)]

    o_lr = 2 * GLA_QK + 2 * GLA_WIDTH
    o_d = o_lr + 2 * GLA_RANK
    w = w_in[0]
    wg = w[:, :o_lr].astype(BF16)
    wlr = jnp.pad(w[:, o_lr:o_d], ((0, 0), (0, 128 - 2 * GLA_RANK))).astype(BF16)
    wd = w[:, o_d:].astype(BF16)
    decw = jnp.zeros((128, 2 * GLA_QK), F32)
    decw = decw.at[:GLA_RANK, :GLA_QK].set(gla_dec_w_fwd[0]).at[GLA_RANK:2 * GLA_RANK, GLA_QK:].set(gla_dec_w_bwd[0])
    decb = jnp.concatenate([gla_dec_b_fwd[0], gla_dec_b_bwd[0]]).reshape(1, 2 * GLA_QK)

    qg, kg, vg, rg, laf, lab, qd, kd, vd = _inproj(
        x2d, sh_a, sc_a, attn_norm_w[0].reshape(1, D), wg, wlr, wd, decw.astype(BF16), decb, L, tm)

    o_f, o_b = _gla(qg, kg, vg, laf, lab, B, L, min(512, L))

    o_d = _diffattn(qd, kd.reshape(B, L, DIFF_WIDTH), vd, positions, rel_bias_table,
                    diff_lambda_q1[0], diff_lambda_k1[0], diff_lambda_q2[0], diff_lambda_k2[0],
                    diff_norm_w[0], min(512, L), min(1024, L // 2))

    wo = w_out[0].astype(BF16)
    x1 = _outproj(o_f, o_b, rg, gla_norm_w[0].reshape(1, GLA_WIDTH), o_d.reshape(T, DIFF_WIDTH),
                  wo[:GLA_WIDTH], wo[GLA_WIDTH:], g_a, x2d, L, tm)

    out = _ffn(x1, sh_f, sc_f, g_f, ffn_norm_w[0].reshape(1, D), w_up[0].astype(BF16), conv_w[0],
               conv_b[0].reshape(1, 2 * D_FF), w_down[0].astype(BF16), final_norm_w.reshape(1, D), L, tm)
    return out.reshape(B, L, D)
```

```python
import functools
import math

import jax
import jax.numpy as jnp
from jax import lax
from jax.experimental import pallas as pl
from jax.experimental.pallas import tpu as pltpu

F32 = jnp.float32
BF16 = jnp.bfloat16

D_MODEL = 1024
GLA_HEADS = 4
GLA_DK = 64
GLA_DV = 128
GLA_RANK = 16
GLA_GATE_TEMP = 16.0
GLA_CHUNK = 64
GLA_QK = GLA_HEADS * GLA_DK
GLA_WIDTH = GLA_HEADS * GLA_DV
DIFF_HEADS = 4
DIFF_DH = 64
DIFF_DV = 2 * DIFF_DH
DIFF_WIDTH = DIFF_HEADS * DIFF_DV
REL_BUCKETS = 32
D_FF = 2816
N_MOD = 6
EPS = 1e-6
LAMBDA_INIT = 0.8 - 0.6 * math.exp(-0.3 * 0)
LOG2E = 1.4426950408889634
ONES_ROWS = 16

BUCKET_STEPS = (1, 2, 3, 4, 5, 6, 7, 8, 12, 16, 23, 32, 46, 64, 91)
FAR_DIST = BUCKET_STEPS[-1]

VMEM_LIMIT = 56 * 1024 * 1024


def _split_bf16(a):
    hi = a.astype(BF16)
    lo = (a - hi.astype(F32)).astype(BF16)
    return hi, lo


def _dot(a, b):
    return jnp.dot(a, b, preferred_element_type=F32)


def _sigmoid(v):
    return 1.0 / (1.0 + jnp.exp(-v))


def _params(*sem):
    return pltpu.CompilerParams(dimension_semantics=sem, vmem_limit_bytes=VMEM_LIMIT)


def _ada_kernel(c_ref, w_ref, b_ref, o_ref):
    c = c_ref[...]
    s_hi, s_lo = _split_bf16(c * _sigmoid(c))
    w_hi, w_lo = _split_bf16(w_ref[...])
    o_ref[...] = (_dot(s_hi, w_hi) + _dot(s_lo, w_hi) + _dot(s_hi, w_lo)) + b_ref[...]


def _ada(c, w, b):
    B, D = c.shape
    N = w.shape[1]
    tn = 1024
    return pl.pallas_call(
        _ada_kernel,
        grid=(N // tn,),
        in_specs=[pl.BlockSpec((B, D), lambda j: (0, 0)),
                  pl.BlockSpec((D, tn), lambda j: (0, j)),
                  pl.BlockSpec((1, tn), lambda j: (0, j))],
        out_specs=pl.BlockSpec((B, tn), lambda j: (0, j)),
        out_shape=jax.ShapeDtypeStruct((B, N), F32),
        compiler_params=_params("arbitrary"),
        name="ada",
    )(c, w, b.reshape(1, N))


def _rms_mod(x, nw, sh, sc):
    ms = jnp.mean(x * x, axis=-1, keepdims=True)
    return (x * lax.rsqrt(ms + EPS) * nw) * (1.0 + sc) + sh


def _inproj_kernel(x_ref, sh_ref, sc_ref, nw_ref, wg_ref, wlr_ref, wd_ref, decw_ref, decb_ref,
                   qg_ref, kg_ref, vg_ref, rg_ref, laf_ref, lab_ref, qd_ref, kd_ref, vd_ref):
    h = _rms_mod(x_ref[...], nw_ref[...], sh_ref[0], sc_ref[0]).astype(BF16)
    g = _dot(h, wg_ref[...])
    qg_ref[...] = g[:, :GLA_QK] * (GLA_DK ** -0.5)
    kg_ref[...] = g[:, GLA_QK:2 * GLA_QK]
    vg_ref[...] = g[:, 2 * GLA_QK:2 * GLA_QK + GLA_WIDTH].astype(BF16)
    rg_ref[...] = g[:, 2 * GLA_QK + GLA_WIDTH:]
    lr = _dot(h, wlr_ref[...])
    z = _dot(lr.astype(BF16), decw_ref[...]) + decb_ref[...]
    la = (jnp.minimum(z, 0.0) - jnp.log(1.0 + jnp.exp(-jnp.abs(z)))) * (1.0 / GLA_GATE_TEMP)
    laf_ref[...] = la[:, :GLA_QK]
    lab_ref[...] = la[:, GLA_QK:]
    d = _dot(h, wd_ref[...])
    qd_ref[0] = (d[:, :DIFF_WIDTH] * (DIFF_DH ** -0.5 * LOG2E)).T.astype(BF16)
    kd_ref[...] = d[:, DIFF_WIDTH:2 * DIFF_WIDTH].astype(BF16)
    vt = d[:, 2 * DIFF_WIDTH:].T.astype(BF16)
    for hd in range(DIFF_HEADS):
        vd_ref[0, hd, :DIFF_DV, :] = vt[hd * DIFF_DV:(hd + 1) * DIFF_DV, :]
    vd_ref[0, :, DIFF_DV:, :] = jnp.ones((DIFF_HEADS, ONES_ROWS, vt.shape[1]), BF16)


def _inproj(x2d, sh, sc, nw, wg, wlr, wd, decw, decb, L, tm):
    T, D = x2d.shape
    per_b = L // tm
    row = lambda i: (i, 0)
    full = lambda i: (0, 0)
    bvec = lambda i: (i // per_b, 0, 0)
    B = T // L
    outs = [(GLA_QK, F32), (GLA_QK, F32), (GLA_WIDTH, BF16), (GLA_WIDTH, F32), (GLA_QK, F32), (GLA_QK, F32)]
    col = lambda i: (i // per_b, 0, i % per_b)
    col4 = lambda i: (i // per_b, 0, 0, i % per_b)
    return pl.pallas_call(
        _inproj_kernel,
        grid=(T // tm,),
        in_specs=[pl.BlockSpec((tm, D), row),
                  pl.BlockSpec((1, 1, D), bvec), pl.BlockSpec((1, 1, D), bvec),
                  pl.BlockSpec((1, D), full),
                  pl.BlockSpec(wg.shape, full), pl.BlockSpec(wlr.shape, full), pl.BlockSpec(wd.shape, full),
                  pl.BlockSpec(decw.shape, full), pl.BlockSpec(decb.shape, full)],
        out_specs=[pl.BlockSpec((tm, n), row) for n, _ in outs] + [
            pl.BlockSpec((1, DIFF_WIDTH, tm), col), pl.BlockSpec((tm, DIFF_WIDTH), row),
            pl.BlockSpec((1, DIFF_HEADS, DIFF_DV + ONES_ROWS, tm), col4)],
        out_shape=[jax.ShapeDtypeStruct((T, n), dt) for n, dt in outs] + [
            jax.ShapeDtypeStruct((B, DIFF_WIDTH, L), BF16), jax.ShapeDtypeStruct((T, DIFF_WIDTH), BF16),
            jax.ShapeDtypeStruct((B, DIFF_HEADS, DIFF_DV + ONES_ROWS, L), BF16)],
        compiler_params=_params("arbitrary"),
        name="inproj",
    )(x2d, sh, sc, nw, wg, wlr, wd, decw, decb)


def _gla_kernel(qf_ref, kf_ref, vf_ref, laf_ref, qb_ref, kb_ref, vb_ref, lab_ref,
                of_ref, ob_ref, sf_ref, sb_ref, *, n_chunks, group):
    C = GLA_CHUNK

    @pl.when(pl.program_id(1) == 0)
    def _():
        sf_ref[...] = jnp.zeros_like(sf_ref)
        sb_ref[...] = jnp.zeros_like(sb_ref)

    r = lax.broadcasted_iota(jnp.int32, (C, C), 0)
    s = lax.broadcasted_iota(jnp.int32, (C, C), 1)
    tri_f = jnp.where(s <= r, 1.0, 0.0).astype(BF16)
    tri_b = jnp.where(s >= r, 1.0, 0.0).astype(BF16)
    r4 = lax.broadcasted_iota(jnp.int32, (GLA_HEADS * C, C), 0) % C
    s4 = lax.broadcasted_iota(jnp.int32, (GLA_HEADS * C, C), 1)
    mask_f = s4 <= r4
    mask_b = s4 > r4
    lane = lax.broadcasted_iota(jnp.int32, (C, GLA_QK), 1)
    head_masks = [(lane >= hd * GLA_DK) & (lane < (hd + 1) * GLA_DK) for hd in range(GLA_HEADS)]
    nt = (((1,), (1,)), ((), ()))

    def per_head(x):
        zero = jnp.zeros_like(x)
        return [jnp.where(mk, x, zero) for mk in head_masks]

    dirs = ((qf_ref, kf_ref, vf_ref, laf_ref, of_ref, sf_ref, tri_f, mask_f, C - 1, False),
            (qb_ref, kb_ref, vb_ref, lab_ref, ob_ref, sb_ref, tri_b, mask_b, 0, True))

    def run_group(g):
        work = []
        for (q_ref, k_ref, v_ref, la_ref, o_ref, s_ref, tri, att_mask, last_row, rev) in dirs:
            for i in range(group):
                c = g * group + i
                c = (n_chunks - 1 - c) if rev else c
                work.append(dict(rows=pl.ds(pl.multiple_of(c * C, C), C), q_ref=q_ref, k_ref=k_ref, v_ref=v_ref,
                                 la_ref=la_ref, o_ref=o_ref, s_ref=s_ref, tri=tri, att_mask=att_mask,
                                 last_row=last_row))
        for w in work:
            la_hi, la_lo = _split_bf16(w["la_ref"][w["rows"], :])
            w["b"] = _dot(w["tri"], la_hi) + _dot(w["tri"], la_lo)
        for w in work:
            b = w["b"]
            tot = b[w["last_row"]:w["last_row"] + 1, :]
            q = w["q_ref"][w["rows"], :]
            k = w["k_ref"][w["rows"], :]
            w["dec"] = jnp.exp(tot)
            w["qs"] = jnp.concatenate(per_head((q * jnp.exp(b)).astype(BF16)), axis=0)
            w["k_in"] = (k * jnp.exp(-b)).astype(BF16)
            w["k_st"] = jnp.concatenate(per_head((k * jnp.exp(tot - b)).astype(BF16)), axis=0)
            v = w["v_ref"][w["rows"], :]
            w["v"] = v
            v_rows = jnp.concatenate([v[:, hd * GLA_DV:(hd + 1) * GLA_DV] for hd in range(GLA_HEADS)], axis=0)
            w["vt"] = v_rows.astype(F32).T.astype(BF16)
        for w in work:
            att = lax.dot_general(w["qs"], w["k_in"], nt, preferred_element_type=F32)
            w["att"] = jnp.where(w["att_mask"], att, 0.0).astype(BF16)
        for w in work:
            w["o"] = jnp.concatenate(
                [_dot(w["att"][hd * C:(hd + 1) * C, :], w["v"][:, hd * GLA_DV:(hd + 1) * GLA_DV])
                 for hd in range(GLA_HEADS)], axis=1)
        for w in work:
            w["upd"] = _dot(w["vt"], w["k_st"])
        for d in range(2):
            s_ref = dirs[d][5]
            st = s_ref[...]
            for w in work[d * group:(d + 1) * group]:
                oi = lax.dot_general(w["qs"], st.astype(BF16), nt, preferred_element_type=F32)
                oi = jnp.concatenate([oi[hd * C:(hd + 1) * C, :] for hd in range(GLA_HEADS)], axis=1)
                w["o_ref"][w["rows"], :] = w["o"] + oi
                st = st * w["dec"] + w["upd"]
            s_ref[...] = st

    n_groups = n_chunks // group
    if n_groups == 1:
        run_group(0)
    else:
        def body(g, carry):
            run_group(g)
            return carry
        lax.fori_loop(0, n_groups, body, 0)


def _gla(qg, kg, vg, laf, lab, B, L, tb):
    T = B * L
    J = L // tb
    fwd = lambda b, j: (b * J + j, 0)
    bwd = lambda b, j: (b * J + (J - 1 - j), 0)
    qk = lambda im: pl.BlockSpec((tb, GLA_QK), im)
    vv = lambda im: pl.BlockSpec((tb, GLA_WIDTH), im)
    return pl.pallas_call(
        functools.partial(_gla_kernel, n_chunks=tb // GLA_CHUNK, group=min(4, tb // GLA_CHUNK)),
        grid=(B, J),
        in_specs=[qk(fwd), qk(fwd), vv(fwd), qk(fwd), qk(bwd), qk(bwd), vv(bwd), qk(bwd)],
        out_specs=[vv(fwd), vv(bwd)],
        out_shape=[jax.ShapeDtypeStruct((T, GLA_WIDTH), F32)] * 2,
        scratch_shapes=[pltpu.VMEM((GLA_DV, GLA_QK), F32)] * 2,
        compiler_params=_params("arbitrary", "arbitrary"),
        name="gla",
    )(qg, kg, vg, laf, qg, kg, vg, lab)


def _bias_from_rel(rel, tab_ref, hd):
    n = jnp.abs(rel)
    half = REL_BUCKETS // 2
    neg = jnp.full(rel.shape, tab_ref[0, hd], F32)
    pos = jnp.full(rel.shape, tab_ref[half, hd], F32)
    for j, t in enumerate(BUCKET_STEPS):
        ge = n >= t
        neg = jnp.where(ge, tab_ref[j + 1, hd], neg)
        pos = jnp.where(ge, tab_ref[half + j + 1, hd], pos)
    return jnp.where(rel > 0, pos, neg)


KEY_CHUNK = 128
COL_TILE = 256
CLS_LEFT, CLS_RIGHT, CLS_STRIP0, CLS_GENERIC = 0, 1, 2, 99
STEP_LEFT, STEP_RIGHT, STEP_MIXED = 0, 1, 2


def _strips_kernel(tab_ref, o_ref, *, tq, n_strips):
    hd = pl.program_id(0)
    kk = lax.broadcasted_iota(jnp.int32, (KEY_CHUNK, tq), 0)
    qq = lax.broadcasted_iota(jnp.int32, (KEY_CHUNK, tq), 1)
    o_ref[0, CLS_LEFT] = jnp.full((KEY_CHUNK, tq), tab_ref[REL_BUCKETS // 2 - 1, hd], F32)
    o_ref[0, CLS_RIGHT] = jnp.full((KEY_CHUNK, tq), tab_ref[REL_BUCKETS - 1, hd], F32)
    for j in range(n_strips):
        o_ref[0, CLS_STRIP0 + j] = _bias_from_rel(KEY_CHUNK * (j - 1) + kk - qq, tab_ref, hd)


def _strips(table, tq, n_strips):
    H = table.shape[1]
    return pl.pallas_call(
        functools.partial(_strips_kernel, tq=tq, n_strips=n_strips),
        grid_spec=pltpu.PrefetchScalarGridSpec(
            num_scalar_prefetch=1, grid=(H,), in_specs=[],
            out_specs=pl.BlockSpec((1, CLS_STRIP0 + n_strips, KEY_CHUNK, tq), lambda h, *_: (h, 0, 0, 0))),
        out_shape=jax.ShapeDtypeStruct((H, CLS_STRIP0 + n_strips, KEY_CHUNK, tq), F32),
        compiler_params=_params("arbitrary"),
        name="bias_strips",
    )(table)


def _diff_finish(acc, l, lq1_ref, lk1_ref, lq2_ref, lk2_ref, nw_ref, o_ref, tq):
    lam = (jnp.exp(jnp.sum(lq1_ref[...] * lk1_ref[...], axis=1, keepdims=True))
           - jnp.exp(jnp.sum(lq2_ref[...] * lk2_ref[...], axis=1, keepdims=True)) + LAMBDA_INIT)
    o = acc[:, :tq] / l[:, :tq] - lam * (acc[:, tq:] / l[:, tq:])
    ms = jnp.mean(o * o, axis=0, keepdims=True)
    nw = jnp.concatenate([nw_ref[...]] * (tq // 128), axis=1)
    o = o * lax.rsqrt(ms + EPS) * nw * (1.0 - LAMBDA_INIT)
    o_ref[0] = o.T.astype(o_ref.dtype)


def _stack_maps(qT):
    row = lax.broadcasted_iota(jnp.int32, qT.shape, 0)
    zero = jnp.zeros_like(qT)
    return jnp.concatenate([jnp.where(row < DIFF_DH, qT, zero), jnp.where(row >= DIFF_DH, qT, zero)], axis=1)


def _diff_fast_kernel(chunkcls_ref, tab_ref,
                      qT_ref, k_ref, vT_ref, strip_ref, lq1_ref, lk1_ref, lq2_ref, lk2_ref, nw_ref,
                      o_ref, m_ref, acc_ref, sa_ref, sb_ref, smaxa_ref, smaxb_ref, *, tq, tk, n_kt):
    hd = pl.program_id(1)
    r = pl.program_id(0) * pl.num_programs(2) + pl.program_id(2)
    n_ct = 2 * tq // COL_TILE
    n_chunk = tk // KEY_CHUNK
    qstack = _stack_maps(qT_ref[0])
    bias_max = functools.reduce(jnp.maximum, [tab_ref[i, hd] for i in range(REL_BUCKETS)])
    m_ref[...] = jnp.full(m_ref.shape, -1e30, F32)
    acc_ref[...] = jnp.zeros(acc_ref.shape, F32)

    def scores_tile(kt, buf, j):
        s_ref, smax_ref = buf
        kblk = k_ref[0, pl.ds(pl.multiple_of(kt * tk, tk), tk), :]
        cols = slice(j * COL_TILE, (j + 1) * COL_TILE)
        s = _dot(kblk, qstack[:, cols])
        s_ref[:, cols] = s
        smax_ref[:, cols] = jnp.max(s, axis=0, keepdims=True)

    def chain(kt, buf, j):
        s_ref, smax_ref = buf
        cols = slice(j * COL_TILE, (j + 1) * COL_TILE)
        q0 = (j * COL_TILE) % tq
        m_old = m_ref[:, cols]
        m_new = jnp.maximum(m_old, smax_ref[:, cols] + bias_max)
        alpha = jnp.exp2(m_old - m_new)
        ps = []
        for c in range(n_chunk):
            rows = slice(c * KEY_CHUNK, (c + 1) * KEY_CHUNK)
            bias = strip_ref[0, chunkcls_ref[r, kt * n_chunk + c], :, q0:q0 + COL_TILE]
            ps.append(jnp.exp2((s_ref[rows, cols] + bias) - m_new).astype(BF16))
        vblk = vT_ref[0, 0, :, pl.ds(pl.multiple_of(kt * tk, tk), tk)]
        acc_ref[:, cols] = alpha * acc_ref[:, cols] + _dot(vblk, jnp.concatenate(ps, axis=0))
        m_ref[:, cols] = m_new

    def step(kt, cur, nxt):
        for j in range(n_ct):
            if nxt is not None:
                scores_tile(kt + 1, nxt, j)
            chain(kt, cur, j)

    buf_a = (sa_ref, smaxa_ref)
    buf_b = (sb_ref, smaxb_ref)
    for j in range(n_ct):
        scores_tile(0, buf_a, j)

    def pair(kk, carry):
        step(2 * kk, buf_a, buf_b)
        step(2 * kk + 1, buf_b, buf_a)
        return carry

    lax.fori_loop(0, n_kt // 2 - 1, pair, 0)
    step(n_kt - 2, buf_a, buf_b)
    step(n_kt - 1, buf_b, None)
    acc = acc_ref[...]
    _diff_finish(acc[:DIFF_DV], acc[DIFF_DV:DIFF_DV + 1], lq1_ref, lk1_ref, lq2_ref, lk2_ref, nw_ref, o_ref, tq)


def _diff_kernel(stepcls_ref, chunkcls_ref, tab_ref,
                 qT_ref, k_ref, vT_ref, strip_ref, posq_ref, posk_ref, lq1_ref, lk1_ref, lq2_ref, lk2_ref, nw_ref,
                 o_ref, m_ref, l_ref, acc_ref, sa_ref, sb_ref, smaxa_ref, smaxb_ref, bias_ref,
                 *, tq, tk, n_kt, n_strips):
    hd = pl.program_id(1)
    r = pl.program_id(0) * pl.num_programs(2) + pl.program_id(2)
    half = REL_BUCKETS // 2
    n_ct = 2 * tq // COL_TILE
    n_chunk = tk // KEY_CHUNK

    qstack = _stack_maps(qT_ref[0])

    m_ref[...] = jnp.full(m_ref.shape, -1e30, F32)
    l_ref[...] = jnp.zeros(l_ref.shape, F32)
    acc_ref[...] = jnp.zeros(acc_ref.shape, F32)

    bias_max = functools.reduce(jnp.maximum, [tab_ref[i, hd] for i in range(REL_BUCKETS)])

    def scores_tile(kt, buf, j):
        s_ref, smax_ref = buf
        kblk = k_ref[0, pl.ds(pl.multiple_of(kt * tk, tk), tk), :]
        cols = slice(j * COL_TILE, (j + 1) * COL_TILE)
        s = _dot(kblk, qstack[:, cols])
        s_ref[:, cols] = s
        smax_ref[:, cols] = jnp.max(s, axis=0, keepdims=True)

    def scores(kt, buf):
        for j in range(n_ct):
            scores_tile(kt, buf, j)

    def build_bias(kt):
        posq = posq_ref[0]
        for c in range(n_chunk):
            cc = chunkcls_ref[r, kt * n_chunk + c]
            rows = slice(c * KEY_CHUNK, (c + 1) * KEY_CHUNK)

            @pl.when(cc == CLS_LEFT)
            def _():
                bias_ref[rows, :] = jnp.full((KEY_CHUNK, tq), tab_ref[half - 1, hd], F32)

            @pl.when(cc == CLS_RIGHT)
            def _():
                bias_ref[rows, :] = jnp.full((KEY_CHUNK, tq), tab_ref[REL_BUCKETS - 1, hd], F32)

            @pl.when(jnp.logical_and(cc >= CLS_STRIP0, cc < CLS_STRIP0 + n_strips))
            def _():
                bias_ref[rows, :] = strip_ref[0, cc]

            @pl.when(cc == CLS_GENERIC)
            def _():
                k0 = pl.multiple_of(kt * tk + c * KEY_CHUNK, KEY_CHUNK)
                pk = posk_ref[0, pl.ds(k0, KEY_CHUNK), :]
                pk = jnp.concatenate([pk] * (tq // 128), axis=1)
                bias_ref[rows, :] = _bias_from_rel(pk - posq, tab_ref, hd)

    def softmax_pv(kt, cur, const, nxt):
        s_ref, smax_ref = cur
        vblk = vT_ref[0, 0, :DIFF_DV, pl.ds(pl.multiple_of(kt * tk, tk), tk)]
        for j in range(n_ct):
            if nxt is not None:
                scores_tile(kt + 1, nxt, j)
            cols = slice(j * COL_TILE, (j + 1) * COL_TILE)
            shift = bias_max if const is None else const
            m_old = m_ref[:, cols]
            m_new = jnp.maximum(m_old, smax_ref[:, cols] + shift)
            alpha = jnp.exp2(m_old - m_new)
            if const is None:
                q0 = (j * COL_TILE) % tq
                p = jnp.exp2((s_ref[:, cols] + bias_ref[:, q0:q0 + COL_TILE]) - m_new)
            else:
                p = jnp.exp2(s_ref[:, cols] - (m_new - shift))
            l_ref[:, cols] = alpha * l_ref[:, cols] + jnp.sum(p, axis=0, keepdims=True)
            acc_ref[:, cols] = alpha * acc_ref[:, cols] + _dot(vblk, p.astype(BF16))
            m_ref[:, cols] = m_new

    def step(kt, cur, nxt):
        cls = stepcls_ref[r, kt]

        @pl.when(cls == STEP_MIXED)
        def _():
            build_bias(kt)
            softmax_pv(kt, cur, None, nxt)

        @pl.when(cls != STEP_MIXED)
        def _():
            const = jnp.where(cls == STEP_LEFT, tab_ref[half - 1, hd], tab_ref[REL_BUCKETS - 1, hd])
            softmax_pv(kt, cur, const, nxt)

    buf_a = (sa_ref, smaxa_ref)
    buf_b = (sb_ref, smaxb_ref)
    scores(0, buf_a)

    def pair(kk, carry):
        step(2 * kk, buf_a, buf_b)
        step(2 * kk + 1, buf_b, buf_a)
        return carry

    lax.fori_loop(0, n_kt // 2 - 1, pair, 0)
    step(n_kt - 2, buf_a, buf_b)
    step(n_kt - 1, buf_b, None)

    _diff_finish(acc_ref[...], l_ref[...], lq1_ref, lk1_ref, lq2_ref, lk2_ref, nw_ref, o_ref, tq)


def _classify(positions, tq, tk, n_strips):
    B, L = positions.shape
    nq, nc = L // tq, L // KEY_CHUNK
    pq = positions.reshape(B, nq, tq)
    pk = positions.reshape(B, nc, KEY_CHUNK)
    qlo, qhi, q0 = pq.min(axis=2), pq.max(axis=2), pq[:, :, 0]
    klo, khi, k0 = pk.min(axis=2), pk.max(axis=2), pk[:, :, 0]
    q_run = jnp.all(pq == q0[:, :, None] + jnp.arange(tq, dtype=positions.dtype), axis=2)
    k_run = jnp.all(pk == k0[:, :, None] + jnp.arange(KEY_CHUNK, dtype=positions.dtype), axis=2)
    left = (qlo[:, :, None] - khi[:, None, :]) >= FAR_DIST
    right = (klo[:, None, :] - qhi[:, :, None]) >= FAR_DIST
    d = k0[:, None, :] - q0[:, :, None]
    j = d // KEY_CHUNK + 1
    strip_ok = (q_run[:, :, None] & k_run[:, None, :] & (d % KEY_CHUNK == 0) & (j >= 0) & (j < n_strips))
    cls = jnp.where(left, CLS_LEFT, jnp.where(right, CLS_RIGHT, jnp.where(strip_ok, CLS_STRIP0 + j, CLS_GENERIC)))
    cls = cls.astype(jnp.int32)
    per_step = cls.reshape(B, nq, L // tk, tk // KEY_CHUNK)
    step = jnp.where(jnp.all(per_step == CLS_LEFT, axis=3), STEP_LEFT,
                     jnp.where(jnp.all(per_step == CLS_RIGHT, axis=3), STEP_RIGHT, STEP_MIXED)).astype(jnp.int32)
    return step.reshape(B * nq, L // tk), cls.reshape(B * nq, nc)


def _diffattn(qT, kd, vT, positions, table, lq1, lk1, lq2, lk2, nw, tq, tk):
    B, _, L = qT.shape
    nq, nk = L // tq, L // tk
    n_strips = tq // KEY_CHUNK + 2
    n_cls = CLS_STRIP0 + n_strips
    table2 = table * LOG2E
    strips = _strips(table2, tq, n_strips)
    stepcls, chunkcls = _classify(positions, tq, tk, n_strips)
    nw_b = jnp.broadcast_to(nw.reshape(DIFF_DV, 1), (DIFF_DV, 128))
    vecs = [a.reshape(1, DIFF_DH) for a in (lq1, lk1, lq2, lk2)]
    full2 = lambda b, h, i, *_: (0, 0)
    q_spec = pl.BlockSpec((1, DIFF_DV, tq), lambda b, h, i, *_: (b, h, i))
    k_spec = pl.BlockSpec((1, L, DIFF_DV), lambda b, h, i, *_: (b, 0, h))
    v_spec = pl.BlockSpec((1, 1, DIFF_DV + ONES_ROWS, L), lambda b, h, i, *_: (b, h, 0, 0))
    strip_spec = pl.BlockSpec((1, n_cls, KEY_CHUNK, tq), lambda b, h, i, *_: (h, 0, 0, 0))
    tail_specs = [pl.BlockSpec((1, DIFF_DH), full2)] * 4 + [pl.BlockSpec((DIFF_DV, 128), full2)]
    out_spec = pl.BlockSpec((1, tq, DIFF_DV), lambda b, h, i, *_: (b, i, h))
    out_shape = jax.ShapeDtypeStruct((B, L, DIFF_WIDTH), BF16)
    s_buf = pltpu.VMEM((tk, 2 * tq), F32)
    row_buf = pltpu.VMEM((1, 2 * tq), F32)

    def fast(_):
        return pl.pallas_call(
            functools.partial(_diff_fast_kernel, tq=tq, tk=tk, n_kt=nk),
            grid_spec=pltpu.PrefetchScalarGridSpec(
                num_scalar_prefetch=2, grid=(B, DIFF_HEADS, nq),
                in_specs=[q_spec, k_spec, v_spec, strip_spec] + tail_specs,
                out_specs=out_spec,
                scratch_shapes=[row_buf, pltpu.VMEM((DIFF_DV + ONES_ROWS, 2 * tq), F32),
                                s_buf, s_buf, row_buf, row_buf]),
            out_shape=out_shape,
            compiler_params=_params("arbitrary", "arbitrary", "arbitrary"),
            name="diffattn",
        )(chunkcls, table2, qT, kd, vT, strips, *vecs, nw_b)

    def general(_):
        posq = positions.reshape(B, 1, L)
        posk = jnp.broadcast_to(positions[:, :, None], (B, L, 128))
        return pl.pallas_call(
            functools.partial(_diff_kernel, tq=tq, tk=tk, n_kt=nk, n_strips=n_strips),
            grid_spec=pltpu.PrefetchScalarGridSpec(
                num_scalar_prefetch=3, grid=(B, DIFF_HEADS, nq),
                in_specs=[q_spec, k_spec, v_spec, strip_spec,
                          pl.BlockSpec((1, 1, tq), lambda b, h, i, *_: (b, 0, i)),
                          pl.BlockSpec((1, L, 128), lambda b, h, i, *_: (b, 0, 0))] + tail_specs,
                out_specs=out_spec,
                scratch_shapes=[row_buf, row_buf, pltpu.VMEM((DIFF_DV, 2 * tq), F32), s_buf, s_buf,
                                row_buf, row_buf, pltpu.VMEM((tk, tq), F32)]),
            out_shape=out_shape,
            compiler_params=_params("arbitrary", "arbitrary", "arbitrary"),
            name="diffattn_general",
        )(stepcls, chunkcls, table2, qT, kd, vT, strips, posq, posk, *vecs, nw_b)

    return lax.cond(jnp.all(chunkcls != CLS_GENERIC), fast, general, 0)


def _outproj_kernel(of_ref, ob_ref, rg_ref, gnw_ref, od_ref, wa_ref, wb_ref, ga_ref, x_ref, o_ref):
    o = of_ref[...] + ob_ref[...]
    parts = []
    for hd in range(GLA_HEADS):
        oh = o[:, hd * GLA_DV:(hd + 1) * GLA_DV]
        ms = jnp.mean(oh * oh, axis=-1, keepdims=True)
        parts.append(oh * lax.rsqrt(ms + EPS))
    r = rg_ref[...]
    oa = jnp.concatenate(parts, axis=1) * gnw_ref[...] * (r * _sigmoid(r))
    mixed = _dot(oa.astype(BF16), wa_ref[...]) + _dot(od_ref[...], wb_ref[...])
    o_ref[...] = x_ref[...] + ga_ref[0] * mixed


def _outproj(of, ob, rg, gnw, od, wa, wb, ga, x2d, L, tm):
    T, D = x2d.shape
    per_b = L // tm
    row = lambda i: (i, 0)
    full = lambda i: (0, 0)
    bvec = lambda i: (i // per_b, 0, 0)
    return pl.pallas_call(
        _outproj_kernel,
        grid=(T // tm,),
        in_specs=[pl.BlockSpec((tm, GLA_WIDTH), row), pl.BlockSpec((tm, GLA_WIDTH), row),
                  pl.BlockSpec((tm, GLA_WIDTH), row), pl.BlockSpec((1, GLA_WIDTH), full),
                  pl.BlockSpec((tm, DIFF_WIDTH), row),
                  pl.BlockSpec(wa.shape, full), pl.BlockSpec(wb.shape, full),
                  pl.BlockSpec((1, 1, D), bvec), pl.BlockSpec((tm, D), row)],
        out_specs=pl.BlockSpec((tm, D), row),
        out_shape=jax.ShapeDtypeStruct((T, D), F32),
        compiler_params=_params("arbitrary"),
        name="outproj",
    )(of, ob, rg, gnw, od, wa, wb, ga, x2d)


HALO = 8
FF_CHUNK = 256


def _ffn_kernel(x_ref, xp_ref, xn_ref, sh_ref, sc_ref, gf_ref, nw_ref, wup_ref, cw_ref, cb_ref, wdn_ref, fw_ref,
                o_ref, acc_ref, *, tm, per_b):
    i = pl.program_id(0)
    first = (i % per_b) == 0
    last = (i % per_b) == per_b - 1
    nw, sh, sc = nw_ref[...], sh_ref[0], sc_ref[0]
    x = x_ref[...]
    h_mid = _rms_mod(x, nw, sh, sc)
    h_prev = jnp.where(first, 0.0, _rms_mod(xp_ref[...], nw, sh, sc))
    h_next = jnp.where(last, 0.0, _rms_mod(xn_ref[...], nw, sh, sc))
    h = jnp.concatenate([h_prev, h_mid, h_next], axis=0).astype(BF16)
    rows = tm + 2 * HALO

    def conv(u, col):
        w = cw_ref[:, col:col + FF_CHUNK]
        up = pltpu.roll(u, 1, 0)[HALO:HALO + tm]
        un = pltpu.roll(u, rows - 1, 0)[HALO:HALO + tm]
        return (w[0:1] * up + w[1:2] * u[HALO:HALO + tm] + w[2:3] * un) + cb_ref[:, col:col + FF_CHUNK]

    def up(n):
        cg, cv = n * FF_CHUNK, D_FF + n * FF_CHUNK
        return _dot(h, wup_ref[:, cg:cg + FF_CHUNK]), _dot(h, wup_ref[:, cv:cv + FF_CHUNK])

    n_chunks = D_FF // FF_CHUNK
    u_gate, u_val = up(0)
    for n in range(n_chunks):
        cg, cv = n * FF_CHUNK, D_FF + n * FF_CHUNK
        if n + 1 < n_chunks:
            nxt = up(n + 1)
        gate = conv(u_gate, cg)
        val = conv(u_val, cv)
        a = (gate * _sigmoid(gate) * val).astype(BF16)
        part = _dot(a, wdn_ref[cg:cg + FF_CHUNK, :])
        if n == 0:
            acc_ref[...] = part
        else:
            acc_ref[...] += part
        if n + 1 < n_chunks:
            u_gate, u_val = nxt
    y = x + gf_ref[0] * acc_ref[...]
    ms = jnp.mean(y * y, axis=-1, keepdims=True)
    o_ref[...] = y * lax.rsqrt(ms + EPS) * fw_ref[...]


def _ffn(x1, sh, sc, gf, nw, wup, cw, cb, wdn, fw, L, tm):
    T, D = x1.shape
    per_b = L // tm
    hb = tm // HALO
    n_halo = T // HALO
    row = lambda i: (i, 0)
    full = lambda i: (0, 0)
    bvec = lambda i: (i // per_b, 0, 0)
    return pl.pallas_call(
        functools.partial(_ffn_kernel, tm=tm, per_b=per_b),
        grid=(T // tm,),
        in_specs=[pl.BlockSpec((tm, D), row),
                  pl.BlockSpec((HALO, D), lambda i: (jnp.maximum(i * hb - 1, 0), 0)),
                  pl.BlockSpec((HALO, D), lambda i: (jnp.minimum((i + 1) * hb, n_halo - 1), 0)),
                  pl.BlockSpec((1, 1, D), bvec), pl.BlockSpec((1, 1, D), bvec), pl.BlockSpec((1, 1, D), bvec),
                  pl.BlockSpec((1, D), full),
                  pl.BlockSpec(wup.shape, full), pl.BlockSpec(cw.shape, full), pl.BlockSpec(cb.shape, full),
                  pl.BlockSpec(wdn.shape, full), pl.BlockSpec((1, D), full)],
        out_specs=pl.BlockSpec((tm, D), row),
        out_shape=jax.ShapeDtypeStruct((T, D), F32),
        scratch_shapes=[pltpu.VMEM((tm, D), F32)],
        compiler_params=_params("arbitrary"),
        name="ffn",
    )(x1, x1, x1, sh, sc, gf, nw, wup, cw, cb, wdn, fw)


def kernel(x, c, positions, w_ada, b_ada, attn_norm_w, w_in, gla_dec_w_fwd, gla_dec_b_fwd, gla_dec_w_bwd,
           gla_dec_b_bwd, gla_norm_w, diff_lambda_q1, diff_lambda_k1, diff_lambda_q2, diff_lambda_k2,
           diff_norm_w, rel_bias_table, w_out, ffn_norm_w, w_up, conv_w, conv_b, w_down, final_norm_w):
    B, L, D = x.shape
    T = B * L
    tm = min(512, L)
    x2d = x.reshape(T, D)

    mod = _ada(c, w_ada[0], b_ada[0])
    sh_a, sc_a, g_a, sh_f, sc_f, g_f = [m.reshape(B, 1, D) for m in jnp.split(mod, N_MOD, axis=-1)]

    o_lr = 2 * GLA_QK + 2 * GLA_WIDTH
    o_d = o_lr + 2 * GLA_RANK
    w = w_in[0]
    wg = w[:, :o_lr].astype(BF16)
    wlr = jnp.pad(w[:, o_lr:o_d], ((0, 0), (0, 128 - 2 * GLA_RANK))).astype(BF16)
    wd = w[:, o_d:].astype(BF16)
    decw = jnp.zeros((128, 2 * GLA_QK), F32)
    decw = decw.at[:GLA_RANK, :GLA_QK].set(gla_dec_w_fwd[0]).at[GLA_RANK:2 * GLA_RANK, GLA_QK:].set(gla_dec_w_bwd[0])
    decb = jnp.concatenate([gla_dec_b_fwd[0], gla_dec_b_bwd[0]]).reshape(1, 2 * GLA_QK)

    qg, kg, vg, rg, laf, lab, qd, kd, vd = _inproj(
        x2d, sh_a, sc_a, attn_norm_w[0].reshape(1, D), wg, wlr, wd, decw.astype(BF16), decb, L, tm)

    o_f, o_b = _gla(qg, kg, vg, laf, lab, B, L, min(512, L))

    o_d = _diffattn(qd, kd.reshape(B, L, DIFF_WIDTH), vd, positions, rel_bias_table,
                    diff_lambda_q1[0], diff_lambda_k1[0], diff_lambda_q2[0], diff_lambda_k2[0],
                    diff_norm_w[0], min(512, L), min(1024, L // 2))

    wo = w_out[0].astype(BF16)
    x1 = _outproj(o_f, o_b, rg, gla_norm_w[0].reshape(1, GLA_WIDTH), o_d.reshape(T, DIFF_WIDTH),
                  wo[:GLA_WIDTH], wo[GLA_WIDTH:], g_a, x2d, L, tm)

    out = _ffn(x1, sh_f, sc_f, g_f, ffn_norm_w[0].reshape(1, D), w_up[0].astype(BF16), conv_w[0],
               conv_b[0].reshape(1, 2 * D_FF), w_down[0].astype(BF16), final_norm_w.reshape(1, D), L, tm)
    return out.reshape(B, L, D)
```

```python
import functools
import math

import jax
import jax.numpy as jnp
from jax import lax
from jax.experimental import pallas as pl
from jax.experimental.pallas import tpu as pltpu

F32 = jnp.float32
BF16 = jnp.bfloat16

D_MODEL = 1024
GLA_HEADS = 4
GLA_DK = 64
GLA_DV = 128
GLA_RANK = 16
GLA_GATE_TEMP = 16.0
GLA_CHUNK = 64
GLA_QK = GLA_HEADS * GLA_DK
GLA_WIDTH = GLA_HEADS * GLA_DV
DIFF_HEADS = 4
DIFF_DH = 64
DIFF_DV = 2 * DIFF_DH
DIFF_WIDTH = DIFF_HEADS * DIFF_DV
REL_BUCKETS = 32
D_FF = 2816
N_MOD = 6
EPS = 1e-6
LAMBDA_INIT = 0.8 - 0.6 * math.exp(-0.3 * 0)
LOG2E = 1.4426950408889634
ONES_ROWS = 16

BUCKET_STEPS = (1, 2, 3, 4, 5, 6, 7, 8, 12, 16, 23, 32, 46, 64, 91)
FAR_DIST = BUCKET_STEPS[-1]

VMEM_LIMIT = 56 * 1024 * 1024


def _split_bf16(a):
    hi = a.astype(BF16)
    lo = (a - hi.astype(F32)).astype(BF16)
    return hi, lo


def _dot(a, b):
    return jnp.dot(a, b, preferred_element_type=F32)


def _sigmoid(v):
    return 1.0 / (1.0 + jnp.exp(-v))


def _params(*sem):
    return pltpu.CompilerParams(dimension_semantics=sem, vmem_limit_bytes=VMEM_LIMIT)


def _ada_kernel(c_ref, w_ref, b_ref, o_ref):
    c = c_ref[...]
    s_hi, s_lo = _split_bf16(c * _sigmoid(c))
    w_hi, w_lo = _split_bf16(w_ref[...])
    o_ref[...] = (_dot(s_hi, w_hi) + _dot(s_lo, w_hi) + _dot(s_hi, w_lo)) + b_ref[...]


def _ada(c, w, b):
    B, D = c.shape
    N = w.shape[1]
    tn = 1024
    return pl.pallas_call(
        _ada_kernel,
        grid=(N // tn,),
        in_specs=[pl.BlockSpec((B, D), lambda j: (0, 0)),
                  pl.BlockSpec((D, tn), lambda j: (0, j)),
                  pl.BlockSpec((1, tn), lambda j: (0, j))],
        out_specs=pl.BlockSpec((B, tn), lambda j: (0, j)),
        out_shape=jax.ShapeDtypeStruct((B, N), F32),
        compiler_params=_params("arbitrary"),
        name="ada",
    )(c, w, b.reshape(1, N))


def _rms_mod(x, nw, sh, sc):
    ms = jnp.mean(x * x, axis=-1, keepdims=True)
    return (x * lax.rsqrt(ms + EPS) * nw) * (1.0 + sc) + sh


def _inproj_kernel(x_ref, sh_ref, sc_ref, nw_ref, wg_ref, wlr_ref, wd_ref, decw_ref, decb_ref,
                   qg_ref, kg_ref, vg_ref, rg_ref, laf_ref, lab_ref, qd_ref, kd_ref, vd_ref):
    h = _rms_mod(x_ref[...], nw_ref[...], sh_ref[0], sc_ref[0]).astype(BF16)
    g = _dot(h, wg_ref[...])
    qg_ref[...] = g[:, :GLA_QK] * (GLA_DK ** -0.5)
    kg_ref[...] = g[:, GLA_QK:2 * GLA_QK]
    vg_ref[...] = g[:, 2 * GLA_QK:2 * GLA_QK + GLA_WIDTH].astype(BF16)
    rg_ref[...] = g[:, 2 * GLA_QK + GLA_WIDTH:]
    lr = _dot(h, wlr_ref[...])
    z = _dot(lr.astype(BF16), decw_ref[...]) + decb_ref[...]
    la = (jnp.minimum(z, 0.0) - jnp.log(1.0 + jnp.exp(-jnp.abs(z)))) * (1.0 / GLA_GATE_TEMP)
    laf_ref[...] = la[:, :GLA_QK]
    lab_ref[...] = la[:, GLA_QK:]
    d = _dot(h, wd_ref[...])
    qd_ref[0] = (d[:, :DIFF_WIDTH] * (DIFF_DH ** -0.5 * LOG2E)).T.astype(BF16)
    kd_ref[...] = d[:, DIFF_WIDTH:2 * DIFF_WIDTH].astype(BF16)
    vt = d[:, 2 * DIFF_WIDTH:].T.astype(BF16)
    for hd in range(DIFF_HEADS):
        vd_ref[0, hd, :DIFF_DV, :] = vt[hd * DIFF_DV:(hd + 1) * DIFF_DV, :]
    vd_ref[0, :, DIFF_DV:, :] = jnp.ones((DIFF_HEADS, ONES_ROWS, vt.shape[1]), BF16)


def _inproj(x2d, sh, sc, nw, wg, wlr, wd, decw, decb, L, tm):
    T, D = x2d.shape
    per_b = L // tm
    row = lambda i: (i, 0)
    full = lambda i: (0, 0)
    bvec = lambda i: (i // per_b, 0, 0)
    B = T // L
    outs = [(GLA_QK, F32), (GLA_QK, F32), (GLA_WIDTH, BF16), (GLA_WIDTH, F32), (GLA_QK, F32), (GLA_QK, F32)]
    col = lambda i: (i // per_b, 0, i % per_b)
    col4 = lambda i: (i // per_b, 0, 0, i % per_b)
    return pl.pallas_call(
        _inproj_kernel,
        grid=(T // tm,),
        in_specs=[pl.BlockSpec((tm, D), row),
                  pl.BlockSpec((1, 1, D), bvec), pl.BlockSpec((1, 1, D), bvec),
                  pl.BlockSpec((1, D), full),
                  pl.BlockSpec(wg.shape, full), pl.BlockSpec(wlr.shape, full), pl.BlockSpec(wd.shape, full),
                  pl.BlockSpec(decw.shape, full), pl.BlockSpec(decb.shape, full)],
        out_specs=[pl.BlockSpec((tm, n), row) for n, _ in outs] + [
            pl.BlockSpec((1, DIFF_WIDTH, tm), col), pl.BlockSpec((tm, DIFF_WIDTH), row),
            pl.BlockSpec((1, DIFF_HEADS, DIFF_DV + ONES_ROWS, tm), col4)],
        out_shape=[jax.ShapeDtypeStruct((T, n), dt) for n, dt in outs] + [
            jax.ShapeDtypeStruct((B, DIFF_WIDTH, L), BF16), jax.ShapeDtypeStruct((T, DIFF_WIDTH), BF16),
            jax.ShapeDtypeStruct((B, DIFF_HEADS, DIFF_DV + ONES_ROWS, L), BF16)],
        compiler_params=_params("arbitrary"),
        name="inproj",
    )(x2d, sh, sc, nw, wg, wlr, wd, decw, decb)


def _gla_kernel(qf_ref, kf_ref, vf_ref, laf_ref, qb_ref, kb_ref, vb_ref, lab_ref,
                of_ref, ob_ref, sf_ref, sb_ref, *, n_chunks, group):
    C = GLA_CHUNK

    @pl.when(pl.program_id(1) == 0)
    def _():
        sf_ref[...] = jnp.zeros_like(sf_ref)
        sb_ref[...] = jnp.zeros_like(sb_ref)

    r = lax.broadcasted_iota(jnp.int32, (C, C), 0)
    s = lax.broadcasted_iota(jnp.int32, (C, C), 1)
    tri_f = jnp.where(s <= r, 1.0, 0.0).astype(BF16)
    tri_b = jnp.where(s >= r, 1.0, 0.0).astype(BF16)
    r4 = lax.broadcasted_iota(jnp.int32, (GLA_HEADS * C, C), 0) % C
    s4 = lax.broadcasted_iota(jnp.int32, (GLA_HEADS * C, C), 1)
    mask_f = s4 <= r4
    mask_b = s4 > r4
    lane = lax.broadcasted_iota(jnp.int32, (C, GLA_QK), 1)
    head_masks = [(lane >= hd * GLA_DK) & (lane < (hd + 1) * GLA_DK) for hd in range(GLA_HEADS)]
    nt = (((1,), (1,)), ((), ()))

    def per_head(x):
        zero = jnp.zeros_like(x)
        return [jnp.where(mk, x, zero) for mk in head_masks]

    dirs = ((qf_ref, kf_ref, vf_ref, laf_ref, of_ref, sf_ref, tri_f, mask_f, C - 1, False),
            (qb_ref, kb_ref, vb_ref, lab_ref, ob_ref, sb_ref, tri_b, mask_b, 0, True))

    def run_group(g):
        work = []
        for (q_ref, k_ref, v_ref, la_ref, o_ref, s_ref, tri, att_mask, last_row, rev) in dirs:
            for i in range(group):
                c = g * group + i
                c = (n_chunks - 1 - c) if rev else c
                work.append(dict(rows=pl.ds(pl.multiple_of(c * C, C), C), q_ref=q_ref, k_ref=k_ref, v_ref=v_ref,
                                 la_ref=la_ref, o_ref=o_ref, s_ref=s_ref, tri=tri, att_mask=att_mask,
                                 last_row=last_row))
        for w in work:
            la_hi, la_lo = _split_bf16(w["la_ref"][w["rows"], :])
            w["b"] = _dot(w["tri"], la_hi) + _dot(w["tri"], la_lo)
        for w in work:
            b = w["b"]
            tot = b[w["last_row"]:w["last_row"] + 1, :]
            q = w["q_ref"][w["rows"], :]
            k = w["k_ref"][w["rows"], :]
            w["dec"] = jnp.exp(tot)
            w["qs"] = jnp.concatenate(per_head((q * jnp.exp(b)).astype(BF16)), axis=0)
            w["k_in"] = (k * jnp.exp(-b)).astype(BF16)
            w["k_st"] = jnp.concatenate(per_head((k * jnp.exp(tot - b)).astype(BF16)), axis=0)
            v = w["v_ref"][w["rows"], :]
            w["v"] = v
            v_rows = jnp.concatenate([v[:, hd * GLA_DV:(hd + 1) * GLA_DV] for hd in range(GLA_HEADS)], axis=0)
            w["vt"] = v_rows.astype(F32).T.astype(BF16)
        for w in work:
            att = lax.dot_general(w["qs"], w["k_in"], nt, preferred_element_type=F32)
            w["att"] = jnp.where(w["att_mask"], att, 0.0).astype(BF16)
        for w in work:
            w["o"] = jnp.concatenate(
                [_dot(w["att"][hd * C:(hd + 1) * C, :], w["v"][:, hd * GLA_DV:(hd + 1) * GLA_DV])
                 for hd in range(GLA_HEADS)], axis=1)
        for w in work:
            w["upd"] = _dot(w["vt"], w["k_st"])
        for d in range(2):
            s_ref = dirs[d][5]
            st = s_ref[...]
            for w in work[d * group:(d + 1) * group]:
                oi = lax.dot_general(w["qs"], st.astype(BF16), nt, preferred_element_type=F32)
                oi = jnp.concatenate([oi[hd * C:(hd + 1) * C, :] for hd in range(GLA_HEADS)], axis=1)
                w["o_ref"][w["rows"], :] = w["o"] + oi
                st = st * w["dec"] + w["upd"]
            s_ref[...] = st

    n_groups = n_chunks // group
    if n_groups == 1:
        run_group(0)
    else:
        def body(g, carry):
            run_group(g)
            return carry
        lax.fori_loop(0, n_groups, body, 0)


def _gla(qg, kg, vg, laf, lab, B, L, tb):
    T = B * L
    J = L // tb
    fwd = lambda b, j: (b * J + j, 0)
    bwd = lambda b, j: (b * J + (J - 1 - j), 0)
    qk = lambda im: pl.BlockSpec((tb, GLA_QK), im)
    vv = lambda im: pl.BlockSpec((tb, GLA_WIDTH), im)
    return pl.pallas_call(
        functools.partial(_gla_kernel, n_chunks=tb // GLA_CHUNK, group=min(4, tb // GLA_CHUNK)),
        grid=(B, J),
        in_specs=[qk(fwd), qk(fwd), vv(fwd), qk(fwd), qk(bwd), qk(bwd), vv(bwd), qk(bwd)],
        out_specs=[vv(fwd), vv(bwd)],
        out_shape=[jax.ShapeDtypeStruct((T, GLA_WIDTH), F32)] * 2,
        scratch_shapes=[pltpu.VMEM((GLA_DV, GLA_QK), F32)] * 2,
        compiler_params=_params("arbitrary", "arbitrary"),
        name="gla",
    )(qg, kg, vg, laf, qg, kg, vg, lab)


def _bias_from_rel(rel, tab_ref, hd):
    n = jnp.abs(rel)
    half = REL_BUCKETS // 2
    neg = jnp.full(rel.shape, tab_ref[0, hd], F32)
    pos = jnp.full(rel.shape, tab_ref[half, hd], F32)
    for j, t in enumerate(BUCKET_STEPS):
        ge = n >= t
        neg = jnp.where(ge, tab_ref[j + 1, hd], neg)
        pos = jnp.where(ge, tab_ref[half + j + 1, hd], pos)
    return jnp.where(rel > 0, pos, neg)


KEY_CHUNK = 128
COL_TILE = 256
CLS_LEFT, CLS_RIGHT, CLS_STRIP0, CLS_GENERIC = 0, 1, 2, 99
STEP_LEFT, STEP_RIGHT, STEP_MIXED = 0, 1, 2


def _strips_kernel(tab_ref, o_ref, *, tq, n_strips):
    hd = pl.program_id(0)
    kk = lax.broadcasted_iota(jnp.int32, (KEY_CHUNK, tq), 0)
    qq = lax.broadcasted_iota(jnp.int32, (KEY_CHUNK, tq), 1)
    o_ref[0, CLS_LEFT] = jnp.full((KEY_CHUNK, tq), tab_ref[REL_BUCKETS // 2 - 1, hd], F32)
    o_ref[0, CLS_RIGHT] = jnp.full((KEY_CHUNK, tq), tab_ref[REL_BUCKETS - 1, hd], F32)
    for j in range(n_strips):
        o_ref[0, CLS_STRIP0 + j] = _bias_from_rel(KEY_CHUNK * (j - 1) + kk - qq, tab_ref, hd)


def _strips(table, tq, n_strips):
    H = table.shape[1]
    return pl.pallas_call(
        functools.partial(_strips_kernel, tq=tq, n_strips=n_strips),
        grid_spec=pltpu.PrefetchScalarGridSpec(
            num_scalar_prefetch=1, grid=(H,), in_specs=[],
            out_specs=pl.BlockSpec((1, CLS_STRIP0 + n_strips, KEY_CHUNK, tq), lambda h, *_: (h, 0, 0, 0))),
        out_shape=jax.ShapeDtypeStruct((H, CLS_STRIP0 + n_strips, KEY_CHUNK, tq), F32),
        compiler_params=_params("arbitrary"),
        name="bias_strips",
    )(table)


def _diff_finish(acc, l, lq1_ref, lk1_ref, lq2_ref, lk2_ref, nw_ref, o_ref, tq):
    lam = (jnp.exp(jnp.sum(lq1_ref[...] * lk1_ref[...], axis=1, keepdims=True))
           - jnp.exp(jnp.sum(lq2_ref[...] * lk2_ref[...], axis=1, keepdims=True)) + LAMBDA_INIT)
    o = acc[:, :tq] / l[:, :tq] - lam * (acc[:, tq:] / l[:, tq:])
    ms = jnp.mean(o * o, axis=0, keepdims=True)
    nw = jnp.concatenate([nw_ref[...]] * (tq // 128), axis=1)
    o = o * lax.rsqrt(ms + EPS) * nw * (1.0 - LAMBDA_INIT)
    o_ref[0] = o.T.astype(o_ref.dtype)


def _stack_maps(qT):
    row = lax.broadcasted_iota(jnp.int32, qT.shape, 0)
    zero = jnp.zeros_like(qT)
    return jnp.concatenate([jnp.where(row < DIFF_DH, qT, zero), jnp.where(row >= DIFF_DH, qT, zero)], axis=1)


def _diff_fast_kernel(chunkcls_ref, tab_ref,
                      qT_ref, k_ref, vT_ref, strip_ref, lq1_ref, lk1_ref, lq2_ref, lk2_ref, nw_ref,
                      o_ref, m_ref, acc_ref, sa_ref, sb_ref, smaxa_ref, smaxb_ref, *, tq, tk, n_kt):
    hd = pl.program_id(1)
    r = pl.program_id(0) * pl.num_programs(2) + pl.program_id(2)
    n_ct = 2 * tq // COL_TILE
    n_chunk = tk // KEY_CHUNK
    qstack = _stack_maps(qT_ref[0])
    bias_max = functools.reduce(jnp.maximum, [tab_ref[i, hd] for i in range(REL_BUCKETS)])
    m_ref[...] = jnp.full(m_ref.shape, -1e30, F32)
    acc_ref[...] = jnp.zeros(acc_ref.shape, F32)

    def scores_tile(kt, buf, j):
        s_ref, smax_ref = buf
        kblk = k_ref[0, pl.ds(pl.multiple_of(kt * tk, tk), tk), :]
        cols = slice(j * COL_TILE, (j + 1) * COL_TILE)
        s = _dot(kblk, qstack[:, cols])
        s_ref[j] = s
        smax_ref[:, cols] = jnp.max(s, axis=0, keepdims=True)

    def chain(kt, buf, j):
        s_ref, smax_ref = buf
        cols = slice(j * COL_TILE, (j + 1) * COL_TILE)
        q0 = (j * COL_TILE) % tq
        m_old = m_ref[:, cols]
        m_new = jnp.maximum(m_old, smax_ref[:, cols] + bias_max)
        alpha = jnp.exp2(m_old - m_new)
        ps = []
        for c in range(n_chunk):
            rows = slice(c * KEY_CHUNK, (c + 1) * KEY_CHUNK)
            bias = strip_ref[0, chunkcls_ref[r, kt * n_chunk + c], :, q0:q0 + COL_TILE]
            ps.append(jnp.exp2((s_ref[j, rows, :] + bias) - m_new).astype(BF16))
        vblk = vT_ref[0, 0, :, pl.ds(pl.multiple_of(kt * tk, tk), tk)]
        acc_ref[:, cols] = alpha * acc_ref[:, cols] + _dot(vblk, jnp.concatenate(ps, axis=0))
        m_ref[:, cols] = m_new

    def step(kt, cur, nxt):
        for j in range(n_ct):
            if nxt is not None:
                scores_tile(kt + 1, nxt, j)
            chain(kt, cur, j)

    buf_a = (sa_ref, smaxa_ref)
    buf_b = (sb_ref, smaxb_ref)
    for j in range(n_ct):
        scores_tile(0, buf_a, j)

    def pair(kk, carry):
        step(2 * kk, buf_a, buf_b)
        step(2 * kk + 1, buf_b, buf_a)
        return carry

    lax.fori_loop(0, n_kt // 2 - 1, pair, 0)
    step(n_kt - 2, buf_a, buf_b)
    step(n_kt - 1, buf_b, None)
    acc = acc_ref[...]
    _diff_finish(acc[:DIFF_DV], acc[DIFF_DV:DIFF_DV + 1], lq1_ref, lk1_ref, lq2_ref, lk2_ref, nw_ref, o_ref, tq)


def _diff_kernel(stepcls_ref, chunkcls_ref, tab_ref,
                 qT_ref, k_ref, vT_ref, strip_ref, posq_ref, posk_ref, lq1_ref, lk1_ref, lq2_ref, lk2_ref, nw_ref,
                 o_ref, m_ref, l_ref, acc_ref, sa_ref, sb_ref, smaxa_ref, smaxb_ref, bias_ref,
                 *, tq, tk, n_kt, n_strips):
    hd = pl.program_id(1)
    r = pl.program_id(0) * pl.num_programs(2) + pl.program_id(2)
    half = REL_BUCKETS // 2
    n_ct = 2 * tq // COL_TILE
    n_chunk = tk // KEY_CHUNK

    qstack = _stack_maps(qT_ref[0])

    m_ref[...] = jnp.full(m_ref.shape, -1e30, F32)
    l_ref[...] = jnp.zeros(l_ref.shape, F32)
    acc_ref[...] = jnp.zeros(acc_ref.shape, F32)

    bias_max = functools.reduce(jnp.maximum, [tab_ref[i, hd] for i in range(REL_BUCKETS)])

    def scores_tile(kt, buf, j):
        s_ref, smax_ref = buf
        kblk = k_ref[0, pl.ds(pl.multiple_of(kt * tk, tk), tk), :]
        cols = slice(j * COL_TILE, (j + 1) * COL_TILE)
        s = _dot(kblk, qstack[:, cols])
        s_ref[:, cols] = s
        smax_ref[:, cols] = jnp.max(s, axis=0, keepdims=True)

    def scores(kt, buf):
        for j in range(n_ct):
            scores_tile(kt, buf, j)

    def build_bias(kt):
        posq = posq_ref[0]
        for c in range(n_chunk):
            cc = chunkcls_ref[r, kt * n_chunk + c]
            rows = slice(c * KEY_CHUNK, (c + 1) * KEY_CHUNK)

            @pl.when(cc == CLS_LEFT)
            def _():
                bias_ref[rows, :] = jnp.full((KEY_CHUNK, tq), tab_ref[half - 1, hd], F32)

            @pl.when(cc == CLS_RIGHT)
            def _():
                bias_ref[rows, :] = jnp.full((KEY_CHUNK, tq), tab_ref[REL_BUCKETS - 1, hd], F32)

            @pl.when(jnp.logical_and(cc >= CLS_STRIP0, cc < CLS_STRIP0 + n_strips))
            def _():
                bias_ref[rows, :] = strip_ref[0, cc]

            @pl.when(cc == CLS_GENERIC)
            def _():
                k0 = pl.multiple_of(kt * tk + c * KEY_CHUNK, KEY_CHUNK)
                pk = posk_ref[0, pl.ds(k0, KEY_CHUNK), :]
                pk = jnp.concatenate([pk] * (tq // 128), axis=1)
                bias_ref[rows, :] = _bias_from_rel(pk - posq, tab_ref, hd)

    def softmax_pv(kt, cur, const, nxt):
        s_ref, smax_ref = cur
        vblk = vT_ref[0, 0, :DIFF_DV, pl.ds(pl.multiple_of(kt * tk, tk), tk)]
        for j in range(n_ct):
            if nxt is not None:
                scores_tile(kt + 1, nxt, j)
            cols = slice(j * COL_TILE, (j + 1) * COL_TILE)
            shift = bias_max if const is None else const
            m_old = m_ref[:, cols]
            m_new = jnp.maximum(m_old, smax_ref[:, cols] + shift)
            alpha = jnp.exp2(m_old - m_new)
            if const is None:
                q0 = (j * COL_TILE) % tq
                p = jnp.exp2((s_ref[:, cols] + bias_ref[:, q0:q0 + COL_TILE]) - m_new)
            else:
                p = jnp.exp2(s_ref[:, cols] - (m_new - shift))
            l_ref[:, cols] = alpha * l_ref[:, cols] + jnp.sum(p, axis=0, keepdims=True)
            acc_ref[:, cols] = alpha * acc_ref[:, cols] + _dot(vblk, p.astype(BF16))
            m_ref[:, cols] = m_new

    def step(kt, cur, nxt):
        cls = stepcls_ref[r, kt]

        @pl.when(cls == STEP_MIXED)
        def _():
            build_bias(kt)
            softmax_pv(kt, cur, None, nxt)

        @pl.when(cls != STEP_MIXED)
        def _():
            const = jnp.where(cls == STEP_LEFT, tab_ref[half - 1, hd], tab_ref[REL_BUCKETS - 1, hd])
            softmax_pv(kt, cur, const, nxt)

    buf_a = (sa_ref, smaxa_ref)
    buf_b = (sb_ref, smaxb_ref)
    scores(0, buf_a)

    def pair(kk, carry):
        step(2 * kk, buf_a, buf_b)
        step(2 * kk + 1, buf_b, buf_a)
        return carry

    lax.fori_loop(0, n_kt // 2 - 1, pair, 0)
    step(n_kt - 2, buf_a, buf_b)
    step(n_kt - 1, buf_b, None)

    _diff_finish(acc_ref[...], l_ref[...], lq1_ref, lk1_ref, lq2_ref, lk2_ref, nw_ref, o_ref, tq)


def _classify(positions, tq, tk, n_strips):
    B, L = positions.shape
    nq, nc = L // tq, L // KEY_CHUNK
    pq = positions.reshape(B, nq, tq)
    pk = positions.reshape(B, nc, KEY_CHUNK)
    qlo, qhi, q0 = pq.min(axis=2), pq.max(axis=2), pq[:, :, 0]
    klo, khi, k0 = pk.min(axis=2), pk.max(axis=2), pk[:, :, 0]
    q_run = jnp.all(pq == q0[:, :, None] + jnp.arange(tq, dtype=positions.dtype), axis=2)
    k_run = jnp.all(pk == k0[:, :, None] + jnp.arange(KEY_CHUNK, dtype=positions.dtype), axis=2)
    left = (qlo[:, :, None] - khi[:, None, :]) >= FAR_DIST
    right = (klo[:, None, :] - qhi[:, :, None]) >= FAR_DIST
    d = k0[:, None, :] - q0[:, :, None]
    j = d // KEY_CHUNK + 1
    strip_ok = (q_run[:, :, None] & k_run[:, None, :] & (d % KEY_CHUNK == 0) & (j >= 0) & (j < n_strips))
    cls = jnp.where(left, CLS_LEFT, jnp.where(right, CLS_RIGHT, jnp.where(strip_ok, CLS_STRIP0 + j, CLS_GENERIC)))
    cls = cls.astype(jnp.int32)
    per_step = cls.reshape(B, nq, L // tk, tk // KEY_CHUNK)
    step = jnp.where(jnp.all(per_step == CLS_LEFT, axis=3), STEP_LEFT,
                     jnp.where(jnp.all(per_step == CLS_RIGHT, axis=3), STEP_RIGHT, STEP_MIXED)).astype(jnp.int32)
    return step.reshape(B * nq, L // tk), cls.reshape(B * nq, nc)


def _diffattn(qT, kd, vT, positions, table, lq1, lk1, lq2, lk2, nw, tq, tk):
    B, _, L = qT.shape
    nq, nk = L // tq, L // tk
    n_strips = tq // KEY_CHUNK + 2
    n_cls = CLS_STRIP0 + n_strips
    table2 = table * LOG2E
    strips = _strips(table2, tq, n_strips)
    stepcls, chunkcls = _classify(positions, tq, tk, n_strips)
    nw_b = jnp.broadcast_to(nw.reshape(DIFF_DV, 1), (DIFF_DV, 128))
    vecs = [a.reshape(1, DIFF_DH) for a in (lq1, lk1, lq2, lk2)]
    full2 = lambda b, h, i, *_: (0, 0)
    q_spec = pl.BlockSpec((1, DIFF_DV, tq), lambda b, h, i, *_: (b, h, i))
    k_spec = pl.BlockSpec((1, L, DIFF_DV), lambda b, h, i, *_: (b, 0, h))
    v_spec = pl.BlockSpec((1, 1, DIFF_DV + ONES_ROWS, L), lambda b, h, i, *_: (b, h, 0, 0))
    strip_spec = pl.BlockSpec((1, n_cls, KEY_CHUNK, tq), lambda b, h, i, *_: (h, 0, 0, 0))
    tail_specs = [pl.BlockSpec((1, DIFF_DH), full2)] * 4 + [pl.BlockSpec((DIFF_DV, 128), full2)]
    out_spec = pl.BlockSpec((1, tq, DIFF_DV), lambda b, h, i, *_: (b, i, h))
    out_shape = jax.ShapeDtypeStruct((B, L, DIFF_WIDTH), BF16)
    s_buf = pltpu.VMEM((tk, 2 * tq), F32)
    s_tiles = pltpu.VMEM((2 * tq // COL_TILE, tk, COL_TILE), F32)
    row_buf = pltpu.VMEM((1, 2 * tq), F32)

    def fast(_):
        return pl.pallas_call(
            functools.partial(_diff_fast_kernel, tq=tq, tk=tk, n_kt=nk),
            grid_spec=pltpu.PrefetchScalarGridSpec(
                num_scalar_prefetch=2, grid=(B, DIFF_HEADS, nq),
                in_specs=[q_spec, k_spec, v_spec, strip_spec] + tail_specs,
                out_specs=out_spec,
                scratch_shapes=[row_buf, pltpu.VMEM((DIFF_DV + ONES_ROWS, 2 * tq), F32),
                                s_tiles, s_tiles, row_buf, row_buf]),
            out_shape=out_shape,
            compiler_params=_params("arbitrary", "arbitrary", "arbitrary"),
            name="diffattn",
        )(chunkcls, table2, qT, kd, vT, strips, *vecs, nw_b)

    def general(_):
        posq = positions.reshape(B, 1, L)
        posk = jnp.broadcast_to(positions[:, :, None], (B, L, 128))
        return pl.pallas_call(
            functools.partial(_diff_kernel, tq=tq, tk=tk, n_kt=nk, n_strips=n_strips),
            grid_spec=pltpu.PrefetchScalarGridSpec(
                num_scalar_prefetch=3, grid=(B, DIFF_HEADS, nq),
                in_specs=[q_spec, k_spec, v_spec, strip_spec,
                          pl.BlockSpec((1, 1, tq), lambda b, h, i, *_: (b, 0, i)),
                          pl.BlockSpec((1, L, 128), lambda b, h, i, *_: (b, 0, 0))] + tail_specs,
                out_specs=out_spec,
                scratch_shapes=[row_buf, row_buf, pltpu.VMEM((DIFF_DV, 2 * tq), F32), s_buf, s_buf,
                                row_buf, row_buf, pltpu.VMEM((tk, tq), F32)]),
            out_shape=out_shape,
            compiler_params=_params("arbitrary", "arbitrary", "arbitrary"),
            name="diffattn_general",
        )(stepcls, chunkcls, table2, qT, kd, vT, strips, posq, posk, *vecs, nw_b)

    return lax.cond(jnp.all(chunkcls != CLS_GENERIC), fast, general, 0)


def _outproj_kernel(of_ref, ob_ref, rg_ref, gnw_ref, od_ref, wa_ref, wb_ref, ga_ref, x_ref, o_ref):
    o = of_ref[...] + ob_ref[...]
    parts = []
    for hd in range(GLA_HEADS):
        oh = o[:, hd * GLA_DV:(hd + 1) * GLA_DV]
        ms = jnp.mean(oh * oh, axis=-1, keepdims=True)
        parts.append(oh * lax.rsqrt(ms + EPS))
    r = rg_ref[...]
    oa = jnp.concatenate(parts, axis=1) * gnw_ref[...] * (r * _sigmoid(r))
    mixed = _dot(oa.astype(BF16), wa_ref[...]) + _dot(od_ref[...], wb_ref[...])
    o_ref[...] = x_ref[...] + ga_ref[0] * mixed


def _outproj(of, ob, rg, gnw, od, wa, wb, ga, x2d, L, tm):
    T, D = x2d.shape
    per_b = L // tm
    row = lambda i: (i, 0)
    full = lambda i: (0, 0)
    bvec = lambda i: (i // per_b, 0, 0)
    return pl.pallas_call(
        _outproj_kernel,
        grid=(T // tm,),
        in_specs=[pl.BlockSpec((tm, GLA_WIDTH), row), pl.BlockSpec((tm, GLA_WIDTH), row),
                  pl.BlockSpec((tm, GLA_WIDTH), row), pl.BlockSpec((1, GLA_WIDTH), full),
                  pl.BlockSpec((tm, DIFF_WIDTH), row),
                  pl.BlockSpec(wa.shape, full), pl.BlockSpec(wb.shape, full),
                  pl.BlockSpec((1, 1, D), bvec), pl.BlockSpec((tm, D), row)],
        out_specs=pl.BlockSpec((tm, D), row),
        out_shape=jax.ShapeDtypeStruct((T, D), F32),
        compiler_params=_params("arbitrary"),
        name="outproj",
    )(of, ob, rg, gnw, od, wa, wb, ga, x2d)


HALO = 8
FF_CHUNK = 256


def _ffn_kernel(x_ref, xp_ref, xn_ref, sh_ref, sc_ref, gf_ref, nw_ref, wup_ref, cw_ref, cb_ref, wdn_ref, fw_ref,
                o_ref, acc_ref, *, tm, per_b):
    i = pl.program_id(0)
    first = (i % per_b) == 0
    last = (i % per_b) == per_b - 1
    nw, sh, sc = nw_ref[...], sh_ref[0], sc_ref[0]
    x = x_ref[...]
    h_mid = _rms_mod(x, nw, sh, sc)
    h_prev = jnp.where(first, 0.0, _rms_mod(xp_ref[...], nw, sh, sc))
    h_next = jnp.where(last, 0.0, _rms_mod(xn_ref[...], nw, sh, sc))
    h = jnp.concatenate([h_prev, h_mid, h_next], axis=0).astype(BF16)
    rows = tm + 2 * HALO

    def conv(u, col):
        w = cw_ref[:, col:col + FF_CHUNK]
        up = pltpu.roll(u, 1, 0)[HALO:HALO + tm]
        un = pltpu.roll(u, rows - 1, 0)[HALO:HALO + tm]
        return (w[0:1] * up + w[1:2] * u[HALO:HALO + tm] + w[2:3] * un) + cb_ref[:, col:col + FF_CHUNK]

    def up(n):
        cg, cv = n * FF_CHUNK, D_FF + n * FF_CHUNK
        return _dot(h, wup_ref[:, cg:cg + FF_CHUNK]), _dot(h, wup_ref[:, cv:cv + FF_CHUNK])

    n_chunks = D_FF // FF_CHUNK
    u_gate, u_val = up(0)
    for n in range(n_chunks):
        cg, cv = n * FF_CHUNK, D_FF + n * FF_CHUNK
        if n + 1 < n_chunks:
            nxt = up(n + 1)
        gate = conv(u_gate, cg)
        val = conv(u_val, cv)
        a = (gate * _sigmoid(gate) * val).astype(BF16)
        part = _dot(a, wdn_ref[cg:cg + FF_CHUNK, :])
        if n == 0:
            acc_ref[...] = part
        else:
            acc_ref[...] += part
        if n + 1 < n_chunks:
            u_gate, u_val = nxt
    y = x + gf_ref[0] * acc_ref[...]
    ms = jnp.mean(y * y, axis=-1, keepdims=True)
    o_ref[...] = y * lax.rsqrt(ms + EPS) * fw_ref[...]


def _ffn(x1, sh, sc, gf, nw, wup, cw, cb, wdn, fw, L, tm):
    T, D = x1.shape
    per_b = L // tm
    hb = tm // HALO
    n_halo = T // HALO
    row = lambda i: (i, 0)
    full = lambda i: (0, 0)
    bvec = lambda i: (i // per_b, 0, 0)
    return pl.pallas_call(
        functools.partial(_ffn_kernel, tm=tm, per_b=per_b),
        grid=(T // tm,),
        in_specs=[pl.BlockSpec((tm, D), row),
                  pl.BlockSpec((HALO, D), lambda i: (jnp.maximum(i * hb - 1, 0), 0)),
                  pl.BlockSpec((HALO, D), lambda i: (jnp.minimum((i + 1) * hb, n_halo - 1), 0)),
                  pl.BlockSpec((1, 1, D), bvec), pl.BlockSpec((1, 1, D), bvec), pl.BlockSpec((1, 1, D), bvec),
                  pl.BlockSpec((1, D), full),
                  pl.BlockSpec(wup.shape, full), pl.BlockSpec(cw.shape, full), pl.BlockSpec(cb.shape, full),
                  pl.BlockSpec(wdn.shape, full), pl.BlockSpec((1, D), full)],
        out_specs=pl.BlockSpec((tm, D), row),
        out_shape=jax.ShapeDtypeStruct((T, D), F32),
        scratch_shapes=[pltpu.VMEM((tm, D), F32)],
        compiler_params=_params("arbitrary"),
        name="ffn",
    )(x1, x1, x1, sh, sc, gf, nw, wup, cw, cb, wdn, fw)


def kernel(x, c, positions, w_ada, b_ada, attn_norm_w, w_in, gla_dec_w_fwd, gla_dec_b_fwd, gla_dec_w_bwd,
           gla_dec_b_bwd, gla_norm_w, diff_lambda_q1, diff_lambda_k1, diff_lambda_q2, diff_lambda_k2,
           diff_norm_w, rel_bias_table, w_out, ffn_norm_w, w_up, conv_w, conv_b, w_down, final_norm_w):
    B, L, D = x.shape
    T = B * L
    tm = min(512, L)
    x2d = x.reshape(T, D)

    mod = _ada(c, w_ada[0], b_ada[0])
    sh_a, sc_a, g_a, sh_f, sc_f, g_f = [m.reshape(B, 1, D) for m in jnp.split(mod, N_MOD, axis=-1)]

    o_lr = 2 * GLA_QK + 2 * GLA_WIDTH
    o_d = o_lr + 2 * GLA_RANK
    w = w_in[0]
    wg = w[:, :o_lr].astype(BF16)
    wlr = jnp.pad(w[:, o_lr:o_d], ((0, 0), (0, 128 - 2 * GLA_RANK))).astype(BF16)
    wd = w[:, o_d:].astype(BF16)
    decw = jnp.zeros((128, 2 * GLA_QK), F32)
    decw = decw.at[:GLA_RANK, :GLA_QK].set(gla_dec_w_fwd[0]).at[GLA_RANK:2 * GLA_RANK, GLA_QK:].set(gla_dec_w_bwd[0])
    decb = jnp.concatenate([gla_dec_b_fwd[0], gla_dec_b_bwd[0]]).reshape(1, 2 * GLA_QK)

    qg, kg, vg, rg, laf, lab, qd, kd, vd = _inproj(
        x2d, sh_a, sc_a, attn_norm_w[0].reshape(1, D), wg, wlr, wd, decw.astype(BF16), decb, L, tm)

    o_f, o_b = _gla(qg, kg, vg, laf, lab, B, L, min(512, L))

    o_d = _diffattn(qd, kd.reshape(B, L, DIFF_WIDTH), vd, positions, rel_bias_table,
                    diff_lambda_q1[0], diff_lambda_k1[0], diff_lambda_q2[0], diff_lambda_k2[0],
                    diff_norm_w[0], min(512, L), min(1024, L // 2))

    wo = w_out[0].astype(BF16)
    x1 = _outproj(o_f, o_b, rg, gla_norm_w[0].reshape(1, GLA_WIDTH), o_d.reshape(T, DIFF_WIDTH),
                  wo[:GLA_WIDTH], wo[GLA_WIDTH:], g_a, x2d, L, tm)

    out = _ffn(x1, sh_f, sc_f, g_f, ffn_norm_w[0].reshape(1, D), w_up[0].astype(BF16), conv_w[0],
               conv_b[0].reshape(1, 2 * D_FF), w_down[0].astype(BF16), final_norm_w.reshape(1, D), L, tm)
    return out.reshape(B, L, D)
```

```python
import functools
import math

import jax
import jax.numpy as jnp
from jax import lax
from jax.experimental import pallas as pl
from jax.experimental.pallas import tpu as pltpu

F32 = jnp.float32
BF16 = jnp.bfloat16

D_MODEL = 1024
GLA_HEADS = 4
GLA_DK = 64
GLA_DV = 128
GLA_RANK = 16
GLA_GATE_TEMP = 16.0
GLA_CHUNK = 64
GLA_QK = GLA_HEADS * GLA_DK
GLA_WIDTH = GLA_HEADS * GLA_DV
DIFF_HEADS = 4
DIFF_DH = 64
DIFF_DV = 2 * DIFF_DH
DIFF_WIDTH = DIFF_HEADS * DIFF_DV
REL_BUCKETS = 32
D_FF = 2816
N_MOD = 6
EPS = 1e-6
LAMBDA_INIT = 0.8 - 0.6 * math.exp(-0.3 * 0)
LOG2E = 1.4426950408889634
ONES_ROWS = 16

BUCKET_STEPS = (1, 2, 3, 4, 5, 6, 7, 8, 12, 16, 23, 32, 46, 64, 91)
FAR_DIST = BUCKET_STEPS[-1]

VMEM_LIMIT = 56 * 1024 * 1024


def _split_bf16(a):
    hi = a.astype(BF16)
    lo = (a - hi.astype(F32)).astype(BF16)
    return hi, lo


def _dot(a, b):
    return jnp.dot(a, b, preferred_element_type=F32)


def _sigmoid(v):
    return 1.0 / (1.0 + jnp.exp(-v))


def _params(*sem):
    return pltpu.CompilerParams(dimension_semantics=sem, vmem_limit_bytes=VMEM_LIMIT)


def _ada_kernel(c_ref, w_ref, b_ref, o_ref):
    c = c_ref[...]
    s_hi, s_lo = _split_bf16(c * _sigmoid(c))
    w_hi, w_lo = _split_bf16(w_ref[...])
    o_ref[...] = (_dot(s_hi, w_hi) + _dot(s_lo, w_hi) + _dot(s_hi, w_lo)) + b_ref[...]


def _ada(c, w, b):
    B, D = c.shape
    N = w.shape[1]
    tn = 1024
    return pl.pallas_call(
        _ada_kernel,
        grid=(N // tn,),
        in_specs=[pl.BlockSpec((B, D), lambda j: (0, 0)),
                  pl.BlockSpec((D, tn), lambda j: (0, j)),
                  pl.BlockSpec((1, tn), lambda j: (0, j))],
        out_specs=pl.BlockSpec((B, tn), lambda j: (0, j)),
        out_shape=jax.ShapeDtypeStruct((B, N), F32),
        compiler_params=_params("arbitrary"),
        name="ada",
    )(c, w, b.reshape(1, N))


def _rms_mod(x, nw, sh, sc):
    ms = jnp.mean(x * x, axis=-1, keepdims=True)
    return (x * lax.rsqrt(ms + EPS) * nw) * (1.0 + sc) + sh


def _inproj_kernel(x_ref, sh_ref, sc_ref, nw_ref, wg_ref, wlr_ref, wd_ref, decw_ref, decb_ref,
                   qg_ref, kg_ref, vg_ref, rg_ref, laf_ref, lab_ref, qd_ref, kd_ref, vd_ref):
    h = _rms_mod(x_ref[...], nw_ref[...], sh_ref[0], sc_ref[0]).astype(BF16)
    g = _dot(h, wg_ref[...])
    qg_ref[...] = g[:, :GLA_QK] * (GLA_DK ** -0.5)
    kg_ref[...] = g[:, GLA_QK:2 * GLA_QK]
    vg_ref[...] = g[:, 2 * GLA_QK:2 * GLA_QK + GLA_WIDTH].astype(BF16)
    rg_ref[...] = g[:, 2 * GLA_QK + GLA_WIDTH:]
    lr = _dot(h, wlr_ref[...])
    z = _dot(lr.astype(BF16), decw_ref[...]) + decb_ref[...]
    la = (jnp.minimum(z, 0.0) - jnp.log(1.0 + jnp.exp(-jnp.abs(z)))) * (1.0 / GLA_GATE_TEMP)
    laf_ref[...] = la[:, :GLA_QK]
    lab_ref[...] = la[:, GLA_QK:]
    d = _dot(h, wd_ref[...])
    qd_ref[0] = (d[:, :DIFF_WIDTH] * (DIFF_DH ** -0.5 * LOG2E)).T.astype(BF16)
    kd_ref[...] = d[:, DIFF_WIDTH:2 * DIFF_WIDTH].astype(BF16)
    vt = d[:, 2 * DIFF_WIDTH:].T.astype(BF16)
    for hd in range(DIFF_HEADS):
        vd_ref[0, hd, :DIFF_DV, :] = vt[hd * DIFF_DV:(hd + 1) * DIFF_DV, :]
    vd_ref[0, :, DIFF_DV:, :] = jnp.ones((DIFF_HEADS, ONES_ROWS, vt.shape[1]), BF16)


def _inproj(x2d, sh, sc, nw, wg, wlr, wd, decw, decb, L, tm):
    T, D = x2d.shape
    per_b = L // tm
    row = lambda i: (i, 0)
    full = lambda i: (0, 0)
    bvec = lambda i: (i // per_b, 0, 0)
    B = T // L
    outs = [(GLA_QK, F32), (GLA_QK, F32), (GLA_WIDTH, BF16), (GLA_WIDTH, F32), (GLA_QK, F32), (GLA_QK, F32)]
    col = lambda i: (i // per_b, 0, i % per_b)
    col4 = lambda i: (i // per_b, 0, 0, i % per_b)
    return pl.pallas_call(
        _inproj_kernel,
        grid=(T // tm,),
        in_specs=[pl.BlockSpec((tm, D), row),
                  pl.BlockSpec((1, 1, D), bvec), pl.BlockSpec((1, 1, D), bvec),
                  pl.BlockSpec((1, D), full),
                  pl.BlockSpec(wg.shape, full), pl.BlockSpec(wlr.shape, full), pl.BlockSpec(wd.shape, full),
                  pl.BlockSpec(decw.shape, full), pl.BlockSpec(decb.shape, full)],
        out_specs=[pl.BlockSpec((tm, n), row) for n, _ in outs] + [
            pl.BlockSpec((1, DIFF_WIDTH, tm), col), pl.BlockSpec((tm, DIFF_WIDTH), row),
            pl.BlockSpec((1, DIFF_HEADS, DIFF_DV + ONES_ROWS, tm), col4)],
        out_shape=[jax.ShapeDtypeStruct((T, n), dt) for n, dt in outs] + [
            jax.ShapeDtypeStruct((B, DIFF_WIDTH, L), BF16), jax.ShapeDtypeStruct((T, DIFF_WIDTH), BF16),
            jax.ShapeDtypeStruct((B, DIFF_HEADS, DIFF_DV + ONES_ROWS, L), BF16)],
        compiler_params=_params("arbitrary"),
        name="inproj",
    )(x2d, sh, sc, nw, wg, wlr, wd, decw, decb)


def _gla_kernel(qf_ref, kf_ref, vf_ref, laf_ref, qb_ref, kb_ref, vb_ref, lab_ref,
                of_ref, ob_ref, sf_ref, sb_ref, *, n_chunks, group):
    C = GLA_CHUNK

    @pl.when(pl.program_id(1) == 0)
    def _():
        sf_ref[...] = jnp.zeros_like(sf_ref)
        sb_ref[...] = jnp.zeros_like(sb_ref)

    r = lax.broadcasted_iota(jnp.int32, (C, C), 0)
    s = lax.broadcasted_iota(jnp.int32, (C, C), 1)
    tri_f = jnp.where(s <= r, 1.0, 0.0).astype(BF16)
    tri_b = jnp.where(s >= r, 1.0, 0.0).astype(BF16)
    r4 = lax.broadcasted_iota(jnp.int32, (GLA_HEADS * C, C), 0) % C
    s4 = lax.broadcasted_iota(jnp.int32, (GLA_HEADS * C, C), 1)
    mask_f = s4 <= r4
    mask_b = s4 > r4
    lane = lax.broadcasted_iota(jnp.int32, (C, GLA_QK), 1)
    head_masks = [(lane >= hd * GLA_DK) & (lane < (hd + 1) * GLA_DK) for hd in range(GLA_HEADS)]
    nt = (((1,), (1,)), ((), ()))

    def per_head(x):
        zero = jnp.zeros_like(x)
        return [jnp.where(mk, x, zero) for mk in head_masks]

    dirs = ((qf_ref, kf_ref, vf_ref, laf_ref, of_ref, sf_ref, tri_f, mask_f, C - 1, False),
            (qb_ref, kb_ref, vb_ref, lab_ref, ob_ref, sb_ref, tri_b, mask_b, 0, True))

    def run_group(g):
        work = []
        for (q_ref, k_ref, v_ref, la_ref, o_ref, s_ref, tri, att_mask, last_row, rev) in dirs:
            for i in range(group):
                c = g * group + i
                c = (n_chunks - 1 - c) if rev else c
                work.append(dict(rows=pl.ds(pl.multiple_of(c * C, C), C), q_ref=q_ref, k_ref=k_ref, v_ref=v_ref,
                                 la_ref=la_ref, o_ref=o_ref, s_ref=s_ref, tri=tri, att_mask=att_mask,
                                 last_row=last_row))
        for w in work:
            la_hi, la_lo = _split_bf16(w["la_ref"][w["rows"], :])
            w["b"] = _dot(w["tri"], la_hi) + _dot(w["tri"], la_lo)
        for w in work:
            b = w["b"]
            tot = b[w["last_row"]:w["last_row"] + 1, :]
            q = w["q_ref"][w["rows"], :]
            k = w["k_ref"][w["rows"], :]
            w["dec"] = jnp.exp(tot)
            w["qs"] = jnp.concatenate(per_head((q * jnp.exp(b)).astype(BF16)), axis=0)
            w["k_in"] = (k * jnp.exp(-b)).astype(BF16)
            w["k_st"] = jnp.concatenate(per_head((k * jnp.exp(tot - b)).astype(BF16)), axis=0)
            v = w["v_ref"][w["rows"], :]
            w["v"] = v
            v_rows = jnp.concatenate([v[:, hd * GLA_DV:(hd + 1) * GLA_DV] for hd in range(GLA_HEADS)], axis=0)
            w["vt"] = v_rows.astype(F32).T.astype(BF16)
        for w in work:
            att = lax.dot_general(w["qs"], w["k_in"], nt, preferred_element_type=F32)
            w["att"] = jnp.where(w["att_mask"], att, 0.0).astype(BF16)
        for w in work:
            w["o"] = jnp.concatenate(
                [_dot(w["att"][hd * C:(hd + 1) * C, :], w["v"][:, hd * GLA_DV:(hd + 1) * GLA_DV])
                 for hd in range(GLA_HEADS)], axis=1)
        for w in work:
            w["upd"] = _dot(w["vt"], w["k_st"])
        for d in range(2):
            s_ref = dirs[d][5]
            st = s_ref[...]
            for w in work[d * group:(d + 1) * group]:
                oi = lax.dot_general(w["qs"], st.astype(BF16), nt, preferred_element_type=F32)
                oi = jnp.concatenate([oi[hd * C:(hd + 1) * C, :] for hd in range(GLA_HEADS)], axis=1)
                w["o_ref"][w["rows"], :] = w["o"] + oi
                st = st * w["dec"] + w["upd"]
            s_ref[...] = st

    n_groups = n_chunks // group
    if n_groups == 1:
        run_group(0)
    else:
        def body(g, carry):
            run_group(g)
            return carry
        lax.fori_loop(0, n_groups, body, 0)


def _gla(qg, kg, vg, laf, lab, B, L, tb):
    T = B * L
    J = L // tb
    fwd = lambda b, j: (b * J + j, 0)
    bwd = lambda b, j: (b * J + (J - 1 - j), 0)
    qk = lambda im: pl.BlockSpec((tb, GLA_QK), im)
    vv = lambda im: pl.BlockSpec((tb, GLA_WIDTH), im)
    return pl.pallas_call(
        functools.partial(_gla_kernel, n_chunks=tb // GLA_CHUNK, group=min(4, tb // GLA_CHUNK)),
        grid=(B, J),
        in_specs=[qk(fwd), qk(fwd), vv(fwd), qk(fwd), qk(bwd), qk(bwd), vv(bwd), qk(bwd)],
        out_specs=[vv(fwd), vv(bwd)],
        out_shape=[jax.ShapeDtypeStruct((T, GLA_WIDTH), F32)] * 2,
        scratch_shapes=[pltpu.VMEM((GLA_DV, GLA_QK), F32)] * 2,
        compiler_params=_params("arbitrary", "arbitrary"),
        name="gla",
    )(qg, kg, vg, laf, qg, kg, vg, lab)


def _bias_from_rel(rel, tab_ref, hd):
    n = jnp.abs(rel)
    half = REL_BUCKETS // 2
    neg = jnp.full(rel.shape, tab_ref[0, hd], F32)
    pos = jnp.full(rel.shape, tab_ref[half, hd], F32)
    for j, t in enumerate(BUCKET_STEPS):
        ge = n >= t
        neg = jnp.where(ge, tab_ref[j + 1, hd], neg)
        pos = jnp.where(ge, tab_ref[half + j + 1, hd], pos)
    return jnp.where(rel > 0, pos, neg)


KEY_CHUNK = 128
COL_TILE = 256
CLS_LEFT, CLS_RIGHT, CLS_STRIP0, CLS_GENERIC = 0, 1, 2, 99
STEP_LEFT, STEP_RIGHT, STEP_MIXED = 0, 1, 2


def _strips_kernel(tab_ref, o_ref, *, tq, n_strips):
    hd = pl.program_id(0)
    kk = lax.broadcasted_iota(jnp.int32, (KEY_CHUNK, tq), 0)
    qq = lax.broadcasted_iota(jnp.int32, (KEY_CHUNK, tq), 1)
    o_ref[0, CLS_LEFT] = jnp.full((KEY_CHUNK, tq), tab_ref[REL_BUCKETS // 2 - 1, hd], F32)
    o_ref[0, CLS_RIGHT] = jnp.full((KEY_CHUNK, tq), tab_ref[REL_BUCKETS - 1, hd], F32)
    for j in range(n_strips):
        o_ref[0, CLS_STRIP0 + j] = _bias_from_rel(KEY_CHUNK * (j - 1) + kk - qq, tab_ref, hd)


def _strips(table, tq, n_strips):
    H = table.shape[1]
    return pl.pallas_call(
        functools.partial(_strips_kernel, tq=tq, n_strips=n_strips),
        grid_spec=pltpu.PrefetchScalarGridSpec(
            num_scalar_prefetch=1, grid=(H,), in_specs=[],
            out_specs=pl.BlockSpec((1, CLS_STRIP0 + n_strips, KEY_CHUNK, tq), lambda h, *_: (h, 0, 0, 0))),
        out_shape=jax.ShapeDtypeStruct((H, CLS_STRIP0 + n_strips, KEY_CHUNK, tq), F32),
        compiler_params=_params("arbitrary"),
        name="bias_strips",
    )(table)


def _diff_finish(acc, l, lq1_ref, lk1_ref, lq2_ref, lk2_ref, nw_ref, o_ref, tq):
    lam = (jnp.exp(jnp.sum(lq1_ref[...] * lk1_ref[...], axis=1, keepdims=True))
           - jnp.exp(jnp.sum(lq2_ref[...] * lk2_ref[...], axis=1, keepdims=True)) + LAMBDA_INIT)
    o = acc[:, :tq] / l[:, :tq] - lam * (acc[:, tq:] / l[:, tq:])
    ms = jnp.mean(o * o, axis=0, keepdims=True)
    nw = jnp.concatenate([nw_ref[...]] * (tq // 128), axis=1)
    o = o * lax.rsqrt(ms + EPS) * nw * (1.0 - LAMBDA_INIT)
    o_ref[0] = o.T.astype(o_ref.dtype)


def _stack_maps(qT):
    row = lax.broadcasted_iota(jnp.int32, qT.shape, 0)
    zero = jnp.zeros_like(qT)
    return jnp.concatenate([jnp.where(row < DIFF_DH, qT, zero), jnp.where(row >= DIFF_DH, qT, zero)], axis=1)


def _diff_fast_kernel(chunkcls_ref, tab_ref,
                      qT_ref, k_ref, vT_ref, strip_ref, lq1_ref, lk1_ref, lq2_ref, lk2_ref, nw_ref,
                      o_ref, qs_ref, m_ref, acc_ref, sa_ref, sb_ref, smaxa_ref, smaxb_ref,
                      *, tq, tk, n_kt, n_qt):
    hd = pl.program_id(1)
    r0 = pl.program_id(0) * n_qt
    n_ct = 2 * tq // COL_TILE
    n_chunk = tk // KEY_CHUNK
    for qt in range(n_qt):
        qs_ref[qt] = _stack_maps(qT_ref[0, :, qt * tq:(qt + 1) * tq])
    bias_max = functools.reduce(jnp.maximum, [tab_ref[i, hd] for i in range(REL_BUCKETS)])

    def scores_tile(qt, kt, buf, j):
        s_ref, smax_ref = buf
        kblk = k_ref[0, pl.ds(pl.multiple_of(kt * tk, tk), tk), :]
        cols = slice(j * COL_TILE, (j + 1) * COL_TILE)
        s = _dot(kblk, qs_ref[qt, :, cols])
        s_ref[:, cols] = s
        smax_ref[:, cols] = jnp.max(s, axis=0, keepdims=True)

    def chain(qt, kt, buf, j):
        s_ref, smax_ref = buf
        cols = slice(j * COL_TILE, (j + 1) * COL_TILE)
        q0 = (j * COL_TILE) % tq
        m_old = m_ref[:, cols]
        m_new = jnp.maximum(m_old, smax_ref[:, cols] + bias_max)
        alpha = jnp.exp2(m_old - m_new)
        ps = []
        for c in range(n_chunk):
            rows = slice(c * KEY_CHUNK, (c + 1) * KEY_CHUNK)
            bias = strip_ref[0, chunkcls_ref[r0 + qt, kt * n_chunk + c], :, q0:q0 + COL_TILE]
            ps.append(jnp.exp2((s_ref[rows, cols] + bias) - m_new).astype(BF16))
        vblk = vT_ref[0, 0, :, pl.ds(pl.multiple_of(kt * tk, tk), tk)]
        acc_ref[:, cols] = alpha * acc_ref[:, cols] + _dot(vblk, jnp.concatenate(ps, axis=0))
        m_ref[:, cols] = m_new

    def step(qt, kt, cur, nxt, nxt_tile):
        for j in range(n_ct):
            if nxt_tile is not None:
                scores_tile(nxt_tile[0], nxt_tile[1], nxt, j)
            chain(qt, kt, cur, j)

    buf_a = (sa_ref, smaxa_ref)
    buf_b = (sb_ref, smaxb_ref)

    def query_tile(qt, handoff):
        m_ref[...] = jnp.full(m_ref.shape, -1e30, F32)
        acc_ref[...] = jnp.zeros(acc_ref.shape, F32)

        def pair(kk, carry):
            step(qt, 2 * kk, buf_a, buf_b, (qt, 2 * kk + 1))
            step(qt, 2 * kk + 1, buf_b, buf_a, (qt, 2 * kk + 2))
            return carry

        lax.fori_loop(0, n_kt // 2 - 1, pair, 0)
        step(qt, n_kt - 2, buf_a, buf_b, (qt, n_kt - 1))
        step(qt, n_kt - 1, buf_b, buf_a, (qt + 1, 0) if handoff else None)
        acc = acc_ref[...]
        _diff_finish(acc[:DIFF_DV], acc[DIFF_DV:DIFF_DV + 1], lq1_ref, lk1_ref, lq2_ref, lk2_ref, nw_ref,
                     o_ref.at[:, pl.ds(pl.multiple_of(qt * tq, tq), tq), :], tq)

    for j in range(n_ct):
        scores_tile(0, 0, buf_a, j)

    def body(qt, carry):
        query_tile(qt, True)
        return carry

    lax.fori_loop(0, n_qt - 1, body, 0)
    query_tile(n_qt - 1, False)


def _diff_kernel(stepcls_ref, chunkcls_ref, tab_ref,
                 qT_ref, k_ref, vT_ref, strip_ref, posq_ref, posk_ref, lq1_ref, lk1_ref, lq2_ref, lk2_ref, nw_ref,
                 o_ref, m_ref, l_ref, acc_ref, sa_ref, sb_ref, smaxa_ref, smaxb_ref, bias_ref,
                 *, tq, tk, n_kt, n_strips):
    hd = pl.program_id(1)
    r = pl.program_id(0) * pl.num_programs(2) + pl.program_id(2)
    half = REL_BUCKETS // 2
    n_ct = 2 * tq // COL_TILE
    n_chunk = tk // KEY_CHUNK

    qstack = _stack_maps(qT_ref[0])

    m_ref[...] = jnp.full(m_ref.shape, -1e30, F32)
    l_ref[...] = jnp.zeros(l_ref.shape, F32)
    acc_ref[...] = jnp.zeros(acc_ref.shape, F32)

    bias_max = functools.reduce(jnp.maximum, [tab_ref[i, hd] for i in range(REL_BUCKETS)])

    def scores_tile(kt, buf, j):
        s_ref, smax_ref = buf
        kblk = k_ref[0, pl.ds(pl.multiple_of(kt * tk, tk), tk), :]
        cols = slice(j * COL_TILE, (j + 1) * COL_TILE)
        s = _dot(kblk, qstack[:, cols])
        s_ref[:, cols] = s
        smax_ref[:, cols] = jnp.max(s, axis=0, keepdims=True)

    def scores(kt, buf):
        for j in range(n_ct):
            scores_tile(kt, buf, j)

    def build_bias(kt):
        posq = posq_ref[0]
        for c in range(n_chunk):
            cc = chunkcls_ref[r, kt * n_chunk + c]
            rows = slice(c * KEY_CHUNK, (c + 1) * KEY_CHUNK)

            @pl.when(cc == CLS_LEFT)
            def _():
                bias_ref[rows, :] = jnp.full((KEY_CHUNK, tq), tab_ref[half - 1, hd], F32)

            @pl.when(cc == CLS_RIGHT)
            def _():
                bias_ref[rows, :] = jnp.full((KEY_CHUNK, tq), tab_ref[REL_BUCKETS - 1, hd], F32)

            @pl.when(jnp.logical_and(cc >= CLS_STRIP0, cc < CLS_STRIP0 + n_strips))
            def _():
                bias_ref[rows, :] = strip_ref[0, cc]

            @pl.when(cc == CLS_GENERIC)
            def _():
                k0 = pl.multiple_of(kt * tk + c * KEY_CHUNK, KEY_CHUNK)
                pk = posk_ref[0, pl.ds(k0, KEY_CHUNK), :]
                pk = jnp.concatenate([pk] * (tq // 128), axis=1)
                bias_ref[rows, :] = _bias_from_rel(pk - posq, tab_ref, hd)

    def softmax_pv(kt, cur, const, nxt):
        s_ref, smax_ref = cur
        vblk = vT_ref[0, 0, :DIFF_DV, pl.ds(pl.multiple_of(kt * tk, tk), tk)]
        for j in range(n_ct):
            if nxt is not None:
                scores_tile(kt + 1, nxt, j)
            cols = slice(j * COL_TILE, (j + 1) * COL_TILE)
            shift = bias_max if const is None else const
            m_old = m_ref[:, cols]
            m_new = jnp.maximum(m_old, smax_ref[:, cols] + shift)
            alpha = jnp.exp2(m_old - m_new)
            if const is None:
                q0 = (j * COL_TILE) % tq
                p = jnp.exp2((s_ref[:, cols] + bias_ref[:, q0:q0 + COL_TILE]) - m_new)
            else:
                p = jnp.exp2(s_ref[:, cols] - (m_new - shift))
            l_ref[:, cols] = alpha * l_ref[:, cols] + jnp.sum(p, axis=0, keepdims=True)
            acc_ref[:, cols] = alpha * acc_ref[:, cols] + _dot(vblk, p.astype(BF16))
            m_ref[:, cols] = m_new

    def step(kt, cur, nxt):
        cls = stepcls_ref[r, kt]

        @pl.when(cls == STEP_MIXED)
        def _():
            build_bias(kt)
            softmax_pv(kt, cur, None, nxt)

        @pl.when(cls != STEP_MIXED)
        def _():
            const = jnp.where(cls == STEP_LEFT, tab_ref[half - 1, hd], tab_ref[REL_BUCKETS - 1, hd])
            softmax_pv(kt, cur, const, nxt)

    buf_a = (sa_ref, smaxa_ref)
    buf_b = (sb_ref, smaxb_ref)
    scores(0, buf_a)

    def pair(kk, carry):
        step(2 * kk, buf_a, buf_b)
        step(2 * kk + 1, buf_b, buf_a)
        return carry

    lax.fori_loop(0, n_kt // 2 - 1, pair, 0)
    step(n_kt - 2, buf_a, buf_b)
    step(n_kt - 1, buf_b, None)

    _diff_finish(acc_ref[...], l_ref[...], lq1_ref, lk1_ref, lq2_ref, lk2_ref, nw_ref, o_ref, tq)


def _classify(positions, tq, tk, n_strips):
    B, L = positions.shape
    nq, nc = L // tq, L // KEY_CHUNK
    pq = positions.reshape(B, nq, tq)
    pk = positions.reshape(B, nc, KEY_CHUNK)
    qlo, qhi, q0 = pq.min(axis=2), pq.max(axis=2), pq[:, :, 0]
    klo, khi, k0 = pk.min(axis=2), pk.max(axis=2), pk[:, :, 0]
    q_run = jnp.all(pq == q0[:, :, None] + jnp.arange(tq, dtype=positions.dtype), axis=2)
    k_run = jnp.all(pk == k0[:, :, None] + jnp.arange(KEY_CHUNK, dtype=positions.dtype), axis=2)
    left = (qlo[:, :, None] - khi[:, None, :]) >= FAR_DIST
    right = (klo[:, None, :] - qhi[:, :, None]) >= FAR_DIST
    d = k0[:, None, :] - q0[:, :, None]
    j = d // KEY_CHUNK + 1
    strip_ok = (q_run[:, :, None] & k_run[:, None, :] & (d % KEY_CHUNK == 0) & (j >= 0) & (j < n_strips))
    cls = jnp.where(left, CLS_LEFT, jnp.where(right, CLS_RIGHT, jnp.where(strip_ok, CLS_STRIP0 + j, CLS_GENERIC)))
    cls = cls.astype(jnp.int32)
    per_step = cls.reshape(B, nq, L // tk, tk // KEY_CHUNK)
    step = jnp.where(jnp.all(per_step == CLS_LEFT, axis=3), STEP_LEFT,
                     jnp.where(jnp.all(per_step == CLS_RIGHT, axis=3), STEP_RIGHT, STEP_MIXED)).astype(jnp.int32)
    return step.reshape(B * nq, L // tk), cls.reshape(B * nq, nc)


def _diffattn(qT, kd, vT, positions, table, lq1, lk1, lq2, lk2, nw, tq, tk):
    B, _, L = qT.shape
    nq, nk = L // tq, L // tk
    n_strips = tq // KEY_CHUNK + 2
    n_cls = CLS_STRIP0 + n_strips
    table2 = table * LOG2E
    strips = _strips(table2, tq, n_strips)
    stepcls, chunkcls = _classify(positions, tq, tk, n_strips)
    nw_b = jnp.broadcast_to(nw.reshape(DIFF_DV, 1), (DIFF_DV, 128))
    vecs = [a.reshape(1, DIFF_DH) for a in (lq1, lk1, lq2, lk2)]
    full2 = lambda b, h, i, *_: (0, 0)
    q_spec = pl.BlockSpec((1, DIFF_DV, tq), lambda b, h, i, *_: (b, h, i))
    k_spec = pl.BlockSpec((1, L, DIFF_DV), lambda b, h, i, *_: (b, 0, h))
    v_spec = pl.BlockSpec((1, 1, DIFF_DV + ONES_ROWS, L), lambda b, h, i, *_: (b, h, 0, 0))
    strip_spec = pl.BlockSpec((1, n_cls, KEY_CHUNK, tq), lambda b, h, i, *_: (h, 0, 0, 0))
    tail_specs = [pl.BlockSpec((1, DIFF_DH), full2)] * 4 + [pl.BlockSpec((DIFF_DV, 128), full2)]
    out_spec = pl.BlockSpec((1, tq, DIFF_DV), lambda b, h, i, *_: (b, i, h))
    out_shape = jax.ShapeDtypeStruct((B, L, DIFF_WIDTH), BF16)
    s_buf = pltpu.VMEM((tk, 2 * tq), F32)
    row_buf = pltpu.VMEM((1, 2 * tq), F32)

    def fast(_):
        one = lambda b, h, *_: (0, 0)
        return pl.pallas_call(
            functools.partial(_diff_fast_kernel, tq=tq, tk=tk, n_kt=nk, n_qt=nq),
            grid_spec=pltpu.PrefetchScalarGridSpec(
                num_scalar_prefetch=2, grid=(B, DIFF_HEADS),
                in_specs=[pl.BlockSpec((1, DIFF_DV, L), lambda b, h, *_: (b, h, 0)),
                          pl.BlockSpec((1, L, DIFF_DV), lambda b, h, *_: (b, 0, h)),
                          pl.BlockSpec((1, 1, DIFF_DV + ONES_ROWS, L), lambda b, h, *_: (b, h, 0, 0)),
                          pl.BlockSpec((1, n_cls, KEY_CHUNK, tq), lambda b, h, *_: (h, 0, 0, 0))]
                + [pl.BlockSpec((1, DIFF_DH), one)] * 4 + [pl.BlockSpec((DIFF_DV, 128), one)],
                out_specs=pl.BlockSpec((1, L, DIFF_DV), lambda b, h, *_: (b, 0, h)),
                scratch_shapes=[pltpu.VMEM((nq, DIFF_DV, 2 * tq), BF16),
                                row_buf, pltpu.VMEM((DIFF_DV + ONES_ROWS, 2 * tq), F32),
                                s_buf, s_buf, row_buf, row_buf]),
            out_shape=out_shape,
            compiler_params=_params("arbitrary", "arbitrary"),
            name="diffattn",
        )(chunkcls, table2, qT, kd, vT, strips, *vecs, nw_b)

    def general(_):
        posq = positions.reshape(B, 1, L)
        posk = jnp.broadcast_to(positions[:, :, None], (B, L, 128))
        return pl.pallas_call(
            functools.partial(_diff_kernel, tq=tq, tk=tk, n_kt=nk, n_strips=n_strips),
            grid_spec=pltpu.PrefetchScalarGridSpec(
                num_scalar_prefetch=3, grid=(B, DIFF_HEADS, nq),
                in_specs=[q_spec, k_spec, v_spec, strip_spec,
                          pl.BlockSpec((1, 1, tq), lambda b, h, i, *_: (b, 0, i)),
                          pl.BlockSpec((1, L, 128), lambda b, h, i, *_: (b, 0, 0))] + tail_specs,
                out_specs=out_spec,
                scratch_shapes=[row_buf, row_buf, pltpu.VMEM((DIFF_DV, 2 * tq), F32), s_buf, s_buf,
                                row_buf, row_buf, pltpu.VMEM((tk, tq), F32)]),
            out_shape=out_shape,
            compiler_params=_params("arbitrary", "arbitrary", "arbitrary"),
            name="diffattn_general",
        )(stepcls, chunkcls, table2, qT, kd, vT, strips, posq, posk, *vecs, nw_b)

    return lax.cond(jnp.all(chunkcls != CLS_GENERIC), fast, general, 0)


def _outproj_kernel(of_ref, ob_ref, rg_ref, gnw_ref, od_ref, wa_ref, wb_ref, ga_ref, x_ref, o_ref):
    o = of_ref[...] + ob_ref[...]
    parts = []
    for hd in range(GLA_HEADS):
        oh = o[:, hd * GLA_DV:(hd + 1) * GLA_DV]
        ms = jnp.mean(oh * oh, axis=-1, keepdims=True)
        parts.append(oh * lax.rsqrt(ms + EPS))
    r = rg_ref[...]
    oa = jnp.concatenate(parts, axis=1) * gnw_ref[...] * (r * _sigmoid(r))
    mixed = _dot(oa.astype(BF16), wa_ref[...]) + _dot(od_ref[...], wb_ref[...])
    o_ref[...] = x_ref[...] + ga_ref[0] * mixed


def _outproj(of, ob, rg, gnw, od, wa, wb, ga, x2d, L, tm):
    T, D = x2d.shape
    per_b = L // tm
    row = lambda i: (i, 0)
    full = lambda i: (0, 0)
    bvec = lambda i: (i // per_b, 0, 0)
    return pl.pallas_call(
        _outproj_kernel,
        grid=(T // tm,),
        in_specs=[pl.BlockSpec((tm, GLA_WIDTH), row), pl.BlockSpec((tm, GLA_WIDTH), row),
                  pl.BlockSpec((tm, GLA_WIDTH), row), pl.BlockSpec((1, GLA_WIDTH), full),
                  pl.BlockSpec((tm, DIFF_WIDTH), row),
                  pl.BlockSpec(wa.shape, full), pl.BlockSpec(wb.shape, full),
                  pl.BlockSpec((1, 1, D), bvec), pl.BlockSpec((tm, D), row)],
        out_specs=pl.BlockSpec((tm, D), row),
        out_shape=jax.ShapeDtypeStruct((T, D), F32),
        compiler_params=_params("arbitrary"),
        name="outproj",
    )(of, ob, rg, gnw, od, wa, wb, ga, x2d)


HALO = 8
FF_CHUNK = 256


def _ffn_kernel(x_ref, xp_ref, xn_ref, sh_ref, sc_ref, gf_ref, nw_ref, wup_ref, cw_ref, cb_ref, wdn_ref, fw_ref,
                o_ref, acc_ref, *, tm, per_b):
    i = pl.program_id(0)
    first = (i % per_b) == 0
    last = (i % per_b) == per_b - 1
    nw, sh, sc = nw_ref[...], sh_ref[0], sc_ref[0]
    x = x_ref[...]
    h_mid = _rms_mod(x, nw, sh, sc)
    h_prev = jnp.where(first, 0.0, _rms_mod(xp_ref[...], nw, sh, sc))
    h_next = jnp.where(last, 0.0, _rms_mod(xn_ref[...], nw, sh, sc))
    h = jnp.concatenate([h_prev, h_mid, h_next], axis=0).astype(BF16)
    rows = tm + 2 * HALO

    def conv(u, col):
        w = cw_ref[:, col:col + FF_CHUNK]
        up = pltpu.roll(u, 1, 0)[HALO:HALO + tm]
        un = pltpu.roll(u, rows - 1, 0)[HALO:HALO + tm]
        return (w[0:1] * up + w[1:2] * u[HALO:HALO + tm] + w[2:3] * un) + cb_ref[:, col:col + FF_CHUNK]

    def up(n):
        cg, cv = n * FF_CHUNK, D_FF + n * FF_CHUNK
        return _dot(h, wup_ref[:, cg:cg + FF_CHUNK]), _dot(h, wup_ref[:, cv:cv + FF_CHUNK])

    n_chunks = D_FF // FF_CHUNK
    u_gate, u_val = up(0)
    for n in range(n_chunks):
        cg, cv = n * FF_CHUNK, D_FF + n * FF_CHUNK
        if n + 1 < n_chunks:
            nxt = up(n + 1)
        gate = conv(u_gate, cg)
        val = conv(u_val, cv)
        a = (gate * _sigmoid(gate) * val).astype(BF16)
        part = _dot(a, wdn_ref[cg:cg + FF_CHUNK, :])
        if n == 0:
            acc_ref[...] = part
        else:
            acc_ref[...] += part
        if n + 1 < n_chunks:
            u_gate, u_val = nxt
    y = x + gf_ref[0] * acc_ref[...]
    ms = jnp.mean(y * y, axis=-1, keepdims=True)
    o_ref[...] = y * lax.rsqrt(ms + EPS) * fw_ref[...]


def _ffn(x1, sh, sc, gf, nw, wup, cw, cb, wdn, fw, L, tm):
    T, D = x1.shape
    per_b = L // tm
    hb = tm // HALO
    n_halo = T // HALO
    row = lambda i: (i, 0)
    full = lambda i: (0, 0)
    bvec = lambda i: (i // per_b, 0, 0)
    return pl.pallas_call(
        functools.partial(_ffn_kernel, tm=tm, per_b=per_b),
        grid=(T // tm,),
        in_specs=[pl.BlockSpec((tm, D), row),
                  pl.BlockSpec((HALO, D), lambda i: (jnp.maximum(i * hb - 1, 0), 0)),
                  pl.BlockSpec((HALO, D), lambda i: (jnp.minimum((i + 1) * hb, n_halo - 1), 0)),
                  pl.BlockSpec((1, 1, D), bvec), pl.BlockSpec((1, 1, D), bvec), pl.BlockSpec((1, 1, D), bvec),
                  pl.BlockSpec((1, D), full),
                  pl.BlockSpec(wup.shape, full), pl.BlockSpec(cw.shape, full), pl.BlockSpec(cb.shape, full),
                  pl.BlockSpec(wdn.shape, full), pl.BlockSpec((1, D), full)],
        out_specs=pl.BlockSpec((tm, D), row),
        out_shape=jax.ShapeDtypeStruct((T, D), F32),
        scratch_shapes=[pltpu.VMEM((tm, D), F32)],
        compiler_params=_params("arbitrary"),
        name="ffn",
    )(x1, x1, x1, sh, sc, gf, nw, wup, cw, cb, wdn, fw)


def kernel(x, c, positions, w_ada, b_ada, attn_norm_w, w_in, gla_dec_w_fwd, gla_dec_b_fwd, gla_dec_w_bwd,
           gla_dec_b_bwd, gla_norm_w, diff_lambda_q1, diff_lambda_k1, diff_lambda_q2, diff_lambda_k2,
           diff_norm_w, rel_bias_table, w_out, ffn_norm_w, w_up, conv_w, conv_b, w_down, final_norm_w):
    B, L, D = x.shape
    T = B * L
    tm = min(512, L)
    x2d = x.reshape(T, D)

    mod = _ada(c, w_ada[0], b_ada[0])
    sh_a, sc_a, g_a, sh_f, sc_f, g_f = [m.reshape(B, 1, D) for m in jnp.split(mod, N_MOD, axis=-1)]

    o_lr = 2 * GLA_QK + 2 * GLA_WIDTH
    o_d = o_lr + 2 * GLA_RANK
    w = w_in[0]
    wg = w[:, :o_lr].astype(BF16)
    wlr = jnp.pad(w[:, o_lr:o_d], ((0, 0), (0, 128 - 2 * GLA_RANK))).astype(BF16)
    wd = w[:, o_d:].astype(BF16)
    decw = jnp.zeros((128, 2 * GLA_QK), F32)
    decw = decw.at[:GLA_RANK, :GLA_QK].set(gla_dec_w_fwd[0]).at[GLA_RANK:2 * GLA_RANK, GLA_QK:].set(gla_dec_w_bwd[0])
    decb = jnp.concatenate([gla_dec_b_fwd[0], gla_dec_b_bwd[0]]).reshape(1, 2 * GLA_QK)

    qg, kg, vg, rg, laf, lab, qd, kd, vd = _inproj(
        x2d, sh_a, sc_a, attn_norm_w[0].reshape(1, D), wg, wlr, wd, decw.astype(BF16), decb, L, tm)

    o_f, o_b = _gla(qg, kg, vg, laf, lab, B, L, min(512, L))

    o_d = _diffattn(qd, kd.reshape(B, L, DIFF_WIDTH), vd, positions, rel_bias_table,
                    diff_lambda_q1[0], diff_lambda_k1[0], diff_lambda_q2[0], diff_lambda_k2[0],
                    diff_norm_w[0], min(512, L), min(1024, L // 2))

    wo = w_out[0].astype(BF16)
    x1 = _outproj(o_f, o_b, rg, gla_norm_w[0].reshape(1, GLA_WIDTH), o_d.reshape(T, DIFF_WIDTH),
                  wo[:GLA_WIDTH], wo[GLA_WIDTH:], g_a, x2d, L, tm)

    out = _ffn(x1, sh_f, sc_f, g_f, ffn_norm_w[0].reshape(1, D), w_up[0].astype(BF16), conv_w[0],
               conv_b[0].reshape(1, 2 * D_FF), w_down[0].astype(BF16), final_norm_w.reshape(1, D), L, tm)
    return out.reshape(B, L, D)
```

```python
import functools
import math

import jax
import jax.numpy as jnp
from jax import lax
from jax.experimental import pallas as pl
from jax.experimental.pallas import tpu as pltpu

F32 = jnp.float32
BF16 = jnp.bfloat16

D_MODEL = 1024
GLA_HEADS = 4
GLA_DK = 64
GLA_DV = 128
GLA_RANK = 16
GLA_GATE_TEMP = 16.0
GLA_CHUNK = 64
GLA_QK = GLA_HEADS * GLA_DK
GLA_WIDTH = GLA_HEADS * GLA_DV
DIFF_HEADS = 4
DIFF_DH = 64
DIFF_DV = 2 * DIFF_DH
DIFF_WIDTH = DIFF_HEADS * DIFF_DV
REL_BUCKETS = 32
D_FF = 2816
N_MOD = 6
EPS = 1e-6
LAMBDA_INIT = 0.8 - 0.6 * math.exp(-0.3 * 0)
LOG2E = 1.4426950408889634
ONES_ROWS = 16

BUCKET_STEPS = (1, 2, 3, 4, 5, 6, 7, 8, 12, 16, 23, 32, 46, 64, 91)
FAR_DIST = BUCKET_STEPS[-1]

VMEM_LIMIT = 56 * 1024 * 1024


def _split_bf16(a):
    hi = a.astype(BF16)
    lo = (a - hi.astype(F32)).astype(BF16)
    return hi, lo


def _dot(a, b):
    return jnp.dot(a, b, preferred_element_type=F32)


def _sigmoid(v):
    return 1.0 / (1.0 + jnp.exp(-v))


def _params(*sem):
    return pltpu.CompilerParams(dimension_semantics=sem, vmem_limit_bytes=VMEM_LIMIT)


def _ada_kernel(c_ref, w_ref, b_ref, o_ref):
    c = c_ref[...]
    s_hi, s_lo = _split_bf16(c * _sigmoid(c))
    w_hi, w_lo = _split_bf16(w_ref[...])
    o_ref[...] = (_dot(s_hi, w_hi) + _dot(s_lo, w_hi) + _dot(s_hi, w_lo)) + b_ref[...]


def _ada(c, w, b):
    B, D = c.shape
    N = w.shape[1]
    tn = 1024
    return pl.pallas_call(
        _ada_kernel,
        grid=(N // tn,),
        in_specs=[pl.BlockSpec((B, D), lambda j: (0, 0)),
                  pl.BlockSpec((D, tn), lambda j: (0, j)),
                  pl.BlockSpec((1, tn), lambda j: (0, j))],
        out_specs=pl.BlockSpec((B, tn), lambda j: (0, j)),
        out_shape=jax.ShapeDtypeStruct((B, N), F32),
        compiler_params=_params("arbitrary"),
        name="ada",
    )(c, w, b.reshape(1, N))


def _rms_mod(x, nw, sh, sc):
    ms = jnp.mean(x * x, axis=-1, keepdims=True)
    return (x * lax.rsqrt(ms + EPS) * nw) * (1.0 + sc) + sh


def _inproj_kernel(x_ref, sh_ref, sc_ref, nw_ref, wg_ref, wlr_ref, wd_ref, decw_ref, decb_ref,
                   qg_ref, kg_ref, vg_ref, rg_ref, laf_ref, lab_ref, qd_ref, kd_ref, vd_ref):
    h = _rms_mod(x_ref[...], nw_ref[...], sh_ref[0], sc_ref[0]).astype(BF16)
    g = _dot(h, wg_ref[...])
    qg_ref[...] = g[:, :GLA_QK] * (GLA_DK ** -0.5)
    kg_ref[...] = g[:, GLA_QK:2 * GLA_QK]
    vg_ref[...] = g[:, 2 * GLA_QK:2 * GLA_QK + GLA_WIDTH].astype(BF16)
    rg_ref[...] = g[:, 2 * GLA_QK + GLA_WIDTH:]
    lr = _dot(h, wlr_ref[...])
    z = _dot(lr.astype(BF16), decw_ref[...]) + decb_ref[...]
    la = (jnp.minimum(z, 0.0) - jnp.log(1.0 + jnp.exp(-jnp.abs(z)))) * (1.0 / GLA_GATE_TEMP)
    laf_ref[...] = la[:, :GLA_QK]
    lab_ref[...] = la[:, GLA_QK:]
    d = _dot(h, wd_ref[...])
    qd_ref[0] = (d[:, :DIFF_WIDTH] * (DIFF_DH ** -0.5 * LOG2E)).T.astype(BF16)
    kd_ref[...] = d[:, DIFF_WIDTH:2 * DIFF_WIDTH].astype(BF16)
    vt = d[:, 2 * DIFF_WIDTH:].T.astype(BF16)
    for hd in range(DIFF_HEADS):
        vd_ref[0, hd, :DIFF_DV, :] = vt[hd * DIFF_DV:(hd + 1) * DIFF_DV, :]
    vd_ref[0, :, DIFF_DV:, :] = jnp.ones((DIFF_HEADS, ONES_ROWS, vt.shape[1]), BF16)


def _inproj(x2d, sh, sc, nw, wg, wlr, wd, decw, decb, L, tm):
    T, D = x2d.shape
    per_b = L // tm
    row = lambda i: (i, 0)
    full = lambda i: (0, 0)
    bvec = lambda i: (i // per_b, 0, 0)
    B = T // L
    outs = [(GLA_QK, F32), (GLA_QK, F32), (GLA_WIDTH, BF16), (GLA_WIDTH, F32), (GLA_QK, F32), (GLA_QK, F32)]
    col = lambda i: (i // per_b, 0, i % per_b)
    col4 = lambda i: (i // per_b, 0, 0, i % per_b)
    return pl.pallas_call(
        _inproj_kernel,
        grid=(T // tm,),
        in_specs=[pl.BlockSpec((tm, D), row),
                  pl.BlockSpec((1, 1, D), bvec), pl.BlockSpec((1, 1, D), bvec),
                  pl.BlockSpec((1, D), full),
                  pl.BlockSpec(wg.shape, full), pl.BlockSpec(wlr.shape, full), pl.BlockSpec(wd.shape, full),
                  pl.BlockSpec(decw.shape, full), pl.BlockSpec(decb.shape, full)],
        out_specs=[pl.BlockSpec((tm, n), row) for n, _ in outs] + [
            pl.BlockSpec((1, DIFF_WIDTH, tm), col), pl.BlockSpec((tm, DIFF_WIDTH), row),
            pl.BlockSpec((1, DIFF_HEADS, DIFF_DV + ONES_ROWS, tm), col4)],
        out_shape=[jax.ShapeDtypeStruct((T, n), dt) for n, dt in outs] + [
            jax.ShapeDtypeStruct((B, DIFF_WIDTH, L), BF16), jax.ShapeDtypeStruct((T, DIFF_WIDTH), BF16),
            jax.ShapeDtypeStruct((B, DIFF_HEADS, DIFF_DV + ONES_ROWS, L), BF16)],
        compiler_params=_params("arbitrary"),
        name="inproj",
    )(x2d, sh, sc, nw, wg, wlr, wd, decw, decb)


def _gla_kernel(qf_ref, kf_ref, vf_ref, laf_ref, qb_ref, kb_ref, vb_ref, lab_ref,
                of_ref, ob_ref, sf_ref, sb_ref, *, n_chunks, group):
    C = GLA_CHUNK

    @pl.when(pl.program_id(1) == 0)
    def _():
        sf_ref[...] = jnp.zeros_like(sf_ref)
        sb_ref[...] = jnp.zeros_like(sb_ref)

    r = lax.broadcasted_iota(jnp.int32, (C, C), 0)
    s = lax.broadcasted_iota(jnp.int32, (C, C), 1)
    tri_f = jnp.where(s <= r, 1.0, 0.0).astype(BF16)
    tri_b = jnp.where(s >= r, 1.0, 0.0).astype(BF16)
    r4 = lax.broadcasted_iota(jnp.int32, (GLA_HEADS * C, C), 0) % C
    s4 = lax.broadcasted_iota(jnp.int32, (GLA_HEADS * C, C), 1)
    mask_f = s4 <= r4
    mask_b = s4 > r4
    lane = lax.broadcasted_iota(jnp.int32, (C, GLA_QK), 1)
    head_masks = [(lane >= hd * GLA_DK) & (lane < (hd + 1) * GLA_DK) for hd in range(GLA_HEADS)]
    nt = (((1,), (1,)), ((), ()))

    def per_head(x):
        zero = jnp.zeros_like(x)
        return [jnp.where(mk, x, zero) for mk in head_masks]

    dirs = ((qf_ref, kf_ref, vf_ref, laf_ref, of_ref, sf_ref, tri_f, mask_f, C - 1, False),
            (qb_ref, kb_ref, vb_ref, lab_ref, ob_ref, sb_ref, tri_b, mask_b, 0, True))

    def run_group(g):
        work = []
        for (q_ref, k_ref, v_ref, la_ref, o_ref, s_ref, tri, att_mask, last_row, rev) in dirs:
            for i in range(group):
                c = g * group + i
                c = (n_chunks - 1 - c) if rev else c
                work.append(dict(rows=pl.ds(pl.multiple_of(c * C, C), C), q_ref=q_ref, k_ref=k_ref, v_ref=v_ref,
                                 la_ref=la_ref, o_ref=o_ref, s_ref=s_ref, tri=tri, att_mask=att_mask,
                                 last_row=last_row))
        for w in work:
            la_hi, la_lo = _split_bf16(w["la_ref"][w["rows"], :])
            w["b"] = _dot(w["tri"], la_hi) + _dot(w["tri"], la_lo)
        for w in work:
            b = w["b"]
            tot = b[w["last_row"]:w["last_row"] + 1, :]
            q = w["q_ref"][w["rows"], :]
            k = w["k_ref"][w["rows"], :]
            w["dec"] = jnp.exp(tot)
            w["qs"] = jnp.concatenate(per_head((q * jnp.exp(b)).astype(BF16)), axis=0)
            w["k_in"] = (k * jnp.exp(-b)).astype(BF16)
            w["k_st"] = jnp.concatenate(per_head((k * jnp.exp(tot - b)).astype(BF16)), axis=0)
            v = w["v_ref"][w["rows"], :]
            w["v"] = v
            v_rows = jnp.concatenate([v[:, hd * GLA_DV:(hd + 1) * GLA_DV] for hd in range(GLA_HEADS)], axis=0)
            w["vt"] = v_rows.astype(F32).T.astype(BF16)
        for w in work:
            att = lax.dot_general(w["qs"], w["k_in"], nt, preferred_element_type=F32)
            w["att"] = jnp.where(w["att_mask"], att, 0.0).astype(BF16)
        for w in work:
            w["o"] = jnp.concatenate(
                [_dot(w["att"][hd * C:(hd + 1) * C, :], w["v"][:, hd * GLA_DV:(hd + 1) * GLA_DV])
                 for hd in range(GLA_HEADS)], axis=1)
        for w in work:
            w["upd"] = _dot(w["vt"], w["k_st"])
        for d in range(2):
            s_ref = dirs[d][5]
            st = s_ref[...]
            for w in work[d * group:(d + 1) * group]:
                oi = lax.dot_general(w["qs"], st.astype(BF16), nt, preferred_element_type=F32)
                oi = jnp.concatenate([oi[hd * C:(hd + 1) * C, :] for hd in range(GLA_HEADS)], axis=1)
                w["o_ref"][w["rows"], :] = w["o"] + oi
                st = st * w["dec"] + w["upd"]
            s_ref[...] = st

    n_groups = n_chunks // group
    if n_groups == 1:
        run_group(0)
    else:
        def body(g, carry):
            run_group(g)
            return carry
        lax.fori_loop(0, n_groups, body, 0)


def _gla(qg, kg, vg, laf, lab, B, L, tb):
    T = B * L
    J = L // tb
    fwd = lambda b, j: (b * J + j, 0)
    bwd = lambda b, j: (b * J + (J - 1 - j), 0)
    qk = lambda im: pl.BlockSpec((tb, GLA_QK), im)
    vv = lambda im: pl.BlockSpec((tb, GLA_WIDTH), im)
    return pl.pallas_call(
        functools.partial(_gla_kernel, n_chunks=tb // GLA_CHUNK, group=min(4, tb // GLA_CHUNK)),
        grid=(B, J),
        in_specs=[qk(fwd), qk(fwd), vv(fwd), qk(fwd), qk(bwd), qk(bwd), vv(bwd), qk(bwd)],
        out_specs=[vv(fwd), vv(bwd)],
        out_shape=[jax.ShapeDtypeStruct((T, GLA_WIDTH), F32)] * 2,
        scratch_shapes=[pltpu.VMEM((GLA_DV, GLA_QK), F32)] * 2,
        compiler_params=_params("arbitrary", "arbitrary"),
        name="gla",
    )(qg, kg, vg, laf, qg, kg, vg, lab)


def _bias_from_rel(rel, tab_ref, hd):
    n = jnp.abs(rel)
    half = REL_BUCKETS // 2
    neg = jnp.full(rel.shape, tab_ref[0, hd], F32)
    pos = jnp.full(rel.shape, tab_ref[half, hd], F32)
    for j, t in enumerate(BUCKET_STEPS):
        ge = n >= t
        neg = jnp.where(ge, tab_ref[j + 1, hd], neg)
        pos = jnp.where(ge, tab_ref[half + j + 1, hd], pos)
    return jnp.where(rel > 0, pos, neg)


KEY_CHUNK = 128
COL_TILE = 256
CLS_LEFT, CLS_RIGHT, CLS_STRIP0, CLS_GENERIC = 0, 1, 2, 99
STEP_LEFT, STEP_RIGHT, STEP_MIXED = 0, 1, 2


def _strips_kernel(tab_ref, o_ref, *, tq, n_strips):
    hd = pl.program_id(0)
    kk = lax.broadcasted_iota(jnp.int32, (KEY_CHUNK, tq), 0)
    qq = lax.broadcasted_iota(jnp.int32, (KEY_CHUNK, tq), 1)
    o_ref[0, CLS_LEFT] = jnp.full((KEY_CHUNK, tq), tab_ref[REL_BUCKETS // 2 - 1, hd], F32)
    o_ref[0, CLS_RIGHT] = jnp.full((KEY_CHUNK, tq), tab_ref[REL_BUCKETS - 1, hd], F32)
    for j in range(n_strips):
        o_ref[0, CLS_STRIP0 + j] = _bias_from_rel(KEY_CHUNK * (j - 1) + kk - qq, tab_ref, hd)


def _strips(table, tq, n_strips):
    H = table.shape[1]
    return pl.pallas_call(
        functools.partial(_strips_kernel, tq=tq, n_strips=n_strips),
        grid_spec=pltpu.PrefetchScalarGridSpec(
            num_scalar_prefetch=1, grid=(H,), in_specs=[],
            out_specs=pl.BlockSpec((1, CLS_STRIP0 + n_strips, KEY_CHUNK, tq), lambda h, *_: (h, 0, 0, 0))),
        out_shape=jax.ShapeDtypeStruct((H, CLS_STRIP0 + n_strips, KEY_CHUNK, tq), F32),
        compiler_params=_params("arbitrary"),
        name="bias_strips",
    )(table)


def _diff_finish(acc, l, lq1_ref, lk1_ref, lq2_ref, lk2_ref, nw_ref, o_ref, tq):
    lam = (jnp.exp(jnp.sum(lq1_ref[...] * lk1_ref[...], axis=1, keepdims=True))
           - jnp.exp(jnp.sum(lq2_ref[...] * lk2_ref[...], axis=1, keepdims=True)) + LAMBDA_INIT)
    o = acc[:, :tq] / l[:, :tq] - lam * (acc[:, tq:] / l[:, tq:])
    ms = jnp.mean(o * o, axis=0, keepdims=True)
    nw = jnp.concatenate([nw_ref[...]] * (tq // 128), axis=1)
    o = o * lax.rsqrt(ms + EPS) * nw * (1.0 - LAMBDA_INIT)
    o_ref[0] = o.T.astype(o_ref.dtype)


def _stack_maps(qT):
    row = lax.broadcasted_iota(jnp.int32, qT.shape, 0)
    zero = jnp.zeros_like(qT)
    return jnp.concatenate([jnp.where(row < DIFF_DH, qT, zero), jnp.where(row >= DIFF_DH, qT, zero)], axis=1)


def _diff_fast_kernel(chunkcls_ref, tab_ref,
                      qT_ref, k_ref, vT_ref, strip_ref, lq1_ref, lk1_ref, lq2_ref, lk2_ref, nw_ref,
                      o_ref, qs_ref, m_ref, acc_ref, sa_ref, sb_ref, smaxa_ref, smaxb_ref,
                      *, tq, tk, n_kt, n_qt):
    hd = pl.program_id(1)
    r0 = pl.program_id(0) * n_qt
    n_ct = 2 * tq // COL_TILE
    n_chunk = tk // KEY_CHUNK
    for qt in range(n_qt):
        qs_ref[qt] = _stack_maps(qT_ref[0, :, qt * tq:(qt + 1) * tq])
    bias_max = functools.reduce(jnp.maximum, [tab_ref[i, hd] for i in range(REL_BUCKETS)])

    def scores_tile(qt, kt, buf, j):
        s_ref, smax_ref = buf
        kblk = k_ref[0, pl.ds(pl.multiple_of(kt * tk, tk), tk), :]
        cols = slice(j * COL_TILE, (j + 1) * COL_TILE)
        s = _dot(kblk, qs_ref[qt, :, cols])
        s_ref[:, cols] = s
        smax_ref[:, cols] = jnp.max(s, axis=0, keepdims=True)

    def chain(qt, kt, buf, j):
        s_ref, smax_ref = buf
        cols = slice(j * COL_TILE, (j + 1) * COL_TILE)
        q0 = (j * COL_TILE) % tq
        m_old = m_ref[:, cols]
        m_new = jnp.maximum(m_old, smax_ref[:, cols] + bias_max)
        alpha = jnp.exp2(m_old - m_new)
        ps = []
        for c in range(n_chunk):
            rows = slice(c * KEY_CHUNK, (c + 1) * KEY_CHUNK)
            bias = strip_ref[0, chunkcls_ref[r0 + qt, kt * n_chunk + c], :, q0:q0 + COL_TILE]
            ps.append(jnp.exp2((s_ref[rows, cols] + bias) - m_new).astype(BF16))
        vblk = vT_ref[0, 0, :, pl.ds(pl.multiple_of(kt * tk, tk), tk)]
        acc_ref[:, cols] = alpha * acc_ref[:, cols] + _dot(vblk, jnp.concatenate(ps, axis=0))
        m_ref[:, cols] = m_new

    def step(qt, kt, cur, nxt, nxt_tile):
        for j in range(n_ct):
            if nxt_tile is not None:
                scores_tile(nxt_tile[0], nxt_tile[1], nxt, j)
            chain(qt, kt, cur, j)

    buf_a = (sa_ref, smaxa_ref)
    buf_b = (sb_ref, smaxb_ref)

    def query_tile(qt, handoff):
        m_ref[...] = jnp.full(m_ref.shape, -1e30, F32)
        acc_ref[...] = jnp.zeros(acc_ref.shape, F32)

        for kt in range(n_kt):
            cur, nxt = (buf_a, buf_b) if kt % 2 == 0 else (buf_b, buf_a)
            if kt + 1 < n_kt:
                nxt_tile = (qt, kt + 1)
            else:
                nxt_tile = (qt + 1, 0) if handoff else None
            step(qt, kt, cur, nxt, nxt_tile)
        acc = acc_ref[...]
        _diff_finish(acc[:DIFF_DV], acc[DIFF_DV:DIFF_DV + 1], lq1_ref, lk1_ref, lq2_ref, lk2_ref, nw_ref,
                     o_ref.at[:, pl.ds(pl.multiple_of(qt * tq, tq), tq), :], tq)

    for j in range(n_ct):
        scores_tile(0, 0, buf_a, j)

    def body(qt, carry):
        query_tile(qt, True)
        return carry

    lax.fori_loop(0, n_qt - 1, body, 0)
    query_tile(n_qt - 1, False)


def _diff_kernel(stepcls_ref, chunkcls_ref, tab_ref,
                 qT_ref, k_ref, vT_ref, strip_ref, posq_ref, posk_ref, lq1_ref, lk1_ref, lq2_ref, lk2_ref, nw_ref,
                 o_ref, m_ref, l_ref, acc_ref, sa_ref, sb_ref, smaxa_ref, smaxb_ref, bias_ref,
                 *, tq, tk, n_kt, n_strips):
    hd = pl.program_id(1)
    r = pl.program_id(0) * pl.num_programs(2) + pl.program_id(2)
    half = REL_BUCKETS // 2
    n_ct = 2 * tq // COL_TILE
    n_chunk = tk // KEY_CHUNK

    qstack = _stack_maps(qT_ref[0])

    m_ref[...] = jnp.full(m_ref.shape, -1e30, F32)
    l_ref[...] = jnp.zeros(l_ref.shape, F32)
    acc_ref[...] = jnp.zeros(acc_ref.shape, F32)

    bias_max = functools.reduce(jnp.maximum, [tab_ref[i, hd] for i in range(REL_BUCKETS)])

    def scores_tile(kt, buf, j):
        s_ref, smax_ref = buf
        kblk = k_ref[0, pl.ds(pl.multiple_of(kt * tk, tk), tk), :]
        cols = slice(j * COL_TILE, (j + 1) * COL_TILE)
        s = _dot(kblk, qstack[:, cols])
        s_ref[:, cols] = s
        smax_ref[:, cols] = jnp.max(s, axis=0, keepdims=True)

    def scores(kt, buf):
        for j in range(n_ct):
            scores_tile(kt, buf, j)

    def build_bias(kt):
        posq = posq_ref[0]
        for c in range(n_chunk):
            cc = chunkcls_ref[r, kt * n_chunk + c]
            rows = slice(c * KEY_CHUNK, (c + 1) * KEY_CHUNK)

            @pl.when(cc == CLS_LEFT)
            def _():
                bias_ref[rows, :] = jnp.full((KEY_CHUNK, tq), tab_ref[half - 1, hd], F32)

            @pl.when(cc == CLS_RIGHT)
            def _():
                bias_ref[rows, :] = jnp.full((KEY_CHUNK, tq), tab_ref[REL_BUCKETS - 1, hd], F32)

            @pl.when(jnp.logical_and(cc >= CLS_STRIP0, cc < CLS_STRIP0 + n_strips))
            def _():
                bias_ref[rows, :] = strip_ref[0, cc]

            @pl.when(cc == CLS_GENERIC)
            def _():
                k0 = pl.multiple_of(kt * tk + c * KEY_CHUNK, KEY_CHUNK)
                pk = posk_ref[0, pl.ds(k0, KEY_CHUNK), :]
                pk = jnp.concatenate([pk] * (tq // 128), axis=1)
                bias_ref[rows, :] = _bias_from_rel(pk - posq, tab_ref, hd)

    def softmax_pv(kt, cur, const, nxt):
        s_ref, smax_ref = cur
        vblk = vT_ref[0, 0, :DIFF_DV, pl.ds(pl.multiple_of(kt * tk, tk), tk)]
        for j in range(n_ct):
            if nxt is not None:
                scores_tile(kt + 1, nxt, j)
            cols = slice(j * COL_TILE, (j + 1) * COL_TILE)
            shift = bias_max if const is None else const
            m_old = m_ref[:, cols]
            m_new = jnp.maximum(m_old, smax_ref[:, cols] + shift)
            alpha = jnp.exp2(m_old - m_new)
            if const is None:
                q0 = (j * COL_TILE) % tq
                p = jnp.exp2((s_ref[:, cols] + bias_ref[:, q0:q0 + COL_TILE]) - m_new)
            else:
                p = jnp.exp2(s_ref[:, cols] - (m_new - shift))
            l_ref[:, cols] = alpha * l_ref[:, cols] + jnp.sum(p, axis=0, keepdims=True)
            acc_ref[:, cols] = alpha * acc_ref[:, cols] + _dot(vblk, p.astype(BF16))
            m_ref[:, cols] = m_new

    def step(kt, cur, nxt):
        cls = stepcls_ref[r, kt]

        @pl.when(cls == STEP_MIXED)
        def _():
            build_bias(kt)
            softmax_pv(kt, cur, None, nxt)

        @pl.when(cls != STEP_MIXED)
        def _():
            const = jnp.where(cls == STEP_LEFT, tab_ref[half - 1, hd], tab_ref[REL_BUCKETS - 1, hd])
            softmax_pv(kt, cur, const, nxt)

    buf_a = (sa_ref, smaxa_ref)
    buf_b = (sb_ref, smaxb_ref)
    scores(0, buf_a)

    def pair(kk, carry):
        step(2 * kk, buf_a, buf_b)
        step(2 * kk + 1, buf_b, buf_a)
        return carry

    lax.fori_loop(0, n_kt // 2 - 1, pair, 0)
    step(n_kt - 2, buf_a, buf_b)
    step(n_kt - 1, buf_b, None)

    _diff_finish(acc_ref[...], l_ref[...], lq1_ref, lk1_ref, lq2_ref, lk2_ref, nw_ref, o_ref, tq)


def _classify(positions, tq, tk, n_strips):
    B, L = positions.shape
    nq, nc = L // tq, L // KEY_CHUNK
    pq = positions.reshape(B, nq, tq)
    pk = positions.reshape(B, nc, KEY_CHUNK)
    qlo, qhi, q0 = pq.min(axis=2), pq.max(axis=2), pq[:, :, 0]
    klo, khi, k0 = pk.min(axis=2), pk.max(axis=2), pk[:, :, 0]
    q_run = jnp.all(pq == q0[:, :, None] + jnp.arange(tq, dtype=positions.dtype), axis=2)
    k_run = jnp.all(pk == k0[:, :, None] + jnp.arange(KEY_CHUNK, dtype=positions.dtype), axis=2)
    left = (qlo[:, :, None] - khi[:, None, :]) >= FAR_DIST
    right = (klo[:, None, :] - qhi[:, :, None]) >= FAR_DIST
    d = k0[:, None, :] - q0[:, :, None]
    j = d // KEY_CHUNK + 1
    strip_ok = (q_run[:, :, None] & k_run[:, None, :] & (d % KEY_CHUNK == 0) & (j >= 0) & (j < n_strips))
    cls = jnp.where(left, CLS_LEFT, jnp.where(right, CLS_RIGHT, jnp.where(strip_ok, CLS_STRIP0 + j, CLS_GENERIC)))
    cls = cls.astype(jnp.int32)
    per_step = cls.reshape(B, nq, L // tk, tk // KEY_CHUNK)
    step = jnp.where(jnp.all(per_step == CLS_LEFT, axis=3), STEP_LEFT,
                     jnp.where(jnp.all(per_step == CLS_RIGHT, axis=3), STEP_RIGHT, STEP_MIXED)).astype(jnp.int32)
    return step.reshape(B * nq, L // tk), cls.reshape(B * nq, nc)


def _diffattn(qT, kd, vT, positions, table, lq1, lk1, lq2, lk2, nw, tq, tk):
    B, _, L = qT.shape
    nq, nk = L // tq, L // tk
    n_strips = tq // KEY_CHUNK + 2
    n_cls = CLS_STRIP0 + n_strips
    table2 = table * LOG2E
    strips = _strips(table2, tq, n_strips)
    stepcls, chunkcls = _classify(positions, tq, tk, n_strips)
    nw_b = jnp.broadcast_to(nw.reshape(DIFF_DV, 1), (DIFF_DV, 128))
    vecs = [a.reshape(1, DIFF_DH) for a in (lq1, lk1, lq2, lk2)]
    full2 = lambda b, h, i, *_: (0, 0)
    q_spec = pl.BlockSpec((1, DIFF_DV, tq), lambda b, h, i, *_: (b, h, i))
    k_spec = pl.BlockSpec((1, L, DIFF_DV), lambda b, h, i, *_: (b, 0, h))
    v_spec = pl.BlockSpec((1, 1, DIFF_DV + ONES_ROWS, L), lambda b, h, i, *_: (b, h, 0, 0))
    strip_spec = pl.BlockSpec((1, n_cls, KEY_CHUNK, tq), lambda b, h, i, *_: (h, 0, 0, 0))
    tail_specs = [pl.BlockSpec((1, DIFF_DH), full2)] * 4 + [pl.BlockSpec((DIFF_DV, 128), full2)]
    out_spec = pl.BlockSpec((1, tq, DIFF_DV), lambda b, h, i, *_: (b, i, h))
    out_shape = jax.ShapeDtypeStruct((B, L, DIFF_WIDTH), BF16)
    s_buf = pltpu.VMEM((tk, 2 * tq), F32)
    row_buf = pltpu.VMEM((1, 2 * tq), F32)

    def fast(_):
        one = lambda b, h, *_: (0, 0)
        return pl.pallas_call(
            functools.partial(_diff_fast_kernel, tq=tq, tk=tk, n_kt=nk, n_qt=nq),
            grid_spec=pltpu.PrefetchScalarGridSpec(
                num_scalar_prefetch=2, grid=(B, DIFF_HEADS),
                in_specs=[pl.BlockSpec((1, DIFF_DV, L), lambda b, h, *_: (b, h, 0)),
                          pl.BlockSpec((1, L, DIFF_DV), lambda b, h, *_: (b, 0, h)),
                          pl.BlockSpec((1, 1, DIFF_DV + ONES_ROWS, L), lambda b, h, *_: (b, h, 0, 0)),
                          pl.BlockSpec((1, n_cls, KEY_CHUNK, tq), lambda b, h, *_: (h, 0, 0, 0))]
                + [pl.BlockSpec((1, DIFF_DH), one)] * 4 + [pl.BlockSpec((DIFF_DV, 128), one)],
                out_specs=pl.BlockSpec((1, L, DIFF_DV), lambda b, h, *_: (b, 0, h)),
                scratch_shapes=[pltpu.VMEM((nq, DIFF_DV, 2 * tq), BF16),
                                row_buf, pltpu.VMEM((DIFF_DV + ONES_ROWS, 2 * tq), F32),
                                s_buf, s_buf, row_buf, row_buf]),
            out_shape=out_shape,
            compiler_params=_params("arbitrary", "arbitrary"),
            name="diffattn",
        )(chunkcls, table2, qT, kd, vT, strips, *vecs, nw_b)

    def general(_):
        posq = positions.reshape(B, 1, L)
        posk = jnp.broadcast_to(positions[:, :, None], (B, L, 128))
        return pl.pallas_call(
            functools.partial(_diff_kernel, tq=tq, tk=tk, n_kt=nk, n_strips=n_strips),
            grid_spec=pltpu.PrefetchScalarGridSpec(
                num_scalar_prefetch=3, grid=(B, DIFF_HEADS, nq),
                in_specs=[q_spec, k_spec, v_spec, strip_spec,
                          pl.BlockSpec((1, 1, tq), lambda b, h, i, *_: (b, 0, i)),
                          pl.BlockSpec((1, L, 128), lambda b, h, i, *_: (b, 0, 0))] + tail_specs,
                out_specs=out_spec,
                scratch_shapes=[row_buf, row_buf, pltpu.VMEM((DIFF_DV, 2 * tq), F32), s_buf, s_buf,
                                row_buf, row_buf, pltpu.VMEM((tk, tq), F32)]),
            out_shape=out_shape,
            compiler_params=_params("arbitrary", "arbitrary", "arbitrary"),
            name="diffattn_general",
        )(stepcls, chunkcls, table2, qT, kd, vT, strips, posq, posk, *vecs, nw_b)

    return lax.cond(jnp.all(chunkcls != CLS_GENERIC), fast, general, 0)


def _outproj_kernel(of_ref, ob_ref, rg_ref, gnw_ref, od_ref, wa_ref, wb_ref, ga_ref, x_ref, o_ref):
    o = of_ref[...] + ob_ref[...]
    parts = []
    for hd in range(GLA_HEADS):
        oh = o[:, hd * GLA_DV:(hd + 1) * GLA_DV]
        ms = jnp.mean(oh * oh, axis=-1, keepdims=True)
        parts.append(oh * lax.rsqrt(ms + EPS))
    r = rg_ref[...]
    oa = jnp.concatenate(parts, axis=1) * gnw_ref[...] * (r * _sigmoid(r))
    mixed = _dot(oa.astype(BF16), wa_ref[...]) + _dot(od_ref[...], wb_ref[...])
    o_ref[...] = x_ref[...] + ga_ref[0] * mixed


def _outproj(of, ob, rg, gnw, od, wa, wb, ga, x2d, L, tm):
    T, D = x2d.shape
    per_b = L // tm
    row = lambda i: (i, 0)
    full = lambda i: (0, 0)
    bvec = lambda i: (i // per_b, 0, 0)
    return pl.pallas_call(
        _outproj_kernel,
        grid=(T // tm,),
        in_specs=[pl.BlockSpec((tm, GLA_WIDTH), row), pl.BlockSpec((tm, GLA_WIDTH), row),
                  pl.BlockSpec((tm, GLA_WIDTH), row), pl.BlockSpec((1, GLA_WIDTH), full),
                  pl.BlockSpec((tm, DIFF_WIDTH), row),
                  pl.BlockSpec(wa.shape, full), pl.BlockSpec(wb.shape, full),
                  pl.BlockSpec((1, 1, D), bvec), pl.BlockSpec((tm, D), row)],
        out_specs=pl.BlockSpec((tm, D), row),
        out_shape=jax.ShapeDtypeStruct((T, D), F32),
        compiler_params=_params("arbitrary"),
        name="outproj",
    )(of, ob, rg, gnw, od, wa, wb, ga, x2d)


HALO = 8
FF_CHUNK = 256


def _ffn_kernel(x_ref, xp_ref, xn_ref, sh_ref, sc_ref, gf_ref, nw_ref, wup_ref, cw_ref, cb_ref, wdn_ref, fw_ref,
                o_ref, acc_ref, *, tm, per_b):
    i = pl.program_id(0)
    first = (i % per_b) == 0
    last = (i % per_b) == per_b - 1
    nw, sh, sc = nw_ref[...], sh_ref[0], sc_ref[0]
    x = x_ref[...]
    h_mid = _rms_mod(x, nw, sh, sc)
    h_prev = jnp.where(first, 0.0, _rms_mod(xp_ref[...], nw, sh, sc))
    h_next = jnp.where(last, 0.0, _rms_mod(xn_ref[...], nw, sh, sc))
    h = jnp.concatenate([h_prev, h_mid, h_next], axis=0).astype(BF16)
    rows = tm + 2 * HALO

    def conv(u, col):
        w = cw_ref[:, col:col + FF_CHUNK]
        up = pltpu.roll(u, 1, 0)[HALO:HALO + tm]
        un = pltpu.roll(u, rows - 1, 0)[HALO:HALO + tm]
        return (w[0:1] * up + w[1:2] * u[HALO:HALO + tm] + w[2:3] * un) + cb_ref[:, col:col + FF_CHUNK]

    def up(n):
        cg, cv = n * FF_CHUNK, D_FF + n * FF_CHUNK
        return _dot(h, wup_ref[:, cg:cg + FF_CHUNK]), _dot(h, wup_ref[:, cv:cv + FF_CHUNK])

    n_chunks = D_FF // FF_CHUNK
    u_gate, u_val = up(0)
    for n in range(n_chunks):
        cg, cv = n * FF_CHUNK, D_FF + n * FF_CHUNK
        if n + 1 < n_chunks:
            nxt = up(n + 1)
        gate = conv(u_gate, cg)
        val = conv(u_val, cv)
        a = (gate * _sigmoid(gate) * val).astype(BF16)
        part = _dot(a, wdn_ref[cg:cg + FF_CHUNK, :])
        if n == 0:
            acc_ref[...] = part
        else:
            acc_ref[...] += part
        if n + 1 < n_chunks:
            u_gate, u_val = nxt
    y = x + gf_ref[0] * acc_ref[...]
    ms = jnp.mean(y * y, axis=-1, keepdims=True)
    o_ref[...] = y * lax.rsqrt(ms + EPS) * fw_ref[...]


def _ffn(x1, sh, sc, gf, nw, wup, cw, cb, wdn, fw, L, tm):
    T, D = x1.shape
    per_b = L // tm
    hb = tm // HALO
    n_halo = T // HALO
    row = lambda i: (i, 0)
    full = lambda i: (0, 0)
    bvec = lambda i: (i // per_b, 0, 0)
    return pl.pallas_call(
        functools.partial(_ffn_kernel, tm=tm, per_b=per_b),
        grid=(T // tm,),
        in_specs=[pl.BlockSpec((tm, D), row),
                  pl.BlockSpec((HALO, D), lambda i: (jnp.maximum(i * hb - 1, 0), 0)),
                  pl.BlockSpec((HALO, D), lambda i: (jnp.minimum((i + 1) * hb, n_halo - 1), 0)),
                  pl.BlockSpec((1, 1, D), bvec), pl.BlockSpec((1, 1, D), bvec), pl.BlockSpec((1, 1, D), bvec),
                  pl.BlockSpec((1, D), full),
                  pl.BlockSpec(wup.shape, full), pl.BlockSpec(cw.shape, full), pl.BlockSpec(cb.shape, full),
                  pl.BlockSpec(wdn.shape, full), pl.BlockSpec((1, D), full)],
        out_specs=pl.BlockSpec((tm, D), row),
        out_shape=jax.ShapeDtypeStruct((T, D), F32),
        scratch_shapes=[pltpu.VMEM((tm, D), F32)],
        compiler_params=_params("arbitrary"),
        name="ffn",
    )(x1, x1, x1, sh, sc, gf, nw, wup, cw, cb, wdn, fw)


def kernel(x, c, positions, w_ada, b_ada, attn_norm_w, w_in, gla_dec_w_fwd, gla_dec_b_fwd, gla_dec_w_bwd,
           gla_dec_b_bwd, gla_norm_w, diff_lambda_q1, diff_lambda_k1, diff_lambda_q2, diff_lambda_k2,
           diff_norm_w, rel_bias_table, w_out, ffn_norm_w, w_up, conv_w, conv_b, w_down, final_norm_w):
    B, L, D = x.shape
    T = B * L
    tm = min(512, L)
    x2d = x.reshape(T, D)

    mod = _ada(c, w_ada[0], b_ada[0])
    sh_a, sc_a, g_a, sh_f, sc_f, g_f = [m.reshape(B, 1, D) for m in jnp.split(mod, N_MOD, axis=-1)]

    o_lr = 2 * GLA_QK + 2 * GLA_WIDTH
    o_d = o_lr + 2 * GLA_RANK
    w = w_in[0]
    wg = w[:, :o_lr].astype(BF16)
    wlr = jnp.pad(w[:, o_lr:o_d], ((0, 0), (0, 128 - 2 * GLA_RANK))).astype(BF16)
    wd = w[:, o_d:].astype(BF16)
    decw = jnp.zeros((128, 2 * GLA_QK), F32)
    decw = decw.at[:GLA_RANK, :GLA_QK].set(gla_dec_w_fwd[0]).at[GLA_RANK:2 * GLA_RANK, GLA_QK:].set(gla_dec_w_bwd[0])
    decb = jnp.concatenate([gla_dec_b_fwd[0], gla_dec_b_bwd[0]]).reshape(1, 2 * GLA_QK)

    qg, kg, vg, rg, laf, lab, qd, kd, vd = _inproj(
        x2d, sh_a, sc_a, attn_norm_w[0].reshape(1, D), wg, wlr, wd, decw.astype(BF16), decb, L, tm)

    o_f, o_b = _gla(qg, kg, vg, laf, lab, B, L, min(512, L))

    o_d = _diffattn(qd, kd.reshape(B, L, DIFF_WIDTH), vd, positions, rel_bias_table,
                    diff_lambda_q1[0], diff_lambda_k1[0], diff_lambda_q2[0], diff_lambda_k2[0],
                    diff_norm_w[0], min(512, L), min(1024, L // 2))

    wo = w_out[0].astype(BF16)
    x1 = _outproj(o_f, o_b, rg, gla_norm_w[0].reshape(1, GLA_WIDTH), o_d.reshape(T, DIFF_WIDTH),
                  wo[:GLA_WIDTH], wo[GLA_WIDTH:], g_a, x2d, L, tm)

    out = _ffn(x1, sh_f, sc_f, g_f, ffn_norm_w[0].reshape(1, D), w_up[0].astype(BF16), conv_w[0],
               conv_b[0].reshape(1, 2 * D_FF), w_down[0].astype(BF16), final_norm_w.reshape(1, D), L, tm)
    return out.reshape(B, L, D)
```

```python
import functools
import math

import jax
import jax.numpy as jnp
from jax import lax
from jax.experimental import pallas as pl
from jax.experimental.pallas import tpu as pltpu

F32 = jnp.float32
BF16 = jnp.bfloat16

D_MODEL = 1024
GLA_HEADS = 4
GLA_DK = 64
GLA_DV = 128
GLA_RANK = 16
GLA_GATE_TEMP = 16.0
GLA_CHUNK = 64
GLA_QK = GLA_HEADS * GLA_DK
GLA_WIDTH = GLA_HEADS * GLA_DV
DIFF_HEADS = 4
DIFF_DH = 64
DIFF_DV = 2 * DIFF_DH
DIFF_WIDTH = DIFF_HEADS * DIFF_DV
REL_BUCKETS = 32
D_FF = 2816
N_MOD = 6
EPS = 1e-6
LAMBDA_INIT = 0.8 - 0.6 * math.exp(-0.3 * 0)
LOG2E = 1.4426950408889634
ONES_ROWS = 16

BUCKET_STEPS = (1, 2, 3, 4, 5, 6, 7, 8, 12, 16, 23, 32, 46, 64, 91)
FAR_DIST = BUCKET_STEPS[-1]

VMEM_LIMIT = 56 * 1024 * 1024


def _split_bf16(a):
    hi = a.astype(BF16)
    lo = (a - hi.astype(F32)).astype(BF16)
    return hi, lo


def _dot(a, b):
    return jnp.dot(a, b, preferred_element_type=F32)


def _sigmoid(v):
    return 1.0 / (1.0 + jnp.exp(-v))


def _params(*sem):
    return pltpu.CompilerParams(dimension_semantics=sem, vmem_limit_bytes=VMEM_LIMIT)


def _ada_kernel(c_ref, w_ref, b_ref, o_ref):
    c = c_ref[...]
    s_hi, s_lo = _split_bf16(c * _sigmoid(c))
    w_hi, w_lo = _split_bf16(w_ref[...])
    o_ref[...] = (_dot(s_hi, w_hi) + _dot(s_lo, w_hi) + _dot(s_hi, w_lo)) + b_ref[...]


def _ada(c, w, b):
    B, D = c.shape
    N = w.shape[1]
    tn = 1024
    return pl.pallas_call(
        _ada_kernel,
        grid=(N // tn,),
        in_specs=[pl.BlockSpec((B, D), lambda j: (0, 0)),
                  pl.BlockSpec((D, tn), lambda j: (0, j)),
                  pl.BlockSpec((1, tn), lambda j: (0, j))],
        out_specs=pl.BlockSpec((B, tn), lambda j: (0, j)),
        out_shape=jax.ShapeDtypeStruct((B, N), F32),
        compiler_params=_params("arbitrary"),
        name="ada",
    )(c, w, b.reshape(1, N))


def _rms_mod(x, nw, sh, sc):
    ms = jnp.mean(x * x, axis=-1, keepdims=True)
    return (x * lax.rsqrt(ms + EPS) * nw) * (1.0 + sc) + sh


def _inproj_kernel(x_ref, sh_ref, sc_ref, nw_ref, wg_ref, wlr_ref, wd_ref, decw_ref, decb_ref,
                   qg_ref, kg_ref, vg_ref, rg_ref, laf_ref, lab_ref, qd_ref, kd_ref, vd_ref):
    h = _rms_mod(x_ref[...], nw_ref[...], sh_ref[0], sc_ref[0]).astype(BF16)
    g = _dot(h, wg_ref[...])
    qg_ref[...] = g[:, :GLA_QK] * (GLA_DK ** -0.5)
    kg_ref[...] = g[:, GLA_QK:2 * GLA_QK]
    vg_ref[...] = g[:, 2 * GLA_QK:2 * GLA_QK + GLA_WIDTH].astype(BF16)
    rg_ref[...] = g[:, 2 * GLA_QK + GLA_WIDTH:]
    lr = _dot(h, wlr_ref[...])
    z = _dot(lr.astype(BF16), decw_ref[...]) + decb_ref[...]
    la = (jnp.minimum(z, 0.0) - jnp.log(1.0 + jnp.exp(-jnp.abs(z)))) * (1.0 / GLA_GATE_TEMP)
    laf_ref[...] = la[:, :GLA_QK]
    lab_ref[...] = la[:, GLA_QK:]
    d = _dot(h, wd_ref[...])
    qd_ref[0] = (d[:, :DIFF_WIDTH] * (DIFF_DH ** -0.5 * LOG2E)).T.astype(BF16)
    kd_ref[...] = d[:, DIFF_WIDTH:2 * DIFF_WIDTH].astype(BF16)
    vt = d[:, 2 * DIFF_WIDTH:].T.astype(BF16)
    for hd in range(DIFF_HEADS):
        vd_ref[0, hd, :DIFF_DV, :] = vt[hd * DIFF_DV:(hd + 1) * DIFF_DV, :]
    vd_ref[0, :, DIFF_DV:, :] = jnp.ones((DIFF_HEADS, ONES_ROWS, vt.shape[1]), BF16)


def _inproj(x2d, sh, sc, nw, wg, wlr, wd, decw, decb, L, tm):
    T, D = x2d.shape
    per_b = L // tm
    row = lambda i: (i, 0)
    full = lambda i: (0, 0)
    bvec = lambda i: (i // per_b, 0, 0)
    B = T // L
    outs = [(GLA_QK, F32), (GLA_QK, F32), (GLA_WIDTH, BF16), (GLA_WIDTH, F32), (GLA_QK, F32), (GLA_QK, F32)]
    col = lambda i: (i // per_b, 0, i % per_b)
    col4 = lambda i: (i // per_b, 0, 0, i % per_b)
    return pl.pallas_call(
        _inproj_kernel,
        grid=(T // tm,),
        in_specs=[pl.BlockSpec((tm, D), row),
                  pl.BlockSpec((1, 1, D), bvec), pl.BlockSpec((1, 1, D), bvec),
                  pl.BlockSpec((1, D), full),
                  pl.BlockSpec(wg.shape, full), pl.BlockSpec(wlr.shape, full), pl.BlockSpec(wd.shape, full),
                  pl.BlockSpec(decw.shape, full), pl.BlockSpec(decb.shape, full)],
        out_specs=[pl.BlockSpec((tm, n), row) for n, _ in outs] + [
            pl.BlockSpec((1, DIFF_WIDTH, tm), col), pl.BlockSpec((tm, DIFF_WIDTH), row),
            pl.BlockSpec((1, DIFF_HEADS, DIFF_DV + ONES_ROWS, tm), col4)],
        out_shape=[jax.ShapeDtypeStruct((T, n), dt) for n, dt in outs] + [
            jax.ShapeDtypeStruct((B, DIFF_WIDTH, L), BF16), jax.ShapeDtypeStruct((T, DIFF_WIDTH), BF16),
            jax.ShapeDtypeStruct((B, DIFF_HEADS, DIFF_DV + ONES_ROWS, L), BF16)],
        compiler_params=_params("arbitrary"),
        name="inproj",
    )(x2d, sh, sc, nw, wg, wlr, wd, decw, decb)


def _gla_kernel(qf_ref, kf_ref, vf_ref, laf_ref, qb_ref, kb_ref, vb_ref, lab_ref,
                of_ref, ob_ref, sf_ref, sb_ref, *, n_chunks, group):
    C = GLA_CHUNK

    @pl.when(pl.program_id(1) == 0)
    def _():
        sf_ref[...] = jnp.zeros_like(sf_ref)
        sb_ref[...] = jnp.zeros_like(sb_ref)

    r = lax.broadcasted_iota(jnp.int32, (C, C), 0)
    s = lax.broadcasted_iota(jnp.int32, (C, C), 1)
    tri_f = jnp.where(s <= r, 1.0, 0.0).astype(BF16)
    tri_b = jnp.where(s >= r, 1.0, 0.0).astype(BF16)
    r4 = lax.broadcasted_iota(jnp.int32, (GLA_HEADS * C, C), 0) % C
    s4 = lax.broadcasted_iota(jnp.int32, (GLA_HEADS * C, C), 1)
    mask_f = s4 <= r4
    mask_b = s4 > r4
    lane = lax.broadcasted_iota(jnp.int32, (C, GLA_QK), 1)
    head_masks = [(lane >= hd * GLA_DK) & (lane < (hd + 1) * GLA_DK) for hd in range(GLA_HEADS)]
    nt = (((1,), (1,)), ((), ()))

    def per_head(x):
        zero = jnp.zeros_like(x)
        return [jnp.where(mk, x, zero) for mk in head_masks]

    dirs = ((qf_ref, kf_ref, vf_ref, laf_ref, of_ref, sf_ref, tri_f, mask_f, C - 1, False),
            (qb_ref, kb_ref, vb_ref, lab_ref, ob_ref, sb_ref, tri_b, mask_b, 0, True))

    def run_group(g):
        work = []
        for (q_ref, k_ref, v_ref, la_ref, o_ref, s_ref, tri, att_mask, last_row, rev) in dirs:
            for i in range(group):
                c = g * group + i
                c = (n_chunks - 1 - c) if rev else c
                work.append(dict(rows=pl.ds(pl.multiple_of(c * C, C), C), q_ref=q_ref, k_ref=k_ref, v_ref=v_ref,
                                 la_ref=la_ref, o_ref=o_ref, s_ref=s_ref, tri=tri, att_mask=att_mask,
                                 last_row=last_row))
        for w in work:
            la_hi, la_lo = _split_bf16(w["la_ref"][w["rows"], :])
            w["b"] = _dot(w["tri"], la_hi) + _dot(w["tri"], la_lo)
        for w in work:
            b = w["b"]
            tot = b[w["last_row"]:w["last_row"] + 1, :]
            q = w["q_ref"][w["rows"], :]
            k = w["k_ref"][w["rows"], :]
            w["dec"] = jnp.exp(tot)
            w["qs"] = jnp.concatenate(per_head((q * jnp.exp(b)).astype(BF16)), axis=0)
            w["k_in"] = (k * jnp.exp(-b)).astype(BF16)
            w["k_st"] = jnp.concatenate(per_head((k * jnp.exp(tot - b)).astype(BF16)), axis=0)
            v = w["v_ref"][w["rows"], :]
            w["v"] = v
            v_rows = jnp.concatenate([v[:, hd * GLA_DV:(hd + 1) * GLA_DV] for hd in range(GLA_HEADS)], axis=0)
            w["vt"] = v_rows.astype(F32).T.astype(BF16)
        for w in work:
            att = lax.dot_general(w["qs"], w["k_in"], nt, preferred_element_type=F32)
            w["att"] = jnp.where(w["att_mask"], att, 0.0).astype(BF16)
        for w in work:
            w["o"] = jnp.concatenate(
                [_dot(w["att"][hd * C:(hd + 1) * C, :], w["v"][:, hd * GLA_DV:(hd + 1) * GLA_DV])
                 for hd in range(GLA_HEADS)], axis=1)
        for w in work:
            w["upd"] = _dot(w["vt"], w["k_st"])
        for d in range(2):
            s_ref = dirs[d][5]
            st = s_ref[...]
            for w in work[d * group:(d + 1) * group]:
                oi = lax.dot_general(w["qs"], st.astype(BF16), nt, preferred_element_type=F32)
                oi = jnp.concatenate([oi[hd * C:(hd + 1) * C, :] for hd in range(GLA_HEADS)], axis=1)
                w["o_ref"][w["rows"], :] = w["o"] + oi
                st = st * w["dec"] + w["upd"]
            s_ref[...] = st

    n_groups = n_chunks // group
    if n_groups == 1:
        run_group(0)
    else:
        def body(g, carry):
            run_group(g)
            return carry
        lax.fori_loop(0, n_groups, body, 0)


def _gla(qg, kg, vg, laf, lab, B, L, tb):
    T = B * L
    J = L // tb
    fwd = lambda b, j: (b * J + j, 0)
    bwd = lambda b, j: (b * J + (J - 1 - j), 0)
    qk = lambda im: pl.BlockSpec((tb, GLA_QK), im)
    vv = lambda im: pl.BlockSpec((tb, GLA_WIDTH), im)
    return pl.pallas_call(
        functools.partial(_gla_kernel, n_chunks=tb // GLA_CHUNK, group=min(8, tb // GLA_CHUNK)),
        grid=(B, J),
        in_specs=[qk(fwd), qk(fwd), vv(fwd), qk(fwd), qk(bwd), qk(bwd), vv(bwd), qk(bwd)],
        out_specs=[vv(fwd), vv(bwd)],
        out_shape=[jax.ShapeDtypeStruct((T, GLA_WIDTH), F32)] * 2,
        scratch_shapes=[pltpu.VMEM((GLA_DV, GLA_QK), F32)] * 2,
        compiler_params=_params("arbitrary", "arbitrary"),
        name="gla",
    )(qg, kg, vg, laf, qg, kg, vg, lab)


def _bias_from_rel(rel, tab_ref, hd):
    n = jnp.abs(rel)
    half = REL_BUCKETS // 2
    neg = jnp.full(rel.shape, tab_ref[0, hd], F32)
    pos = jnp.full(rel.shape, tab_ref[half, hd], F32)
    for j, t in enumerate(BUCKET_STEPS):
        ge = n >= t
        neg = jnp.where(ge, tab_ref[j + 1, hd], neg)
        pos = jnp.where(ge, tab_ref[half + j + 1, hd], pos)
    return jnp.where(rel > 0, pos, neg)


KEY_CHUNK = 128
COL_TILE = 256
CLS_LEFT, CLS_RIGHT, CLS_STRIP0, CLS_GENERIC = 0, 1, 2, 99
STEP_LEFT, STEP_RIGHT, STEP_MIXED = 0, 1, 2


def _strips_kernel(tab_ref, o_ref, *, tq, n_strips):
    hd = pl.program_id(0)
    kk = lax.broadcasted_iota(jnp.int32, (KEY_CHUNK, tq), 0)
    qq = lax.broadcasted_iota(jnp.int32, (KEY_CHUNK, tq), 1)
    o_ref[0, CLS_LEFT] = jnp.full((KEY_CHUNK, tq), tab_ref[REL_BUCKETS // 2 - 1, hd], F32)
    o_ref[0, CLS_RIGHT] = jnp.full((KEY_CHUNK, tq), tab_ref[REL_BUCKETS - 1, hd], F32)
    for j in range(n_strips):
        o_ref[0, CLS_STRIP0 + j] = _bias_from_rel(KEY_CHUNK * (j - 1) + kk - qq, tab_ref, hd)


def _strips(table, tq, n_strips):
    H = table.shape[1]
    return pl.pallas_call(
        functools.partial(_strips_kernel, tq=tq, n_strips=n_strips),
        grid_spec=pltpu.PrefetchScalarGridSpec(
            num_scalar_prefetch=1, grid=(H,), in_specs=[],
            out_specs=pl.BlockSpec((1, CLS_STRIP0 + n_strips, KEY_CHUNK, tq), lambda h, *_: (h, 0, 0, 0))),
        out_shape=jax.ShapeDtypeStruct((H, CLS_STRIP0 + n_strips, KEY_CHUNK, tq), F32),
        compiler_params=_params("arbitrary"),
        name="bias_strips",
    )(table)


def _diff_finish(acc, l, lq1_ref, lk1_ref, lq2_ref, lk2_ref, nw_ref, o_ref, tq):
    lam = (jnp.exp(jnp.sum(lq1_ref[...] * lk1_ref[...], axis=1, keepdims=True))
           - jnp.exp(jnp.sum(lq2_ref[...] * lk2_ref[...], axis=1, keepdims=True)) + LAMBDA_INIT)
    o = acc[:, :tq] / l[:, :tq] - lam * (acc[:, tq:] / l[:, tq:])
    ms = jnp.mean(o * o, axis=0, keepdims=True)
    nw = jnp.concatenate([nw_ref[...]] * (tq // 128), axis=1)
    o = o * lax.rsqrt(ms + EPS) * nw * (1.0 - LAMBDA_INIT)
    o_ref[0] = o.T.astype(o_ref.dtype)


def _stack_maps(qT):
    row = lax.broadcasted_iota(jnp.int32, qT.shape, 0)
    zero = jnp.zeros_like(qT)
    return jnp.concatenate([jnp.where(row < DIFF_DH, qT, zero), jnp.where(row >= DIFF_DH, qT, zero)], axis=1)


def _diff_fast_kernel(chunkcls_ref, tab_ref,
                      qT_ref, k_ref, vT_ref, strip_ref, lq1_ref, lk1_ref, lq2_ref, lk2_ref, nw_ref,
                      o_ref, qs_ref, ma_ref, mb_ref, acca_ref, accb_ref, sa_ref, sb_ref, smaxa_ref, smaxb_ref,
                      *, tq, tk, n_kt, n_qt):
    hd = pl.program_id(1)
    r0 = pl.program_id(0) * n_qt
    n_ct = 2 * tq // COL_TILE
    n_chunk = tk // KEY_CHUNK
    for qt in range(n_qt):
        qs_ref[qt] = _stack_maps(qT_ref[0, :, qt * tq:(qt + 1) * tq])
    bias_max = functools.reduce(jnp.maximum, [tab_ref[i, hd] for i in range(REL_BUCKETS)])

    def scores_tile(qt, kt, buf, j):
        s_ref, smax_ref = buf
        kblk = k_ref[0, pl.ds(pl.multiple_of(kt * tk, tk), tk), :]
        cols = slice(j * COL_TILE, (j + 1) * COL_TILE)
        s = _dot(kblk, qs_ref[qt, :, cols])
        s_ref[:, cols] = s
        smax_ref[:, cols] = jnp.max(s, axis=0, keepdims=True)

    def chain(qt, kt, buf, j, state):
        s_ref, smax_ref = buf
        m_ref, acc_ref = state
        cols = slice(j * COL_TILE, (j + 1) * COL_TILE)
        q0 = (j * COL_TILE) % tq
        m_old = m_ref[:, cols]
        m_new = jnp.maximum(m_old, smax_ref[:, cols] + bias_max)
        alpha = jnp.exp2(m_old - m_new)
        ps = []
        for c in range(n_chunk):
            rows = slice(c * KEY_CHUNK, (c + 1) * KEY_CHUNK)
            bias = strip_ref[0, chunkcls_ref[r0 + qt, kt * n_chunk + c], :, q0:q0 + COL_TILE]
            ps.append(jnp.exp2((s_ref[rows, cols] + bias) - m_new).astype(BF16))
        vblk = vT_ref[0, 0, :, pl.ds(pl.multiple_of(kt * tk, tk), tk)]
        acc_ref[:, cols] = alpha * acc_ref[:, cols] + _dot(vblk, jnp.concatenate(ps, axis=0))
        m_ref[:, cols] = m_new

    def step(qt, kt, cur, nxt, nxt_tile, state):
        for j in range(n_ct):
            if nxt_tile is not None:
                scores_tile(nxt_tile[0], nxt_tile[1], nxt, j)
            chain(qt, kt, cur, j, state)

    buf_a = (sa_ref, smaxa_ref)
    buf_b = (sb_ref, smaxb_ref)

    def finish(qt, state):
        acc = state[1][...]
        _diff_finish(acc[:DIFF_DV], acc[DIFF_DV:DIFF_DV + 1], lq1_ref, lk1_ref, lq2_ref, lk2_ref, nw_ref,
                     o_ref.at[:, pl.ds(pl.multiple_of(qt * tq, tq), tq), :], tq)

    def query_tile(qt, handoff, state, prev):
        m_ref, acc_ref = state
        m_ref[...] = jnp.full(m_ref.shape, -1e30, F32)
        acc_ref[...] = jnp.zeros(acc_ref.shape, F32)
        if prev is not None:
            finish(*prev)
        for kt in range(n_kt):
            cur, nxt = (buf_a, buf_b) if kt % 2 == 0 else (buf_b, buf_a)
            if kt + 1 < n_kt:
                nxt_tile = (qt, kt + 1)
            else:
                nxt_tile = (qt + 1, 0) if handoff else None
            step(qt, kt, cur, nxt, nxt_tile, state)

    state_a = (ma_ref, acca_ref)
    state_b = (mb_ref, accb_ref)
    for j in range(n_ct):
        scores_tile(0, 0, buf_a, j)
    query_tile(0, True, state_a, None)

    def body(pp, carry):
        q1 = 2 * pp + 1
        query_tile(q1, True, state_b, (q1 - 1, state_a))
        query_tile(q1 + 1, True, state_a, (q1, state_b))
        return carry

    lax.fori_loop(0, (n_qt - 2) // 2, body, 0)
    query_tile(n_qt - 1, False, state_b, (n_qt - 2, state_a))
    finish(n_qt - 1, state_b)


def _diff_kernel(stepcls_ref, chunkcls_ref, tab_ref,
                 qT_ref, k_ref, vT_ref, strip_ref, posq_ref, posk_ref, lq1_ref, lk1_ref, lq2_ref, lk2_ref, nw_ref,
                 o_ref, m_ref, l_ref, acc_ref, sa_ref, sb_ref, smaxa_ref, smaxb_ref, bias_ref,
                 *, tq, tk, n_kt, n_strips):
    hd = pl.program_id(1)
    r = pl.program_id(0) * pl.num_programs(2) + pl.program_id(2)
    half = REL_BUCKETS // 2
    n_ct = 2 * tq // COL_TILE
    n_chunk = tk // KEY_CHUNK

    qstack = _stack_maps(qT_ref[0])

    m_ref[...] = jnp.full(m_ref.shape, -1e30, F32)
    l_ref[...] = jnp.zeros(l_ref.shape, F32)
    acc_ref[...] = jnp.zeros(acc_ref.shape, F32)

    bias_max = functools.reduce(jnp.maximum, [tab_ref[i, hd] for i in range(REL_BUCKETS)])

    def scores_tile(kt, buf, j):
        s_ref, smax_ref = buf
        kblk = k_ref[0, pl.ds(pl.multiple_of(kt * tk, tk), tk), :]
        cols = slice(j * COL_TILE, (j + 1) * COL_TILE)
        s = _dot(kblk, qstack[:, cols])
        s_ref[:, cols] = s
        smax_ref[:, cols] = jnp.max(s, axis=0, keepdims=True)

    def scores(kt, buf):
        for j in range(n_ct):
            scores_tile(kt, buf, j)

    def build_bias(kt):
        posq = posq_ref[0]
        for c in range(n_chunk):
            cc = chunkcls_ref[r, kt * n_chunk + c]
            rows = slice(c * KEY_CHUNK, (c + 1) * KEY_CHUNK)

            @pl.when(cc == CLS_LEFT)
            def _():
                bias_ref[rows, :] = jnp.full((KEY_CHUNK, tq), tab_ref[half - 1, hd], F32)

            @pl.when(cc == CLS_RIGHT)
            def _():
                bias_ref[rows, :] = jnp.full((KEY_CHUNK, tq), tab_ref[REL_BUCKETS - 1, hd], F32)

            @pl.when(jnp.logical_and(cc >= CLS_STRIP0, cc < CLS_STRIP0 + n_strips))
            def _():
                bias_ref[rows, :] = strip_ref[0, cc]

            @pl.when(cc == CLS_GENERIC)
            def _():
                k0 = pl.multiple_of(kt * tk + c * KEY_CHUNK, KEY_CHUNK)
                pk = posk_ref[0, pl.ds(k0, KEY_CHUNK), :]
                pk = jnp.concatenate([pk] * (tq // 128), axis=1)
                bias_ref[rows, :] = _bias_from_rel(pk - posq, tab_ref, hd)

    def softmax_pv(kt, cur, const, nxt):
        s_ref, smax_ref = cur
        vblk = vT_ref[0, 0, :DIFF_DV, pl.ds(pl.multiple_of(kt * tk, tk), tk)]
        for j in range(n_ct):
            if nxt is not None:
                scores_tile(kt + 1, nxt, j)
            cols = slice(j * COL_TILE, (j + 1) * COL_TILE)
            shift = bias_max if const is None else const
            m_old = m_ref[:, cols]
            m_new = jnp.maximum(m_old, smax_ref[:, cols] + shift)
            alpha = jnp.exp2(m_old - m_new)
            if const is None:
                q0 = (j * COL_TILE) % tq
                p = jnp.exp2((s_ref[:, cols] + bias_ref[:, q0:q0 + COL_TILE]) - m_new)
            else:
                p = jnp.exp2(s_ref[:, cols] - (m_new - shift))
            l_ref[:, cols] = alpha * l_ref[:, cols] + jnp.sum(p, axis=0, keepdims=True)
            acc_ref[:, cols] = alpha * acc_ref[:, cols] + _dot(vblk, p.astype(BF16))
            m_ref[:, cols] = m_new

    def step(kt, cur, nxt):
        cls = stepcls_ref[r, kt]

        @pl.when(cls == STEP_MIXED)
        def _():
            build_bias(kt)
            softmax_pv(kt, cur, None, nxt)

        @pl.when(cls != STEP_MIXED)
        def _():
            const = jnp.where(cls == STEP_LEFT, tab_ref[half - 1, hd], tab_ref[REL_BUCKETS - 1, hd])
            softmax_pv(kt, cur, const, nxt)

    buf_a = (sa_ref, smaxa_ref)
    buf_b = (sb_ref, smaxb_ref)
    scores(0, buf_a)

    def pair(kk, carry):
        step(2 * kk, buf_a, buf_b)
        step(2 * kk + 1, buf_b, buf_a)
        return carry

    lax.fori_loop(0, n_kt // 2 - 1, pair, 0)
    step(n_kt - 2, buf_a, buf_b)
    step(n_kt - 1, buf_b, None)

    _diff_finish(acc_ref[...], l_ref[...], lq1_ref, lk1_ref, lq2_ref, lk2_ref, nw_ref, o_ref, tq)


def _classify(positions, tq, tk, n_strips):
    B, L = positions.shape
    nq, nc = L // tq, L // KEY_CHUNK
    pq = positions.reshape(B, nq, tq)
    pk = positions.reshape(B, nc, KEY_CHUNK)
    qlo, qhi, q0 = pq.min(axis=2), pq.max(axis=2), pq[:, :, 0]
    klo, khi, k0 = pk.min(axis=2), pk.max(axis=2), pk[:, :, 0]
    q_run = jnp.all(pq == q0[:, :, None] + jnp.arange(tq, dtype=positions.dtype), axis=2)
    k_run = jnp.all(pk == k0[:, :, None] + jnp.arange(KEY_CHUNK, dtype=positions.dtype), axis=2)
    left = (qlo[:, :, None] - khi[:, None, :]) >= FAR_DIST
    right = (klo[:, None, :] - qhi[:, :, None]) >= FAR_DIST
    d = k0[:, None, :] - q0[:, :, None]
    j = d // KEY_CHUNK + 1
    strip_ok = (q_run[:, :, None] & k_run[:, None, :] & (d % KEY_CHUNK == 0) & (j >= 0) & (j < n_strips))
    cls = jnp.where(left, CLS_LEFT, jnp.where(right, CLS_RIGHT, jnp.where(strip_ok, CLS_STRIP0 + j, CLS_GENERIC)))
    cls = cls.astype(jnp.int32)
    per_step = cls.reshape(B, nq, L // tk, tk // KEY_CHUNK)
    step = jnp.where(jnp.all(per_step == CLS_LEFT, axis=3), STEP_LEFT,
                     jnp.where(jnp.all(per_step == CLS_RIGHT, axis=3), STEP_RIGHT, STEP_MIXED)).astype(jnp.int32)
    return step.reshape(B * nq, L // tk), cls.reshape(B * nq, nc)


def _diffattn(qT, kd, vT, positions, table, lq1, lk1, lq2, lk2, nw, tq, tk):
    B, _, L = qT.shape
    nq, nk = L // tq, L // tk
    n_strips = tq // KEY_CHUNK + 2
    n_cls = CLS_STRIP0 + n_strips
    table2 = table * LOG2E
    strips = _strips(table2, tq, n_strips)
    stepcls, chunkcls = _classify(positions, tq, tk, n_strips)
    nw_b = jnp.broadcast_to(nw.reshape(DIFF_DV, 1), (DIFF_DV, 128))
    vecs = [a.reshape(1, DIFF_DH) for a in (lq1, lk1, lq2, lk2)]
    full2 = lambda b, h, i, *_: (0, 0)
    q_spec = pl.BlockSpec((1, DIFF_DV, tq), lambda b, h, i, *_: (b, h, i))
    k_spec = pl.BlockSpec((1, L, DIFF_DV), lambda b, h, i, *_: (b, 0, h))
    v_spec = pl.BlockSpec((1, 1, DIFF_DV + ONES_ROWS, L), lambda b, h, i, *_: (b, h, 0, 0))
    strip_spec = pl.BlockSpec((1, n_cls, KEY_CHUNK, tq), lambda b, h, i, *_: (h, 0, 0, 0))
    tail_specs = [pl.BlockSpec((1, DIFF_DH), full2)] * 4 + [pl.BlockSpec((DIFF_DV, 128), full2)]
    out_spec = pl.BlockSpec((1, tq, DIFF_DV), lambda b, h, i, *_: (b, i, h))
    out_shape = jax.ShapeDtypeStruct((B, L, DIFF_WIDTH), BF16)
    s_buf = pltpu.VMEM((tk, 2 * tq), F32)
    row_buf = pltpu.VMEM((1, 2 * tq), F32)
    acc_buf = pltpu.VMEM((DIFF_DV + ONES_ROWS, 2 * tq), F32)

    def fast(_):
        one = lambda b, h, *_: (0, 0)
        return pl.pallas_call(
            functools.partial(_diff_fast_kernel, tq=tq, tk=tk, n_kt=nk, n_qt=nq),
            grid_spec=pltpu.PrefetchScalarGridSpec(
                num_scalar_prefetch=2, grid=(B, DIFF_HEADS),
                in_specs=[pl.BlockSpec((1, DIFF_DV, L), lambda b, h, *_: (b, h, 0)),
                          pl.BlockSpec((1, L, DIFF_DV), lambda b, h, *_: (b, 0, h)),
                          pl.BlockSpec((1, 1, DIFF_DV + ONES_ROWS, L), lambda b, h, *_: (b, h, 0, 0)),
                          pl.BlockSpec((1, n_cls, KEY_CHUNK, tq), lambda b, h, *_: (h, 0, 0, 0))]
                + [pl.BlockSpec((1, DIFF_DH), one)] * 4 + [pl.BlockSpec((DIFF_DV, 128), one)],
                out_specs=pl.BlockSpec((1, L, DIFF_DV), lambda b, h, *_: (b, 0, h)),
                scratch_shapes=[pltpu.VMEM((nq, DIFF_DV, 2 * tq), BF16),
                                row_buf, row_buf, acc_buf, acc_buf, s_buf, s_buf, row_buf, row_buf]),
            out_shape=out_shape,
            compiler_params=_params("arbitrary", "arbitrary"),
            name="diffattn",
        )(chunkcls, table2, qT, kd, vT, strips, *vecs, nw_b)

    def general(_):
        posq = positions.reshape(B, 1, L)
        posk = jnp.broadcast_to(positions[:, :, None], (B, L, 128))
        return pl.pallas_call(
            functools.partial(_diff_kernel, tq=tq, tk=tk, n_kt=nk, n_strips=n_strips),
            grid_spec=pltpu.PrefetchScalarGridSpec(
                num_scalar_prefetch=3, grid=(B, DIFF_HEADS, nq),
                in_specs=[q_spec, k_spec, v_spec, strip_spec,
                          pl.BlockSpec((1, 1, tq), lambda b, h, i, *_: (b, 0, i)),
                          pl.BlockSpec((1, L, 128), lambda b, h, i, *_: (b, 0, 0))] + tail_specs,
                out_specs=out_spec,
                scratch_shapes=[row_buf, row_buf, pltpu.VMEM((DIFF_DV, 2 * tq), F32), s_buf, s_buf,
                                row_buf, row_buf, pltpu.VMEM((tk, tq), F32)]),
            out_shape=out_shape,
            compiler_params=_params("arbitrary", "arbitrary", "arbitrary"),
            name="diffattn_general",
        )(stepcls, chunkcls, table2, qT, kd, vT, strips, posq, posk, *vecs, nw_b)

    return lax.cond(jnp.all(chunkcls != CLS_GENERIC), fast, general, 0)


def _outproj_kernel(of_ref, ob_ref, rg_ref, gnw_ref, od_ref, wa_ref, wb_ref, ga_ref, x_ref, o_ref):
    o = of_ref[...] + ob_ref[...]
    parts = []
    for hd in range(GLA_HEADS):
        oh = o[:, hd * GLA_DV:(hd + 1) * GLA_DV]
        ms = jnp.mean(oh * oh, axis=-1, keepdims=True)
        parts.append(oh * lax.rsqrt(ms + EPS))
    r = rg_ref[...]
    oa = jnp.concatenate(parts, axis=1) * gnw_ref[...] * (r * _sigmoid(r))
    mixed = _dot(oa.astype(BF16), wa_ref[...]) + _dot(od_ref[...], wb_ref[...])
    o_ref[...] = x_ref[...] + ga_ref[0] * mixed


def _outproj(of, ob, rg, gnw, od, wa, wb, ga, x2d, L, tm):
    T, D = x2d.shape
    per_b = L // tm
    row = lambda i: (i, 0)
    full = lambda i: (0, 0)
    bvec = lambda i: (i // per_b, 0, 0)
    return pl.pallas_call(
        _outproj_kernel,
        grid=(T // tm,),
        in_specs=[pl.BlockSpec((tm, GLA_WIDTH), row), pl.BlockSpec((tm, GLA_WIDTH), row),
                  pl.BlockSpec((tm, GLA_WIDTH), row), pl.BlockSpec((1, GLA_WIDTH), full),
                  pl.BlockSpec((tm, DIFF_WIDTH), row),
                  pl.BlockSpec(wa.shape, full), pl.BlockSpec(wb.shape, full),
                  pl.BlockSpec((1, 1, D), bvec), pl.BlockSpec((tm, D), row)],
        out_specs=pl.BlockSpec((tm, D), row),
        out_shape=jax.ShapeDtypeStruct((T, D), F32),
        compiler_params=_params("arbitrary"),
        name="outproj",
    )(of, ob, rg, gnw, od, wa, wb, ga, x2d)


HALO = 8
FF_CHUNK = 256


def _ffn_kernel(x_ref, xp_ref, xn_ref, sh_ref, sc_ref, gf_ref, nw_ref, wup_ref, cw_ref, cb_ref, wdn_ref, fw_ref,
                o_ref, acc_ref, *, tm, per_b):
    i = pl.program_id(0)
    first = (i % per_b) == 0
    last = (i % per_b) == per_b - 1
    nw, sh, sc = nw_ref[...], sh_ref[0], sc_ref[0]
    x = x_ref[...]
    h_mid = _rms_mod(x, nw, sh, sc)
    h_prev = jnp.where(first, 0.0, _rms_mod(xp_ref[...], nw, sh, sc))
    h_next = jnp.where(last, 0.0, _rms_mod(xn_ref[...], nw, sh, sc))
    h = jnp.concatenate([h_prev, h_mid, h_next], axis=0).astype(BF16)
    rows = tm + 2 * HALO

    def conv(u, col):
        w = cw_ref[:, col:col + FF_CHUNK]
        up = pltpu.roll(u, 1, 0)[HALO:HALO + tm]
        un = pltpu.roll(u, rows - 1, 0)[HALO:HALO + tm]
        return (w[0:1] * up + w[1:2] * u[HALO:HALO + tm] + w[2:3] * un) + cb_ref[:, col:col + FF_CHUNK]

    def up(n):
        cg, cv = n * FF_CHUNK, D_FF + n * FF_CHUNK
        return _dot(h, wup_ref[:, cg:cg + FF_CHUNK]), _dot(h, wup_ref[:, cv:cv + FF_CHUNK])

    n_chunks = D_FF // FF_CHUNK
    u_gate, u_val = up(0)
    for n in range(n_chunks):
        cg, cv = n * FF_CHUNK, D_FF + n * FF_CHUNK
        if n + 1 < n_chunks:
            nxt = up(n + 1)
        gate = conv(u_gate, cg)
        val = conv(u_val, cv)
        a = (gate * _sigmoid(gate) * val).astype(BF16)
        part = _dot(a, wdn_ref[cg:cg + FF_CHUNK, :])
        if n == 0:
            acc_ref[...] = part
        else:
            acc_ref[...] += part
        if n + 1 < n_chunks:
            u_gate, u_val = nxt
    y = x + gf_ref[0] * acc_ref[...]
    ms = jnp.mean(y * y, axis=-1, keepdims=True)
    o_ref[...] = y * lax.rsqrt(ms + EPS) * fw_ref[...]


def _ffn(x1, sh, sc, gf, nw, wup, cw, cb, wdn, fw, L, tm):
    T, D = x1.shape
    per_b = L // tm
    hb = tm // HALO
    n_halo = T // HALO
    row = lambda i: (i, 0)
    full = lambda i: (0, 0)
    bvec = lambda i: (i // per_b, 0, 0)
    return pl.pallas_call(
        functools.partial(_ffn_kernel, tm=tm, per_b=per_b),
        grid=(T // tm,),
        in_specs=[pl.BlockSpec((tm, D), row),
                  pl.BlockSpec((HALO, D), lambda i: (jnp.maximum(i * hb - 1, 0), 0)),
                  pl.BlockSpec((HALO, D), lambda i: (jnp.minimum((i + 1) * hb, n_halo - 1), 0)),
                  pl.BlockSpec((1, 1, D), bvec), pl.BlockSpec((1, 1, D), bvec), pl.BlockSpec((1, 1, D), bvec),
                  pl.BlockSpec((1, D), full),
                  pl.BlockSpec(wup.shape, full), pl.BlockSpec(cw.shape, full), pl.BlockSpec(cb.shape, full),
                  pl.BlockSpec(wdn.shape, full), pl.BlockSpec((1, D), full)],
        out_specs=pl.BlockSpec((tm, D), row),
        out_shape=jax.ShapeDtypeStruct((T, D), F32),
        scratch_shapes=[pltpu.VMEM((tm, D), F32)],
        compiler_params=_params("arbitrary"),
        name="ffn",
    )(x1, x1, x1, sh, sc, gf, nw, wup, cw, cb, wdn, fw)


def kernel(x, c, positions, w_ada, b_ada, attn_norm_w, w_in, gla_dec_w_fwd, gla_dec_b_fwd, gla_dec_w_bwd,
           gla_dec_b_bwd, gla_norm_w, diff_lambda_q1, diff_lambda_k1, diff_lambda_q2, diff_lambda_k2,
           diff_norm_w, rel_bias_table, w_out, ffn_norm_w, w_up, conv_w, conv_b, w_down, final_norm_w):
    B, L, D = x.shape
    T = B * L
    tm = min(512, L)
    x2d = x.reshape(T, D)

    mod = _ada(c, w_ada[0], b_ada[0])
    sh_a, sc_a, g_a, sh_f, sc_f, g_f = [m.reshape(B, 1, D) for m in jnp.split(mod, N_MOD, axis=-1)]

    o_lr = 2 * GLA_QK + 2 * GLA_WIDTH
    o_d = o_lr + 2 * GLA_RANK
    w = w_in[0]
    wg = w[:, :o_lr].astype(BF16)
    wlr = jnp.pad(w[:, o_lr:o_d], ((0, 0), (0, 128 - 2 * GLA_RANK))).astype(BF16)
    wd = w[:, o_d:].astype(BF16)
    decw = jnp.zeros((128, 2 * GLA_QK), F32)
    decw = decw.at[:GLA_RANK, :GLA_QK].set(gla_dec_w_fwd[0]).at[GLA_RANK:2 * GLA_RANK, GLA_QK:].set(gla_dec_w_bwd[0])
    decb = jnp.concatenate([gla_dec_b_fwd[0], gla_dec_b_bwd[0]]).reshape(1, 2 * GLA_QK)

    qg, kg, vg, rg, laf, lab, qd, kd, vd = _inproj(
        x2d, sh_a, sc_a, attn_norm_w[0].reshape(1, D), wg, wlr, wd, decw.astype(BF16), decb, L, tm)

    o_f, o_b = _gla(qg, kg, vg, laf, lab, B, L, min(512, L))

    o_d = _diffattn(qd, kd.reshape(B, L, DIFF_WIDTH), vd, positions, rel_bias_table,
                    diff_lambda_q1[0], diff_lambda_k1[0], diff_lambda_q2[0], diff_lambda_k2[0],
                    diff_norm_w[0], min(512, L), min(1024, L // 2))

    wo = w_out[0].astype(BF16)
    x1 = _outproj(o_f, o_b, rg, gla_norm_w[0].reshape(1, GLA_WIDTH), o_d.reshape(T, DIFF_WIDTH),
                  wo[:GLA_WIDTH], wo[GLA_WIDTH:], g_a, x2d, L, tm)

    out = _ffn(x1, sh_f, sc_f, g_f, ffn_norm_w[0].reshape(1, D), w_up[0].astype(BF16), conv_w[0],
               conv_b[0].reshape(1, 2 * D_FF), w_down[0].astype(BF16), final_norm_w.reshape(1, D), L, tm)
    return out.reshape(B, L, D)
```

```python
import functools
import math
from typing import NamedTuple

import jax
import jax.numpy as jnp
from jax import lax
from jax.experimental import pallas as pl
from jax.experimental.pallas import tpu as pltpu

F32 = jnp.float32
BF16 = jnp.bfloat16

D_MODEL = 1024
GLA_HEADS = 4
GLA_DK = 64
GLA_DV = 128
GLA_RANK = 16
GLA_GATE_TEMP = 16.0
GLA_CHUNK = 64
GLA_QK = GLA_HEADS * GLA_DK
GLA_WIDTH = GLA_HEADS * GLA_DV
DIFF_HEADS = 4
DIFF_DH = 64
DIFF_DV = 2 * DIFF_DH
DIFF_WIDTH = DIFF_HEADS * DIFF_DV
REL_BUCKETS = 32
D_FF = 2816
N_MOD = 6
EPS = 1e-6
LAMBDA_INIT = 0.8 - 0.6 * math.exp(-0.3 * 0)
LOG2E = 1.4426950408889634
ONES_ROWS = 16

BUCKET_STEPS = (1, 2, 3, 4, 5, 6, 7, 8, 12, 16, 23, 32, 46, 64, 91)
FAR_DIST = BUCKET_STEPS[-1]

V7X_VMEM_BYTES = 64 * 1024 * 1024
VMEM_LIMIT = V7X_VMEM_BYTES - 8 * 1024 * 1024


class _Tiles(NamedTuple):
    proj_rows: int
    ffn_rows: int
    gla_block: int
    attn_q: int
    attn_k: int


def _tiles(seq_len):
    return _Tiles(proj_rows=min(1024, seq_len), ffn_rows=min(512, seq_len), gla_block=min(1024, seq_len),
                  attn_q=min(512, seq_len), attn_k=min(1024, seq_len // 2))


def _split_bf16(a):
    hi = a.astype(BF16)
    lo = (a - hi.astype(F32)).astype(BF16)
    return hi, lo


def _dot(a, b):
    return jnp.dot(a, b, preferred_element_type=F32)


def _sigmoid(v):
    return 1.0 / (1.0 + jnp.exp(-v))


def _params(*sem):
    return pltpu.CompilerParams(dimension_semantics=sem, vmem_limit_bytes=VMEM_LIMIT)


def _ada_kernel(c_ref, w_ref, b_ref, o_ref):
    c = c_ref[...]
    s_hi, s_lo = _split_bf16(c * _sigmoid(c))
    w_hi, w_lo = _split_bf16(w_ref[...])
    o_ref[...] = (_dot(s_hi, w_hi) + _dot(s_lo, w_hi) + _dot(s_hi, w_lo)) + b_ref[...]


def _ada(c, w, b):
    B, D = c.shape
    N = w.shape[1]
    tn = 1024
    return pl.pallas_call(
        _ada_kernel,
        grid=(N // tn,),
        in_specs=[pl.BlockSpec((B, D), lambda j: (0, 0)),
                  pl.BlockSpec((D, tn), lambda j: (0, j)),
                  pl.BlockSpec((1, tn), lambda j: (0, j))],
        out_specs=pl.BlockSpec((B, tn), lambda j: (0, j)),
        out_shape=jax.ShapeDtypeStruct((B, N), F32),
        compiler_params=_params("arbitrary"),
        name="ada",
    )(c, w, b.reshape(1, N))


def _rms_mod(x, nw, sh, sc):
    ms = jnp.mean(x * x, axis=-1, keepdims=True)
    return (x * lax.rsqrt(ms + EPS) * nw) * (1.0 + sc) + sh


def _inproj_kernel(x_ref, sh_ref, sc_ref, nw_ref, wg_ref, wlr_ref, wd_ref, decw_ref, decb_ref,
                   qg_ref, kg_ref, vg_ref, rg_ref, laf_ref, lab_ref, qd_ref, kd_ref, vd_ref):
    h = _rms_mod(x_ref[...], nw_ref[...], sh_ref[0], sc_ref[0]).astype(BF16)
    g = _dot(h, wg_ref[...])
    qg_ref[...] = g[:, :GLA_QK] * (GLA_DK ** -0.5)
    kg_ref[...] = g[:, GLA_QK:2 * GLA_QK]
    vg_ref[...] = g[:, 2 * GLA_QK:2 * GLA_QK + GLA_WIDTH].astype(BF16)
    rg_ref[...] = g[:, 2 * GLA_QK + GLA_WIDTH:]
    lr = _dot(h, wlr_ref[...])
    z = _dot(lr.astype(BF16), decw_ref[...]) + decb_ref[...]
    la = (jnp.minimum(z, 0.0) - jnp.log(1.0 + jnp.exp(-jnp.abs(z)))) * (1.0 / GLA_GATE_TEMP)
    laf_ref[...] = la[:, :GLA_QK]
    lab_ref[...] = la[:, GLA_QK:]
    d = _dot(h, wd_ref[...])
    qd_ref[0] = (d[:, :DIFF_WIDTH] * (DIFF_DH ** -0.5 * LOG2E)).T.astype(BF16)
    kd_ref[...] = d[:, DIFF_WIDTH:2 * DIFF_WIDTH].astype(BF16)
    vt = d[:, 2 * DIFF_WIDTH:].T.astype(BF16)
    for hd in range(DIFF_HEADS):
        vd_ref[0, hd, :DIFF_DV, :] = vt[hd * DIFF_DV:(hd + 1) * DIFF_DV, :]
    vd_ref[0, :, DIFF_DV:, :] = jnp.ones((DIFF_HEADS, ONES_ROWS, vt.shape[1]), BF16)


def _inproj(x2d, sh, sc, nw, wg, wlr, wd, decw, decb, L, tm):
    T, D = x2d.shape
    per_b = L // tm
    row = lambda i: (i, 0)
    full = lambda i: (0, 0)
    bvec = lambda i: (i // per_b, 0, 0)
    B = T // L
    outs = [(GLA_QK, F32), (GLA_QK, F32), (GLA_WIDTH, BF16), (GLA_WIDTH, F32), (GLA_QK, F32), (GLA_QK, F32)]
    col = lambda i: (i // per_b, 0, i % per_b)
    col4 = lambda i: (i // per_b, 0, 0, i % per_b)
    return pl.pallas_call(
        _inproj_kernel,
        grid=(T // tm,),
        in_specs=[pl.BlockSpec((tm, D), row),
                  pl.BlockSpec((1, 1, D), bvec), pl.BlockSpec((1, 1, D), bvec),
                  pl.BlockSpec((1, D), full),
                  pl.BlockSpec(wg.shape, full), pl.BlockSpec(wlr.shape, full), pl.BlockSpec(wd.shape, full),
                  pl.BlockSpec(decw.shape, full), pl.BlockSpec(decb.shape, full)],
        out_specs=[pl.BlockSpec((tm, n), row) for n, _ in outs] + [
            pl.BlockSpec((1, DIFF_WIDTH, tm), col), pl.BlockSpec((tm, DIFF_WIDTH), row),
            pl.BlockSpec((1, DIFF_HEADS, DIFF_DV + ONES_ROWS, tm), col4)],
        out_shape=[jax.ShapeDtypeStruct((T, n), dt) for n, dt in outs] + [
            jax.ShapeDtypeStruct((B, DIFF_WIDTH, L), BF16), jax.ShapeDtypeStruct((T, DIFF_WIDTH), BF16),
            jax.ShapeDtypeStruct((B, DIFF_HEADS, DIFF_DV + ONES_ROWS, L), BF16)],
        compiler_params=_params("arbitrary"),
        name="inproj",
    )(x2d, sh, sc, nw, wg, wlr, wd, decw, decb)


def _gla_kernel(qf_ref, kf_ref, vf_ref, laf_ref, qb_ref, kb_ref, vb_ref, lab_ref,
                of_ref, ob_ref, sf_ref, sb_ref, *, n_chunks, group):
    C = GLA_CHUNK

    @pl.when(pl.program_id(1) == 0)
    def _():
        sf_ref[...] = jnp.zeros_like(sf_ref)
        sb_ref[...] = jnp.zeros_like(sb_ref)

    r = lax.broadcasted_iota(jnp.int32, (C, C), 0)
    s = lax.broadcasted_iota(jnp.int32, (C, C), 1)
    tri_f = jnp.where(s <= r, 1.0, 0.0).astype(BF16)
    tri_b = jnp.where(s >= r, 1.0, 0.0).astype(BF16)
    r4 = lax.broadcasted_iota(jnp.int32, (GLA_HEADS * C, C), 0) % C
    s4 = lax.broadcasted_iota(jnp.int32, (GLA_HEADS * C, C), 1)
    mask_f = s4 <= r4
    mask_b = s4 > r4
    lane = lax.broadcasted_iota(jnp.int32, (C, GLA_QK), 1)
    head_masks = [(lane >= hd * GLA_DK) & (lane < (hd + 1) * GLA_DK) for hd in range(GLA_HEADS)]
    nt = (((1,), (1,)), ((), ()))

    def per_head(x):
        zero = jnp.zeros_like(x)
        return [jnp.where(mk, x, zero) for mk in head_masks]

    dirs = ((qf_ref, kf_ref, vf_ref, laf_ref, of_ref, sf_ref, tri_f, mask_f, C - 1, False),
            (qb_ref, kb_ref, vb_ref, lab_ref, ob_ref, sb_ref, tri_b, mask_b, 0, True))

    def run_group(g):
        work = []
        for (q_ref, k_ref, v_ref, la_ref, o_ref, s_ref, tri, att_mask, last_row, rev) in dirs:
            for i in range(group):
                c = g * group + i
                c = (n_chunks - 1 - c) if rev else c
                work.append(dict(rows=pl.ds(pl.multiple_of(c * C, C), C), q_ref=q_ref, k_ref=k_ref, v_ref=v_ref,
                                 la_ref=la_ref, o_ref=o_ref, s_ref=s_ref, tri=tri, att_mask=att_mask,
                                 last_row=last_row))
        for w in work:
            la_hi, la_lo = _split_bf16(w["la_ref"][w["rows"], :])
            w["b"] = _dot(w["tri"], la_hi) + _dot(w["tri"], la_lo)
        for w in work:
            b = w["b"]
            tot = b[w["last_row"]:w["last_row"] + 1, :]
            q = w["q_ref"][w["rows"], :]
            k = w["k_ref"][w["rows"], :]
            w["dec"] = jnp.exp(tot)
            w["qs"] = jnp.concatenate(per_head((q * jnp.exp(b)).astype(BF16)), axis=0)
            w["k_in"] = (k * jnp.exp(-b)).astype(BF16)
            w["k_st"] = jnp.concatenate(per_head((k * jnp.exp(tot - b)).astype(BF16)), axis=0)
            v = w["v_ref"][w["rows"], :]
            w["v"] = v
            v_rows = jnp.concatenate([v[:, hd * GLA_DV:(hd + 1) * GLA_DV] for hd in range(GLA_HEADS)], axis=0)
            w["vt"] = v_rows.astype(F32).T.astype(BF16)
        for w in work:
            att = lax.dot_general(w["qs"], w["k_in"], nt, preferred_element_type=F32)
            w["att"] = jnp.where(w["att_mask"], att, 0.0).astype(BF16)
        for w in work:
            w["o"] = jnp.concatenate(
                [_dot(w["att"][hd * C:(hd + 1) * C, :], w["v"][:, hd * GLA_DV:(hd + 1) * GLA_DV])
                 for hd in range(GLA_HEADS)], axis=1)
        for w in work:
            w["upd"] = _dot(w["vt"], w["k_st"])
        for d in range(2):
            s_ref = dirs[d][5]
            st = s_ref[...]
            for w in work[d * group:(d + 1) * group]:
                oi = lax.dot_general(w["qs"], st.astype(BF16), nt, preferred_element_type=F32)
                oi = jnp.concatenate([oi[hd * C:(hd + 1) * C, :] for hd in range(GLA_HEADS)], axis=1)
                w["o_ref"][w["rows"], :] = w["o"] + oi
                st = st * w["dec"] + w["upd"]
            s_ref[...] = st

    n_groups = n_chunks // group
    if n_groups == 1:
        run_group(0)
    else:
        def body(g, carry):
            run_group(g)
            return carry
        lax.fori_loop(0, n_groups, body, 0)


def _gla(qg, kg, vg, laf, lab, B, L, tb):
    T = B * L
    J = L // tb
    fwd = lambda b, j: (b * J + j, 0)
    bwd = lambda b, j: (b * J + (J - 1 - j), 0)
    qk = lambda im: pl.BlockSpec((tb, GLA_QK), im)
    vv = lambda im: pl.BlockSpec((tb, GLA_WIDTH), im)
    return pl.pallas_call(
        functools.partial(_gla_kernel, n_chunks=tb // GLA_CHUNK, group=tb // GLA_CHUNK),
        grid=(B, J),
        in_specs=[qk(fwd), qk(fwd), vv(fwd), qk(fwd), qk(bwd), qk(bwd), vv(bwd), qk(bwd)],
        out_specs=[vv(fwd), vv(bwd)],
        out_shape=[jax.ShapeDtypeStruct((T, GLA_WIDTH), F32)] * 2,
        scratch_shapes=[pltpu.VMEM((GLA_DV, GLA_QK), F32)] * 2,
        compiler_params=_params("arbitrary", "arbitrary"),
        name="gla",
    )(qg, kg, vg, laf, qg, kg, vg, lab)


def _bias_from_rel(rel, tab_ref, hd):
    n = jnp.abs(rel)
    half = REL_BUCKETS // 2
    neg = jnp.full(rel.shape, tab_ref[0, hd], F32)
    pos = jnp.full(rel.shape, tab_ref[half, hd], F32)
    for j, t in enumerate(BUCKET_STEPS):
        ge = n >= t
        neg = jnp.where(ge, tab_ref[j + 1, hd], neg)
        pos = jnp.where(ge, tab_ref[half + j + 1, hd], pos)
    return jnp.where(rel > 0, pos, neg)


KEY_CHUNK = 128
COL_TILE = 256
CLS_LEFT, CLS_RIGHT, CLS_STRIP0, CLS_GENERIC = 0, 1, 2, 99
STEP_LEFT, STEP_RIGHT, STEP_MIXED = 0, 1, 2


def _strips_kernel(tab_ref, o_ref, *, tq, n_strips):
    hd = pl.program_id(0)
    kk = lax.broadcasted_iota(jnp.int32, (KEY_CHUNK, tq), 0)
    qq = lax.broadcasted_iota(jnp.int32, (KEY_CHUNK, tq), 1)
    o_ref[0, CLS_LEFT] = jnp.full((KEY_CHUNK, tq), tab_ref[REL_BUCKETS // 2 - 1, hd], F32)
    o_ref[0, CLS_RIGHT] = jnp.full((KEY_CHUNK, tq), tab_ref[REL_BUCKETS - 1, hd], F32)
    for j in range(n_strips):
        o_ref[0, CLS_STRIP0 + j] = _bias_from_rel(KEY_CHUNK * (j - 1) + kk - qq, tab_ref, hd)


def _strips(table, tq, n_strips):
    H = table.shape[1]
    return pl.pallas_call(
        functools.partial(_strips_kernel, tq=tq, n_strips=n_strips),
        grid_spec=pltpu.PrefetchScalarGridSpec(
            num_scalar_prefetch=1, grid=(H,), in_specs=[],
            out_specs=pl.BlockSpec((1, CLS_STRIP0 + n_strips, KEY_CHUNK, tq), lambda h, *_: (h, 0, 0, 0))),
        out_shape=jax.ShapeDtypeStruct((H, CLS_STRIP0 + n_strips, KEY_CHUNK, tq), F32),
        compiler_params=_params("arbitrary"),
        name="bias_strips",
    )(table)


def _diff_finish(acc, l, lq1_ref, lk1_ref, lq2_ref, lk2_ref, nw_ref, o_ref, tq):
    lam = (jnp.exp(jnp.sum(lq1_ref[...] * lk1_ref[...], axis=1, keepdims=True))
           - jnp.exp(jnp.sum(lq2_ref[...] * lk2_ref[...], axis=1, keepdims=True)) + LAMBDA_INIT)
    o = acc[:, :tq] / l[:, :tq] - lam * (acc[:, tq:] / l[:, tq:])
    ms = jnp.mean(o * o, axis=0, keepdims=True)
    nw = jnp.concatenate([nw_ref[...]] * (tq // 128), axis=1)
    o = o * lax.rsqrt(ms + EPS) * nw * (1.0 - LAMBDA_INIT)
    o_ref[0] = o.T.astype(o_ref.dtype)


def _stack_maps(qT):
    row = lax.broadcasted_iota(jnp.int32, qT.shape, 0)
    zero = jnp.zeros_like(qT)
    return jnp.concatenate([jnp.where(row < DIFF_DH, qT, zero), jnp.where(row >= DIFF_DH, qT, zero)], axis=1)


def _diff_fast_kernel(chunkcls_ref, tab_ref,
                      qT_ref, k_ref, vT_ref, strip_ref, lq1_ref, lk1_ref, lq2_ref, lk2_ref, nw_ref,
                      o_ref, qs_ref, ma_ref, mb_ref, acca_ref, accb_ref, sa_ref, sb_ref, smaxa_ref, smaxb_ref,
                      *, tq, tk, n_kt, n_qt):
    hd = pl.program_id(1)
    r0 = pl.program_id(0) * n_qt
    n_ct = 2 * tq // COL_TILE
    n_chunk = tk // KEY_CHUNK
    for qt in range(n_qt):
        qs_ref[qt] = _stack_maps(qT_ref[0, :, qt * tq:(qt + 1) * tq])
    bias_max = functools.reduce(jnp.maximum, [tab_ref[i, hd] for i in range(REL_BUCKETS)])

    def scores_tile(qt, kt, buf, j):
        s_ref, smax_ref = buf
        kblk = k_ref[0, pl.ds(pl.multiple_of(kt * tk, tk), tk), :]
        cols = slice(j * COL_TILE, (j + 1) * COL_TILE)
        s = _dot(kblk, qs_ref[qt, :, cols])
        s_ref[:, cols] = s
        smax_ref[:, cols] = jnp.max(s, axis=0, keepdims=True)

    def chain(qt, kt, buf, j, state):
        s_ref, smax_ref = buf
        m_ref, acc_ref = state
        cols = slice(j * COL_TILE, (j + 1) * COL_TILE)
        q0 = (j * COL_TILE) % tq
        m_old = m_ref[:, cols]
        m_new = jnp.maximum(m_old, smax_ref[:, cols] + bias_max)
        alpha = jnp.exp2(m_old - m_new)
        ps = []
        for c in range(n_chunk):
            rows = slice(c * KEY_CHUNK, (c + 1) * KEY_CHUNK)
            bias = strip_ref[0, chunkcls_ref[r0 + qt, kt * n_chunk + c], :, q0:q0 + COL_TILE]
            ps.append(jnp.exp2((s_ref[rows, cols] + bias) - m_new).astype(BF16))
        vblk = vT_ref[0, 0, :, pl.ds(pl.multiple_of(kt * tk, tk), tk)]
        acc_ref[:, cols] = alpha * acc_ref[:, cols] + _dot(vblk, jnp.concatenate(ps, axis=0))
        m_ref[:, cols] = m_new

    def step(qt, kt, cur, nxt, nxt_tile, state):
        for j in range(n_ct):
            if nxt_tile is not None:
                scores_tile(nxt_tile[0], nxt_tile[1], nxt, j)
            chain(qt, kt, cur, j, state)

    buf_a = (sa_ref, smaxa_ref)
    buf_b = (sb_ref, smaxb_ref)

    def finish(qt, state):
        acc = state[1][...]
        _diff_finish(acc[:DIFF_DV], acc[DIFF_DV:DIFF_DV + 1], lq1_ref, lk1_ref, lq2_ref, lk2_ref, nw_ref,
                     o_ref.at[:, pl.ds(pl.multiple_of(qt * tq, tq), tq), :], tq)

    def query_tile(qt, handoff, state, prev):
        m_ref, acc_ref = state
        m_ref[...] = jnp.full(m_ref.shape, -1e30, F32)
        acc_ref[...] = jnp.zeros(acc_ref.shape, F32)
        if prev is not None:
            finish(*prev)
        for kt in range(n_kt):
            cur, nxt = (buf_a, buf_b) if kt % 2 == 0 else (buf_b, buf_a)
            if kt + 1 < n_kt:
                nxt_tile = (qt, kt + 1)
            else:
                nxt_tile = (qt + 1, 0) if handoff else None
            step(qt, kt, cur, nxt, nxt_tile, state)

    state_a = (ma_ref, acca_ref)
    state_b = (mb_ref, accb_ref)
    for j in range(n_ct):
        scores_tile(0, 0, buf_a, j)
    query_tile(0, True, state_a, None)

    def body(pp, carry):
        q1 = 2 * pp + 1
        query_tile(q1, True, state_b, (q1 - 1, state_a))
        query_tile(q1 + 1, True, state_a, (q1, state_b))
        return carry

    lax.fori_loop(0, (n_qt - 2) // 2, body, 0)
    query_tile(n_qt - 1, False, state_b, (n_qt - 2, state_a))
    finish(n_qt - 1, state_b)


def _diff_kernel(stepcls_ref, chunkcls_ref, tab_ref,
                 qT_ref, k_ref, vT_ref, strip_ref, posq_ref, posk_ref, lq1_ref, lk1_ref, lq2_ref, lk2_ref, nw_ref,
                 o_ref, m_ref, l_ref, acc_ref, sa_ref, sb_ref, smaxa_ref, smaxb_ref, bias_ref,
                 *, tq, tk, n_kt, n_strips):
    hd = pl.program_id(1)
    r = pl.program_id(0) * pl.num_programs(2) + pl.program_id(2)
    half = REL_BUCKETS // 2
    n_ct = 2 * tq // COL_TILE
    n_chunk = tk // KEY_CHUNK

    qstack = _stack_maps(qT_ref[0])

    m_ref[...] = jnp.full(m_ref.shape, -1e30, F32)
    l_ref[...] = jnp.zeros(l_ref.shape, F32)
    acc_ref[...] = jnp.zeros(acc_ref.shape, F32)

    bias_max = functools.reduce(jnp.maximum, [tab_ref[i, hd] for i in range(REL_BUCKETS)])

    def scores_tile(kt, buf, j):
        s_ref, smax_ref = buf
        kblk = k_ref[0, pl.ds(pl.multiple_of(kt * tk, tk), tk), :]
        cols = slice(j * COL_TILE, (j + 1) * COL_TILE)
        s = _dot(kblk, qstack[:, cols])
        s_ref[:, cols] = s
        smax_ref[:, cols] = jnp.max(s, axis=0, keepdims=True)

    def scores(kt, buf):
        for j in range(n_ct):
            scores_tile(kt, buf, j)

    def build_bias(kt):
        posq = posq_ref[0]
        for c in range(n_chunk):
            cc = chunkcls_ref[r, kt * n_chunk + c]
            rows = slice(c * KEY_CHUNK, (c + 1) * KEY_CHUNK)

            @pl.when(cc == CLS_LEFT)
            def _():
                bias_ref[rows, :] = jnp.full((KEY_CHUNK, tq), tab_ref[half - 1, hd], F32)

            @pl.when(cc == CLS_RIGHT)
            def _():
                bias_ref[rows, :] = jnp.full((KEY_CHUNK, tq), tab_ref[REL_BUCKETS - 1, hd], F32)

            @pl.when(jnp.logical_and(cc >= CLS_STRIP0, cc < CLS_STRIP0 + n_strips))
            def _():
                bias_ref[rows, :] = strip_ref[0, cc]

            @pl.when(cc == CLS_GENERIC)
            def _():
                k0 = pl.multiple_of(kt * tk + c * KEY_CHUNK, KEY_CHUNK)
                pk = posk_ref[0, pl.ds(k0, KEY_CHUNK), :]
                pk = jnp.concatenate([pk] * (tq // 128), axis=1)
                bias_ref[rows, :] = _bias_from_rel(pk - posq, tab_ref, hd)

    def softmax_pv(kt, cur, const, nxt):
        s_ref, smax_ref = cur
        vblk = vT_ref[0, 0, :DIFF_DV, pl.ds(pl.multiple_of(kt * tk, tk), tk)]
        for j in range(n_ct):
            if nxt is not None:
                scores_tile(kt + 1, nxt, j)
            cols = slice(j * COL_TILE, (j + 1) * COL_TILE)
            shift = bias_max if const is None else const
            m_old = m_ref[:, cols]
            m_new = jnp.maximum(m_old, smax_ref[:, cols] + shift)
            alpha = jnp.exp2(m_old - m_new)
            if const is None:
                q0 = (j * COL_TILE) % tq
                p = jnp.exp2((s_ref[:, cols] + bias_ref[:, q0:q0 + COL_TILE]) - m_new)
            else:
                p = jnp.exp2(s_ref[:, cols] - (m_new - shift))
            l_ref[:, cols] = alpha * l_ref[:, cols] + jnp.sum(p, axis=0, keepdims=True)
            acc_ref[:, cols] = alpha * acc_ref[:, cols] + _dot(vblk, p.astype(BF16))
            m_ref[:, cols] = m_new

    def step(kt, cur, nxt):
        cls = stepcls_ref[r, kt]

        @pl.when(cls == STEP_MIXED)
        def _():
            build_bias(kt)
            softmax_pv(kt, cur, None, nxt)

        @pl.when(cls != STEP_MIXED)
        def _():
            const = jnp.where(cls == STEP_LEFT, tab_ref[half - 1, hd], tab_ref[REL_BUCKETS - 1, hd])
            softmax_pv(kt, cur, const, nxt)

    buf_a = (sa_ref, smaxa_ref)
    buf_b = (sb_ref, smaxb_ref)
    scores(0, buf_a)

    def pair(kk, carry):
        step(2 * kk, buf_a, buf_b)
        step(2 * kk + 1, buf_b, buf_a)
        return carry

    lax.fori_loop(0, n_kt // 2 - 1, pair, 0)
    step(n_kt - 2, buf_a, buf_b)
    step(n_kt - 1, buf_b, None)

    _diff_finish(acc_ref[...], l_ref[...], lq1_ref, lk1_ref, lq2_ref, lk2_ref, nw_ref, o_ref, tq)


def _classify(positions, tq, tk, n_strips):
    B, L = positions.shape
    nq, nc = L // tq, L // KEY_CHUNK
    pq = positions.reshape(B, nq, tq)
    pk = positions.reshape(B, nc, KEY_CHUNK)
    qlo, qhi, q0 = pq.min(axis=2), pq.max(axis=2), pq[:, :, 0]
    klo, khi, k0 = pk.min(axis=2), pk.max(axis=2), pk[:, :, 0]
    q_run = jnp.all(pq == q0[:, :, None] + jnp.arange(tq, dtype=positions.dtype), axis=2)
    k_run = jnp.all(pk == k0[:, :, None] + jnp.arange(KEY_CHUNK, dtype=positions.dtype), axis=2)
    left = (qlo[:, :, None] - khi[:, None, :]) >= FAR_DIST
    right = (klo[:, None, :] - qhi[:, :, None]) >= FAR_DIST
    d = k0[:, None, :] - q0[:, :, None]
    j = d // KEY_CHUNK + 1
    strip_ok = (q_run[:, :, None] & k_run[:, None, :] & (d % KEY_CHUNK == 0) & (j >= 0) & (j < n_strips))
    cls = jnp.where(left, CLS_LEFT, jnp.where(right, CLS_RIGHT, jnp.where(strip_ok, CLS_STRIP0 + j, CLS_GENERIC)))
    cls = cls.astype(jnp.int32)
    per_step = cls.reshape(B, nq, L // tk, tk // KEY_CHUNK)
    step = jnp.where(jnp.all(per_step == CLS_LEFT, axis=3), STEP_LEFT,
                     jnp.where(jnp.all(per_step == CLS_RIGHT, axis=3), STEP_RIGHT, STEP_MIXED)).astype(jnp.int32)
    return step.reshape(B * nq, L // tk), cls.reshape(B * nq, nc)


def _diffattn(qT, kd, vT, positions, table, lq1, lk1, lq2, lk2, nw, tq, tk):
    B, _, L = qT.shape
    nq, nk = L // tq, L // tk
    n_strips = tq // KEY_CHUNK + 2
    n_cls = CLS_STRIP0 + n_strips
    table2 = table * LOG2E
    strips = _strips(table2, tq, n_strips)
    stepcls, chunkcls = _classify(positions, tq, tk, n_strips)
    nw_b = jnp.broadcast_to(nw.reshape(DIFF_DV, 1), (DIFF_DV, 128))
    vecs = [a.reshape(1, DIFF_DH) for a in (lq1, lk1, lq2, lk2)]
    full2 = lambda b, h, i, *_: (0, 0)
    q_spec = pl.BlockSpec((1, DIFF_DV, tq), lambda b, h, i, *_: (b, h, i))
    k_spec = pl.BlockSpec((1, L, DIFF_DV), lambda b, h, i, *_: (b, 0, h))
    v_spec = pl.BlockSpec((1, 1, DIFF_DV + ONES_ROWS, L), lambda b, h, i, *_: (b, h, 0, 0))
    strip_spec = pl.BlockSpec((1, n_cls, KEY_CHUNK, tq), lambda b, h, i, *_: (h, 0, 0, 0))
    tail_specs = [pl.BlockSpec((1, DIFF_DH), full2)] * 4 + [pl.BlockSpec((DIFF_DV, 128), full2)]
    out_spec = pl.BlockSpec((1, tq, DIFF_DV), lambda b, h, i, *_: (b, i, h))
    out_shape = jax.ShapeDtypeStruct((B, L, DIFF_WIDTH), BF16)
    s_buf = pltpu.VMEM((tk, 2 * tq), F32)
    row_buf = pltpu.VMEM((1, 2 * tq), F32)
    acc_buf = pltpu.VMEM((DIFF_DV + ONES_ROWS, 2 * tq), F32)

    def fast(_):
        one = lambda b, h, *_: (0, 0)
        return pl.pallas_call(
            functools.partial(_diff_fast_kernel, tq=tq, tk=tk, n_kt=nk, n_qt=nq),
            grid_spec=pltpu.PrefetchScalarGridSpec(
                num_scalar_prefetch=2, grid=(B, DIFF_HEADS),
                in_specs=[pl.BlockSpec((1, DIFF_DV, L), lambda b, h, *_: (b, h, 0)),
                          pl.BlockSpec((1, L, DIFF_DV), lambda b, h, *_: (b, 0, h)),
                          pl.BlockSpec((1, 1, DIFF_DV + ONES_ROWS, L), lambda b, h, *_: (b, h, 0, 0)),
                          pl.BlockSpec((1, n_cls, KEY_CHUNK, tq), lambda b, h, *_: (h, 0, 0, 0))]
                + [pl.BlockSpec((1, DIFF_DH), one)] * 4 + [pl.BlockSpec((DIFF_DV, 128), one)],
                out_specs=pl.BlockSpec((1, L, DIFF_DV), lambda b, h, *_: (b, 0, h)),
                scratch_shapes=[pltpu.VMEM((nq, DIFF_DV, 2 * tq), BF16),
                                row_buf, row_buf, acc_buf, acc_buf, s_buf, s_buf, row_buf, row_buf]),
            out_shape=out_shape,
            compiler_params=_params("arbitrary", "arbitrary"),
            name="diffattn",
        )(chunkcls, table2, qT, kd, vT, strips, *vecs, nw_b)

    def general(_):
        posq = positions.reshape(B, 1, L)
        posk = jnp.broadcast_to(positions[:, :, None], (B, L, 128))
        return pl.pallas_call(
            functools.partial(_diff_kernel, tq=tq, tk=tk, n_kt=nk, n_strips=n_strips),
            grid_spec=pltpu.PrefetchScalarGridSpec(
                num_scalar_prefetch=3, grid=(B, DIFF_HEADS, nq),
                in_specs=[q_spec, k_spec, v_spec, strip_spec,
                          pl.BlockSpec((1, 1, tq), lambda b, h, i, *_: (b, 0, i)),
                          pl.BlockSpec((1, L, 128), lambda b, h, i, *_: (b, 0, 0))] + tail_specs,
                out_specs=out_spec,
                scratch_shapes=[row_buf, row_buf, pltpu.VMEM((DIFF_DV, 2 * tq), F32), s_buf, s_buf,
                                row_buf, row_buf, pltpu.VMEM((tk, tq), F32)]),
            out_shape=out_shape,
            compiler_params=_params("arbitrary", "arbitrary", "arbitrary"),
            name="diffattn_general",
        )(stepcls, chunkcls, table2, qT, kd, vT, strips, posq, posk, *vecs, nw_b)

    return lax.cond(jnp.all(chunkcls != CLS_GENERIC), fast, general, 0)


def _outproj_kernel(of_ref, ob_ref, rg_ref, gnw_ref, od_ref, wa_ref, wb_ref, ga_ref, x_ref, o_ref):
    o = of_ref[...] + ob_ref[...]
    parts = []
    for hd in range(GLA_HEADS):
        oh = o[:, hd * GLA_DV:(hd + 1) * GLA_DV]
        ms = jnp.mean(oh * oh, axis=-1, keepdims=True)
        parts.append(oh * lax.rsqrt(ms + EPS))
    r = rg_ref[...]
    oa = jnp.concatenate(parts, axis=1) * gnw_ref[...] * (r * _sigmoid(r))
    mixed = _dot(oa.astype(BF16), wa_ref[...]) + _dot(od_ref[...], wb_ref[...])
    o_ref[...] = x_ref[...] + ga_ref[0] * mixed


def _outproj(of, ob, rg, gnw, od, wa, wb, ga, x2d, L, tm):
    T, D = x2d.shape
    per_b = L // tm
    row = lambda i: (i, 0)
    full = lambda i: (0, 0)
    bvec = lambda i: (i // per_b, 0, 0)
    return pl.pallas_call(
        _outproj_kernel,
        grid=(T // tm,),
        in_specs=[pl.BlockSpec((tm, GLA_WIDTH), row), pl.BlockSpec((tm, GLA_WIDTH), row),
                  pl.BlockSpec((tm, GLA_WIDTH), row), pl.BlockSpec((1, GLA_WIDTH), full),
                  pl.BlockSpec((tm, DIFF_WIDTH), row),
                  pl.BlockSpec(wa.shape, full), pl.BlockSpec(wb.shape, full),
                  pl.BlockSpec((1, 1, D), bvec), pl.BlockSpec((tm, D), row)],
        out_specs=pl.BlockSpec((tm, D), row),
        out_shape=jax.ShapeDtypeStruct((T, D), F32),
        compiler_params=_params("arbitrary"),
        name="outproj",
    )(of, ob, rg, gnw, od, wa, wb, ga, x2d)


HALO = 8
FF_CHUNK = 256


def _ffn_kernel(x_ref, xp_ref, xn_ref, sh_ref, sc_ref, gf_ref, nw_ref, wup_ref, cw_ref, cb_ref, wdn_ref, fw_ref,
                o_ref, acc_ref, *, tm, per_b):
    i = pl.program_id(0)
    first = (i % per_b) == 0
    last = (i % per_b) == per_b - 1
    nw, sh, sc = nw_ref[...], sh_ref[0], sc_ref[0]
    x = x_ref[...]
    h_mid = _rms_mod(x, nw, sh, sc)
    h_prev = jnp.where(first, 0.0, _rms_mod(xp_ref[...], nw, sh, sc))
    h_next = jnp.where(last, 0.0, _rms_mod(xn_ref[...], nw, sh, sc))
    h = jnp.concatenate([h_prev, h_mid, h_next], axis=0).astype(BF16)
    rows = tm + 2 * HALO

    def conv(u, col):
        w = cw_ref[:, col:col + FF_CHUNK]
        up = pltpu.roll(u, 1, 0)[HALO:HALO + tm]
        un = pltpu.roll(u, rows - 1, 0)[HALO:HALO + tm]
        return (w[0:1] * up + w[1:2] * u[HALO:HALO + tm] + w[2:3] * un) + cb_ref[:, col:col + FF_CHUNK]

    def up(n):
        cg, cv = n * FF_CHUNK, D_FF + n * FF_CHUNK
        return _dot(h, wup_ref[:, cg:cg + FF_CHUNK]), _dot(h, wup_ref[:, cv:cv + FF_CHUNK])

    n_chunks = D_FF // FF_CHUNK
    u_gate, u_val = up(0)
    for n in range(n_chunks):
        cg, cv = n * FF_CHUNK, D_FF + n * FF_CHUNK
        if n + 1 < n_chunks:
            nxt = up(n + 1)
        gate = conv(u_gate, cg)
        val = conv(u_val, cv)
        a = (gate * _sigmoid(gate) * val).astype(BF16)
        part = _dot(a, wdn_ref[cg:cg + FF_CHUNK, :])
        if n == 0:
            acc_ref[...] = part
        else:
            acc_ref[...] += part
        if n + 1 < n_chunks:
            u_gate, u_val = nxt
    y = x + gf_ref[0] * acc_ref[...]
    ms = jnp.mean(y * y, axis=-1, keepdims=True)
    o_ref[...] = y * lax.rsqrt(ms + EPS) * fw_ref[...]


def _ffn(x1, sh, sc, gf, nw, wup, cw, cb, wdn, fw, L, tm):
    T, D = x1.shape
    per_b = L // tm
    hb = tm // HALO
    n_halo = T // HALO
    row = lambda i: (i, 0)
    full = lambda i: (0, 0)
    bvec = lambda i: (i // per_b, 0, 0)
    return pl.pallas_call(
        functools.partial(_ffn_kernel, tm=tm, per_b=per_b),
        grid=(T // tm,),
        in_specs=[pl.BlockSpec((tm, D), row),
                  pl.BlockSpec((HALO, D), lambda i: (jnp.maximum(i * hb - 1, 0), 0)),
                  pl.BlockSpec((HALO, D), lambda i: (jnp.minimum((i + 1) * hb, n_halo - 1), 0)),
                  pl.BlockSpec((1, 1, D), bvec), pl.BlockSpec((1, 1, D), bvec), pl.BlockSpec((1, 1, D), bvec),
                  pl.BlockSpec((1, D), full),
                  pl.BlockSpec(wup.shape, full), pl.BlockSpec(cw.shape, full), pl.BlockSpec(cb.shape, full),
                  pl.BlockSpec(wdn.shape, full), pl.BlockSpec((1, D), full)],
        out_specs=pl.BlockSpec((tm, D), row),
        out_shape=jax.ShapeDtypeStruct((T, D), F32),
        scratch_shapes=[pltpu.VMEM((tm, D), F32)],
        compiler_params=_params("arbitrary"),
        name="ffn",
    )(x1, x1, x1, sh, sc, gf, nw, wup, cw, cb, wdn, fw)


def kernel(x, c, positions, w_ada, b_ada, attn_norm_w, w_in, gla_dec_w_fwd, gla_dec_b_fwd, gla_dec_w_bwd,
           gla_dec_b_bwd, gla_norm_w, diff_lambda_q1, diff_lambda_k1, diff_lambda_q2, diff_lambda_k2,
           diff_norm_w, rel_bias_table, w_out, ffn_norm_w, w_up, conv_w, conv_b, w_down, final_norm_w):
    B, L, D = x.shape
    T = B * L
    tiles = _tiles(L)
    x2d = x.reshape(T, D)

    mod = _ada(c, w_ada[0], b_ada[0])
    sh_a, sc_a, g_a, sh_f, sc_f, g_f = [m.reshape(B, 1, D) for m in jnp.split(mod, N_MOD, axis=-1)]

    o_lr = 2 * GLA_QK + 2 * GLA_WIDTH
    o_d = o_lr + 2 * GLA_RANK
    w = w_in[0]
    wg = w[:, :o_lr].astype(BF16)
    wlr = jnp.pad(w[:, o_lr:o_d], ((0, 0), (0, 128 - 2 * GLA_RANK))).astype(BF16)
    wd = w[:, o_d:].astype(BF16)
    decw = jnp.zeros((128, 2 * GLA_QK), F32)
    decw = decw.at[:GLA_RANK, :GLA_QK].set(gla_dec_w_fwd[0]).at[GLA_RANK:2 * GLA_RANK, GLA_QK:].set(gla_dec_w_bwd[0])
    decb = jnp.concatenate([gla_dec_b_fwd[0], gla_dec_b_bwd[0]]).reshape(1, 2 * GLA_QK)

    qg, kg, vg, rg, laf, lab, qd, kd, vd = _inproj(
        x2d, sh_a, sc_a, attn_norm_w[0].reshape(1, D), wg, wlr, wd, decw.astype(BF16), decb, L, tiles.proj_rows)

    o_f, o_b = _gla(qg, kg, vg, laf, lab, B, L, tiles.gla_block)

    o_d = _diffattn(qd, kd.reshape(B, L, DIFF_WIDTH), vd, positions, rel_bias_table,
                    diff_lambda_q1[0], diff_lambda_k1[0], diff_lambda_q2[0], diff_lambda_k2[0],
                    diff_norm_w[0], tiles.attn_q, tiles.attn_k)

    wo = w_out[0].astype(BF16)
    x1 = _outproj(o_f, o_b, rg, gla_norm_w[0].reshape(1, GLA_WIDTH), o_d.reshape(T, DIFF_WIDTH),
                  wo[:GLA_WIDTH], wo[GLA_WIDTH:], g_a, x2d, L, tiles.proj_rows)

    out = _ffn(x1, sh_f, sc_f, g_f, ffn_norm_w[0].reshape(1, D), w_up[0].astype(BF16), conv_w[0],
               conv_b[0].reshape(1, 2 * D_FF), w_down[0].astype(BF16), final_norm_w.reshape(1, D), L,
               tiles.ffn_rows)
    return out.reshape(B, L, D)
```
